```python
import math
import numpy as np
import jax
import jax.numpy as jnp
from jax import lax

D_MODEL = 1024
BATCH = 16
SEQ = 2048
DEPTH = 2
DEC_BATCH = 8
DEC_SEQ = 8192
PAST_LEN = 128

GRID_W = 64
CHUNK = 128
Q_BLOCK = 128
ROPE_THETA = 10000.0
EPS = 1e-6
D_MIX = D_MODEL
GROUP_W = D_MIX // 4
D_FF = 4 * D_MODEL

SSD_HEADS = 4
SSD_HEAD_DIM = GROUP_W // SSD_HEADS
SSD_D_INNER = GROUP_W
SSD_GROUPS = 2
SSD_STATE = 128
SSD_CONV_W = 5
SSD_XBC = SSD_D_INNER + 2 * SSD_GROUPS * SSD_STATE

GQA_HEADS = 4
GQA_KV_HEADS = 2
GQA_HEAD_DIM = GROUP_W // GQA_HEADS

GLA_HEADS = 4
GLA_DV = GROUP_W // GLA_HEADS
GLA_DK = GLA_DV // 2
GLA_LOWRANK = 16
GLA_TAU = 16.0

MLA_HEADS = 4
MLA_Q_LORA = 256
MLA_KV_LORA = 128
MLA_NOPE = 64
MLA_ROPE = 32
MLA_V = GROUP_W // MLA_HEADS
MLA_QK = MLA_NOPE + MLA_ROPE

IN_SIZES = (SSD_D_INNER, SSD_XBC, 2 * SSD_HEADS,
            GQA_HEADS * GQA_HEAD_DIM, GQA_KV_HEADS * GQA_HEAD_DIM, GQA_KV_HEADS * GQA_HEAD_DIM,
            GLA_HEADS * GLA_DK, GLA_HEADS * GLA_DK, GLA_HEADS * GLA_DV, GLA_HEADS * GLA_DV, 2 * GLA_LOWRANK,
            MLA_Q_LORA, MLA_KV_LORA, MLA_ROPE)
D_IN = sum(IN_SIZES)

kernel_name = 'hybrid_bidir_encoder_hymba4'


def rmsnorm(x, g):
    xf = x.astype(jnp.float32)
    y = xf * lax.rsqrt(jnp.mean(xf * xf, axis=-1, keepdims=True) + EPS)
    return (y * g.astype(jnp.float32)).astype(x.dtype)


def _flip(a):
    return jnp.flip(a, axis=1)


def _rotate_block(x, pos):
    n = x.shape[-1] // 2
    inv_freq = ROPE_THETA ** (-jnp.arange(n, dtype=jnp.float32) / n)
    ang = pos.astype(jnp.float32)[:, None] * inv_freq[None, :]
    cos = jnp.cos(ang)[:, None, :]
    sin = jnp.sin(ang)[:, None, :]
    xf = x.astype(jnp.float32)
    x1, x2 = xf[..., :n], xf[..., n:]
    return jnp.concatenate([x1 * cos - x2 * sin, x1 * sin + x2 * cos], axis=-1)


def axial_rope(x, row_pos, col_pos):
    half = x.shape[-1] // 2
    return jnp.concatenate([_rotate_block(x[..., :half], row_pos),
                            _rotate_block(x[..., half:], col_pos)], axis=-1).astype(x.dtype)


def block_attention(q, k, v, scale):
    b, t, g, r, d = q.shape
    nb = t // Q_BLOCK
    qb = q.reshape(b, nb, Q_BLOCK, g, r, d).transpose(1, 0, 2, 3, 4, 5)

    def one_block(qi):
        s = jnp.einsum('bqgrd,bkgd->bgrqk', qi, k, preferred_element_type=jnp.float32) * scale
        p = jax.nn.softmax(s, axis=-1)
        return jnp.einsum('bgrqk,bkge->bqgre', p.astype(v.dtype), v)

    o = lax.map(one_block, qb)
    return o.transpose(1, 0, 2, 3, 4, 5).reshape(b, t, g * r * v.shape[-1])


def centred_dwconv(x, w, bias):
    c = x.shape[-1]
    pad = SSD_CONV_W // 2
    y = lax.conv_general_dilated(x, w[:, None, :].astype(x.dtype), window_strides=(1,),
                                 padding=[(pad, pad)], dimension_numbers=('NWC', 'WIO', 'NWC'),
                                 feature_group_count=c)
    return y + bias.astype(x.dtype)


def ssd_direction(xs, dt, a, bm, cm):
    b, t, h, p = xs.shape
    g, n = bm.shape[2], bm.shape[3]
    r = h // g
    c = t // CHUNK
    x = xs.astype(jnp.float32).reshape(b, c, CHUNK, g, r, p)
    dtc = dt.astype(jnp.float32).reshape(b, c, CHUNK, g, r)
    bc = bm.astype(jnp.float32).reshape(b, c, CHUNK, g, n)
    cc = cm.astype(jnp.float32).reshape(b, c, CHUNK, g, n)
    acs = jnp.cumsum(dtc * a.astype(jnp.float32).reshape(g, r), axis=2)
    mask = jnp.tril(jnp.ones((CHUNK, CHUNK), dtype=bool))[:, :, None, None]
    seg = acs[:, :, :, None] - acs[:, :, None, :]
    decay = jnp.exp(jnp.where(mask, seg, -jnp.inf))
    xdt = x * dtc[..., None]
    scores = jnp.einsum('bclgn,bcsgn->bclsg', cc, bc)
    y_diag = jnp.einsum('bclsg,bclsgr,bcsgrp->bclgrp', scores, decay, xdt)
    end_decay = jnp.exp(acs[:, :, -1:] - acs)
    states = jnp.einsum('bcsgn,bcsgr,bcsgrp->bcgrpn', bc, end_decay, xdt)
    chunk_decay = jnp.exp(acs[:, :, -1])

    def step(h_prev, inp):
        st, dec = inp
        return h_prev * dec[..., None, None] + st, h_prev

    h0 = jnp.zeros((b, g, r, p, n), jnp.float32)
    _, h_start = lax.scan(step, h0, (states.transpose(1, 0, 2, 3, 4, 5), chunk_decay.transpose(1, 0, 2, 3)))
    h_start = h_start.transpose(1, 0, 2, 3, 4, 5)
    y_off = jnp.einsum('bclgn,bcgrpn,bclgr->bclgrp', cc, h_start, jnp.exp(acs))
    return (y_diag + y_off).reshape(b, t, h, p)


def gla_direction(q, k, v, logg):
    b, t, h, dk = q.shape
    dv = v.shape[-1]
    c = t // CHUNK
    q = q.reshape(b, c, CHUNK, h, dk)
    k = k.reshape(b, c, CHUNK, h, dk)
    v = v.reshape(b, c, CHUNK, h, dv)
    gcs = jnp.cumsum(logg.reshape(b, c, CHUNK, h, dk), axis=2)
    qg = q * jnp.exp(gcs)
    kg = k * jnp.exp(-gcs)
    mask = jnp.tril(jnp.ones((CHUNK, CHUNK), dtype=bool))
    att = jnp.where(mask, jnp.einsum('bclhk,bcshk->bchls', qg, kg), 0.0)
    o_intra = jnp.einsum('bchls,bcshv->bclhv', att, v)
    g_last = gcs[:, :, -1]
    states = jnp.einsum('bcshk,bcshv->bchkv', k * jnp.exp(g_last[:, :, None] - gcs), v)

    def step(s_prev, inp):
        st, dec = inp
        return s_prev * dec[..., None] + st, s_prev

    s0 = jnp.zeros((b, h, dk, dv), jnp.float32)
    _, s_start = lax.scan(step, s0, (states.transpose(1, 0, 2, 3, 4), jnp.exp(g_last).transpose(1, 0, 2, 3)))
    s_start = s_start.transpose(1, 0, 2, 3, 4)
    o_inter = jnp.einsum('bclhk,bchkv->bclhv', qg, s_start)
    return (o_intra + o_inter).reshape(b, t, h, dv)


def mixer_block(h, row_pos, col_pos, w_in, ssd_conv_w, ssd_conv_b, ssd_dt_bias, ssd_a_log, ssd_d,
                ssd_norm_g, gqa_q_norm_g, gqa_k_norm_g, gla_gate_w2, gla_gate_b, gla_norm_g,
                mla_q_norm_g, mla_w_uq, mla_kv_norm_g, mla_w_ukv, w_out):
    b, t, _ = h.shape
    f32 = jnp.float32
    proj = h @ w_in
    splits = np.cumsum(IN_SIZES)[:-1].tolist()
    (ssd_z, ssd_xbc, ssd_dt, gqa_q, gqa_k, gqa_v, gla_q, gla_k, gla_v, gla_r, gla_lr,
     mla_cq, mla_ckv, mla_kr) = jnp.split(proj, splits, axis=-1)

    xbc = jax.nn.silu(centred_dwconv(ssd_xbc, ssd_conv_w, ssd_conv_b))
    xs, bm, cm = jnp.split(xbc, [SSD_D_INNER, SSD_D_INNER + SSD_GROUPS * SSD_STATE], axis=-1)
    xs = xs.reshape(b, t, SSD_HEADS, SSD_HEAD_DIM)
    bm = bm.reshape(b, t, SSD_GROUPS, SSD_STATE)
    cm = cm.reshape(b, t, SSD_GROUPS, SSD_STATE)
    dt = jax.nn.softplus(ssd_dt.astype(f32).reshape(b, t, 2, SSD_HEADS) + ssd_dt_bias.astype(f32))
    a = -jnp.exp(ssd_a_log.astype(f32))
    y_fwd = ssd_direction(xs, dt[:, :, 0], a[0], bm, cm)
    y_bwd = _flip(ssd_direction(_flip(xs), _flip(dt[:, :, 1]), a[1], _flip(bm), _flip(cm)))
    y = y_fwd + y_bwd + ssd_d.astype(f32)[:, None] * xs.astype(f32)
    y = y.reshape(b, t, SSD_D_INNER)
    ssd_out = rmsnorm(y * jax.nn.silu(ssd_z.astype(f32)), ssd_norm_g)

    q = axial_rope(rmsnorm(gqa_q.reshape(b, t, GQA_HEADS, GQA_HEAD_DIM), gqa_q_norm_g), row_pos, col_pos)
    k = axial_rope(rmsnorm(gqa_k.reshape(b, t, GQA_KV_HEADS, GQA_HEAD_DIM), gqa_k_norm_g), row_pos, col_pos)
    v = gqa_v.reshape(b, t, GQA_KV_HEADS, GQA_HEAD_DIM)
    q = q.reshape(b, t, GQA_KV_HEADS, GQA_HEADS // GQA_KV_HEADS, GQA_HEAD_DIM)
    gqa_out = block_attention(q, k, v, GQA_HEAD_DIM ** -0.5)

    lq = gla_q.reshape(b, t, GLA_HEADS, GLA_DK).astype(f32) * GLA_DK ** -0.5
    lk = gla_k.reshape(b, t, GLA_HEADS, GLA_DK).astype(f32)
    lv = gla_v.reshape(b, t, GLA_HEADS, GLA_DV).astype(f32)
    lr = gla_lr.reshape(b, t, 2, GLA_LOWRANK).astype(f32)
    logg = jax.nn.log_sigmoid(jnp.einsum('btdl,dlk->btdk', lr, gla_gate_w2.astype(f32))
                              + gla_gate_b.astype(f32)) / GLA_TAU
    logg = logg.reshape(b, t, 2, GLA_HEADS, GLA_DK)
    o = gla_direction(lq, lk, lv, logg[:, :, 0]) + _flip(
        gla_direction(_flip(lq), _flip(lk), _flip(lv), _flip(logg[:, :, 1])))
    gla_out = rmsnorm(o, gla_norm_g).reshape(b, t, GLA_HEADS * GLA_DV) * jax.nn.silu(gla_r.astype(f32))

    cq = (rmsnorm(mla_cq, mla_q_norm_g) @ mla_w_uq).reshape(b, t, MLA_HEADS, MLA_QK)
    mq = jnp.concatenate([cq[..., :MLA_NOPE], axial_rope(cq[..., MLA_NOPE:], row_pos, col_pos)], axis=-1)
    kv = (rmsnorm(mla_ckv, mla_kv_norm_g) @ mla_w_ukv).reshape(b, t, MLA_HEADS, MLA_NOPE + MLA_V)
    k_rope = axial_rope(mla_kr.reshape(b, t, 1, MLA_ROPE), row_pos, col_pos)
    mk = jnp.concatenate([kv[..., :MLA_NOPE], jnp.broadcast_to(k_rope, (b, t, MLA_HEADS, MLA_ROPE))], axis=-1)
    mv = kv[..., MLA_NOPE:]
    mla_out = block_attention(mq.reshape(b, t, MLA_HEADS, 1, MLA_QK), mk, mv, MLA_QK ** -0.5)

    mix = jnp.concatenate([ssd_out.astype(h.dtype), gqa_out.astype(h.dtype),
                           gla_out.astype(h.dtype), mla_out.astype(h.dtype)], axis=-1)
    return mix @ w_out


def sq_relu_mlp(h, w_ff1, w_ff2):
    return jnp.square(jax.nn.relu(h @ w_ff1)) @ w_ff2


def trunk(x, params):
    (norm1_g, w_in, ssd_conv_w, ssd_conv_b, ssd_dt_bias, ssd_a_log, ssd_d, ssd_norm_g,
     gqa_q_norm_g, gqa_k_norm_g, gla_gate_w2, gla_gate_b, gla_norm_g, mla_q_norm_g, mla_w_uq,
     mla_kv_norm_g, mla_w_ukv, w_out, norm2_g, w_ff1, w_ff2, final_norm_g) = params
    t = x.shape[1]
    rows = t // GRID_W
    row_pos = jnp.repeat(jnp.arange(rows, dtype=jnp.int32), GRID_W)
    col_pos = jnp.tile(jnp.arange(GRID_W, dtype=jnp.int32), rows)
    for i in range(DEPTH):
        h = rmsnorm(x, norm1_g[i])
        x = x + mixer_block(h, row_pos, col_pos, w_in[i], ssd_conv_w[i], ssd_conv_b[i], ssd_dt_bias[i],
                            ssd_a_log[i], ssd_d[i], ssd_norm_g[i], gqa_q_norm_g[i], gqa_k_norm_g[i],
                            gla_gate_w2[i], gla_gate_b[i], gla_norm_g[i], mla_q_norm_g[i], mla_w_uq[i],
                            mla_kv_norm_g[i], mla_w_ukv[i], w_out[i])
        h = rmsnorm(x, norm2_g[i])
        x = x + sq_relu_mlp(h, w_ff1[i], w_ff2[i])
    return rmsnorm(x, final_norm_g)


def setup_inputs(seed: int = 0) -> dict:
    key = jax.random.key(seed)
    ks = iter(jax.random.split(key, 32))
    f32 = jnp.float32
    L = DEPTH

    def nrm(shape, scale):
        return jax.random.normal(next(ks), shape, f32) * scale

    def gain(shape):
        return 1.0 + 0.02 * jax.random.normal(next(ks), shape, f32)

    x_prompt = jax.random.normal(next(ks), (BATCH, SEQ, D_MODEL), f32)
    x_sample = jax.random.normal(next(ks), (DEC_BATCH, DEC_SEQ, D_MODEL), f32)
    norm1_g = gain((L, D_MODEL))
    w_in = nrm((L, D_MODEL, D_IN), D_MODEL ** -0.5)
    ssd_conv_w = nrm((L, SSD_CONV_W, SSD_XBC), SSD_CONV_W ** -0.5)
    ssd_conv_b = nrm((L, SSD_XBC), 0.02)
    dt0 = jnp.exp(jax.random.uniform(next(ks), (L, 2, SSD_HEADS), f32, math.log(1e-3), math.log(1e-1)))
    ssd_dt_bias = dt0 + jnp.log(-jnp.expm1(-dt0))
    ssd_a_log = jnp.log(jax.random.uniform(next(ks), (L, 2, SSD_HEADS), f32, 1.0, 16.0))
    ssd_d = gain((L, SSD_HEADS))
    ssd_norm_g = gain((L, SSD_D_INNER))
    gqa_q_norm_g = gain((L, GQA_HEAD_DIM))
    gqa_k_norm_g = gain((L, GQA_HEAD_DIM))
    gla_gate_w2 = nrm((L, 2, GLA_LOWRANK, GLA_HEADS * GLA_DK), GLA_LOWRANK ** -0.5)
    gla_gate_b = nrm((L, 2, GLA_HEADS * GLA_DK), 0.02)
    gla_norm_g = gain((L, GLA_DV))
    mla_q_norm_g = gain((L, MLA_Q_LORA))
    mla_w_uq = nrm((L, MLA_Q_LORA, MLA_HEADS * MLA_QK), MLA_Q_LORA ** -0.5)
    mla_kv_norm_g = gain((L, MLA_KV_LORA))
    mla_w_ukv = nrm((L, MLA_KV_LORA, MLA_HEADS * (MLA_NOPE + MLA_V)), MLA_KV_LORA ** -0.5)
    w_out = nrm((L, D_MIX, D_MODEL), D_MIX ** -0.5)
    norm2_g = gain((L, D_MODEL))
    w_ff1 = nrm((L, D_MODEL, D_FF), D_MODEL ** -0.5)
    w_ff2 = nrm((L, D_FF, D_MODEL), D_FF ** -0.5)
    final_norm_g = gain((D_MODEL,))
    return {'x_prompt': x_prompt, 'x_sample': x_sample, 'norm1_g': norm1_g, 'w_in': w_in,
            'ssd_conv_w': ssd_conv_w, 'ssd_conv_b': ssd_conv_b, 'ssd_dt_bias': ssd_dt_bias,
            'ssd_a_log': ssd_a_log, 'ssd_d': ssd_d, 'ssd_norm_g': ssd_norm_g,
            'gqa_q_norm_g': gqa_q_norm_g, 'gqa_k_norm_g': gqa_k_norm_g,
            'gla_gate_w2': gla_gate_w2, 'gla_gate_b': gla_gate_b, 'gla_norm_g': gla_norm_g,
            'mla_q_norm_g': mla_q_norm_g, 'mla_w_uq': mla_w_uq, 'mla_kv_norm_g': mla_kv_norm_g,
            'mla_w_ukv': mla_w_ukv, 'w_out': w_out, 'norm2_g': norm2_g, 'w_ff1': w_ff1,
            'w_ff2': w_ff2, 'final_norm_g': final_norm_g}


def reference(x_prompt, x_sample, norm1_g, w_in, ssd_conv_w, ssd_conv_b, ssd_dt_bias, ssd_a_log, ssd_d,
              ssd_norm_g, gqa_q_norm_g, gqa_k_norm_g, gla_gate_w2, gla_gate_b, gla_norm_g,
              mla_q_norm_g, mla_w_uq, mla_kv_norm_g, mla_w_ukv, w_out, norm2_g, w_ff1, w_ff2,
              final_norm_g):
    params = (norm1_g, w_in, ssd_conv_w, ssd_conv_b, ssd_dt_bias, ssd_a_log, ssd_d, ssd_norm_g,
              gqa_q_norm_g, gqa_k_norm_g, gla_gate_w2, gla_gate_b, gla_norm_g, mla_q_norm_g, mla_w_uq,
              mla_kv_norm_g, mla_w_ukv, w_out, norm2_g, w_ff1, w_ff2, final_norm_g)
    y_prompt = trunk(x_prompt, params)
    y_sample = trunk(x_sample, params)
    return (y_prompt, y_sample)
```

```python
import functools

import numpy as np
import jax
import jax.numpy as jnp
from jax import lax
from jax.experimental import pallas as pl
from jax.experimental.pallas import tpu as pltpu

F32 = jnp.float32
BF16 = jnp.bfloat16

D_MODEL = 1024
DEPTH = 2
GRID_W = 64
CHUNK = 128
ROPE_THETA = 10000.0
EPS = 1e-6
GROUP_W = D_MODEL // 4
D_FF = 4 * D_MODEL

SSD_HEADS = 4
SSD_HEAD_DIM = GROUP_W // SSD_HEADS
SSD_GROUPS = 2
SSD_STATE = 128
SSD_CONV_W = 5
SSD_XBC = GROUP_W + 2 * SSD_GROUPS * SSD_STATE

GQA_HEADS = 4
GQA_KV_HEADS = 2
GQA_HEAD_DIM = GROUP_W // GQA_HEADS

GLA_HEADS = 4
GLA_DV = GROUP_W // GLA_HEADS
GLA_DK = GLA_DV // 2
GLA_LOWRANK = 16
GLA_TAU = 16.0

MLA_HEADS = 4
MLA_Q_LORA = 256
MLA_KV_LORA = 128
MLA_NOPE = 64
MLA_ROPE = 32
MLA_V = GROUP_W // MLA_HEADS
MLA_QK = MLA_NOPE + MLA_ROPE

IN_SIZES = (GROUP_W, SSD_XBC, 2 * SSD_HEADS,
            GQA_HEADS * GQA_HEAD_DIM, GQA_KV_HEADS * GQA_HEAD_DIM, GQA_KV_HEADS * GQA_HEAD_DIM,
            GLA_HEADS * GLA_DK, GLA_HEADS * GLA_DK, GLA_HEADS * GLA_DV, GLA_HEADS * GLA_DV, 2 * GLA_LOWRANK,
            MLA_Q_LORA, MLA_KV_LORA, MLA_ROPE)
IN_OFFS = tuple(int(v) for v in np.concatenate([[0], np.cumsum(IN_SIZES)]))

LANE = 128
HALO = 8
VMEM_LIMIT = 52 * 1024 * 1024

SSD_W = SSD_XBC + GROUP_W + 2 * GROUP_W
GQA_W = GQA_HEADS * LANE + GQA_KV_HEADS * LANE + GQA_KV_HEADS * GQA_HEAD_DIM
GLA_W = 2 * GLA_HEADS * GLA_DK + 2 * GLA_HEADS * GLA_DV + LANE
MLA_W = MLA_Q_LORA + MLA_KV_LORA + LANE
PROJ_W = SSD_W + GQA_W + GLA_W + MLA_W


def _in_proj_columns():
    o = IN_OFFS
    cols = []
    cols += list(range(o[1], o[2]))
    cols += list(range(o[0], o[1]))
    for d in range(2):
        for h in range(SSD_HEADS):
            cols += [o[2] + d * SSD_HEADS + h] * SSD_HEAD_DIM
    for h in range(GQA_HEADS):
        cols += list(range(o[3] + h * GQA_HEAD_DIM, o[3] + (h + 1) * GQA_HEAD_DIM)) + [-1] * (LANE - GQA_HEAD_DIM)
    for h in range(GQA_KV_HEADS):
        cols += list(range(o[4] + h * GQA_HEAD_DIM, o[4] + (h + 1) * GQA_HEAD_DIM)) + [-1] * (LANE - GQA_HEAD_DIM)
    cols += list(range(o[5], o[6]))
    cols += list(range(o[6], o[10]))
    cols += list(range(o[10], o[11])) + [-1] * (LANE - 2 * GLA_LOWRANK)
    cols += list(range(o[11], o[13]))
    cols += [-1] * MLA_NOPE + list(range(o[13], o[14])) + [-1] * (LANE - MLA_QK)
    cols = np.asarray(cols, np.int32)
    assert cols.shape[0] == PROJ_W
    return cols


_IN_COLS = _in_proj_columns()


def _gather_cols(w, cols):
    picked = jnp.take(w, jnp.asarray(np.maximum(cols, 0)), axis=1)
    return jnp.where(jnp.asarray(cols >= 0)[None, :], picked, 0.0)


def _params(**kw):
    return pltpu.CompilerParams(vmem_limit_bytes=VMEM_LIMIT, **kw)


def _silu(x):
    return x * (1.0 / (1.0 + jnp.exp(-x)))


def _softplus(x):
    return jnp.maximum(x, 0.0) + jnp.log1p(jnp.exp(-jnp.abs(x)))


def _dot(a, b):
    return jnp.dot(a, b, preferred_element_type=F32)


def _dot_nt(a, b):
    return lax.dot_general(a, b, (((1,), (1,)), ((), ())), preferred_element_type=F32)


def _dot_tn(a, b):
    return lax.dot_general(a, b, (((0,), (0,)), ((), ())), preferred_element_type=F32)


def _split_dot(m_bf16, x):
    x1 = x.astype(BF16)
    r1 = x - x1.astype(F32)
    x2 = r1.astype(BF16)
    x3 = (r1 - x2.astype(F32)).astype(BF16)
    return _dot(m_bf16, x1) + _dot(m_bf16, x2) + _dot(m_bf16, x3)


def _tri(reverse):
    li = lax.broadcasted_iota(jnp.int32, (CHUNK, CHUNK), 0)
    si = lax.broadcasted_iota(jnp.int32, (CHUNK, CHUNK), 1)
    return (si >= li) if reverse else (si <= li)


def _inproj_kernel(x_ref, g_ref, w_ref, ssd_ref, gqa_ref, gla_ref, mla_ref):
    x = x_ref[...]
    ms = jnp.mean(x * x, axis=-1, keepdims=True)
    h = (x * lax.rsqrt(ms + EPS) * g_ref[...]).astype(BF16)
    c0 = 0
    for ref, w in ((ssd_ref, SSD_W), (gqa_ref, GQA_W), (gla_ref, GLA_W), (mla_ref, MLA_W)):
        ref[...] = _dot(h, w_ref[:, c0:c0 + w])
        c0 += w


def _inproj(x2d, g, w_all, tm):
    n = x2d.shape[0]
    out_w = (SSD_W, GQA_W, GLA_W, MLA_W)
    return pl.pallas_call(
        _inproj_kernel,
        grid=(n // tm,),
        in_specs=[pl.BlockSpec((tm, D_MODEL), lambda i: (i, 0)),
                  pl.BlockSpec((1, D_MODEL), lambda i: (0, 0)),
                  pl.BlockSpec((D_MODEL, PROJ_W), lambda i: (0, 0), pipeline_mode=pl.Buffered(1))],
        out_specs=[pl.BlockSpec((tm, w), lambda i: (i, 0)) for w in out_w],
        out_shape=[jax.ShapeDtypeStruct((n, w), F32) for w in out_w],
        compiler_params=_params(dimension_semantics=("arbitrary",)),
        name="inproj",
    )(x2d, g.reshape(1, D_MODEL), w_all)


def _ssd_kernel(xbc_ref, prev_ref, next_ref, z_ref, dtx_ref, cw_ref, cb_ref, dtb_ref, a_ref, d_ref, g_ref,
                out_ref, yf_ref, h_ref, xe_ref, *, nt, tc):
    j = pl.program_id(1)
    tile = jnp.where(j < nt, j, 2 * nt - 1 - j)

    @pl.when((j == 0) | (j == nt))
    def _():
        h_ref[...] = jnp.zeros_like(h_ref)

    xe_ref[HALO:HALO + tc, :] = xbc_ref[0]
    xe_ref[0:HALO, :] = jnp.where(tile > 0, prev_ref[0], 0.0)
    xe_ref[HALO + tc:2 * HALO + tc, :] = jnp.where(tile < nt - 1, next_ref[0], 0.0)
    pad = SSD_CONV_W // 2
    acc = cb_ref[...] + cw_ref[0:1, :] * xe_ref[HALO - pad:HALO - pad + tc, :]
    for w in range(1, SSD_CONV_W):
        acc = acc + cw_ref[w:w + 1, :] * xe_ref[HALO - pad + w:HALO - pad + w + tc, :]
    xbc = _silu(acc)
    xs = xbc[:, :GROUP_W]
    dt = _softplus(dtx_ref[0] + dtb_ref[0])
    a_dt = dt * a_ref[0]
    xdt = xs * dt
    lane_head = lax.broadcasted_iota(jnp.int32, (1, GROUP_W), 1) // SSD_HEAD_DIM

    def chunk(c, reverse):
        sl = slice(c * CHUNK, (c + 1) * CHUNK)
        tri = _tri(reverse)
        acs = _split_dot(tri.astype(BF16), a_dt[sl])
        tot = acs[0:1] if reverse else acs[CHUNK - 1:CHUNK]
        acs_t = acs.T
        x_c = xdt[sl]
        b_c = [xbc[sl, GROUP_W + g * SSD_STATE:GROUP_W + (g + 1) * SSD_STATE].astype(BF16)
               for g in range(SSD_GROUPS)]
        c_c = [xbc[sl, GROUP_W + (SSD_GROUPS + g) * SSD_STATE:GROUP_W + (SSD_GROUPS + g + 1) * SSD_STATE].astype(BF16)
               for g in range(SSD_GROUPS)]
        y = jnp.zeros((CHUNK, GROUP_W), F32)
        for g in range(SSD_GROUPS):
            scores = _dot_nt(c_c[g], b_c[g])
            for r in range(SSD_HEADS // SSD_GROUPS):
                hd = g * (SSD_HEADS // SSD_GROUPS) + r
                col = jnp.broadcast_to(acs[:, hd * SSD_HEAD_DIM:hd * SSD_HEAD_DIM + 1], (CHUNK, CHUNK))
                row = jnp.broadcast_to(acs_t[hd * SSD_HEAD_DIM:hd * SSD_HEAD_DIM + 1, :], (CHUNK, CHUNK))
                decay = jnp.exp(jnp.where(tri, col - row, -jnp.inf))
                x_h = jnp.where(lane_head == hd, x_c, 0.0).astype(BF16)
                y = y + _dot((scores * decay).astype(BF16), x_h)
        h_in = h_ref[...]
        half = GROUP_W // SSD_GROUPS
        y_off = jnp.concatenate([_dot(c_c[g], h_in[:, g * half:(g + 1) * half].astype(BF16))
                                 for g in range(SSD_GROUPS)], axis=1)
        y = y + y_off * jnp.exp(acs)
        x_d = (x_c * jnp.exp(tot - acs)).astype(BF16)
        st = jnp.concatenate([_dot_tn(b_c[g], x_d[:, g * half:(g + 1) * half]) for g in range(SSD_GROUPS)], axis=1)
        h_ref[...] = h_in * jnp.exp(tot) + st
        return y

    nc = tc // CHUNK

    @pl.when(j < nt)
    def _():
        for c in range(nc):
            start = pl.multiple_of(tile * tc + c * CHUNK, CHUNK)
            yf_ref[pl.ds(start, CHUNK), :] = chunk(c, False)

    @pl.when(j >= nt)
    def _():
        for c in reversed(range(nc)):
            sl = slice(c * CHUNK, (c + 1) * CHUNK)
            start = pl.multiple_of(tile * tc + c * CHUNK, CHUNK)
            y = yf_ref[pl.ds(start, CHUNK), :] + chunk(c, True) + d_ref[...] * xs[sl]
            gated = y * _silu(z_ref[0, sl, :])
            ms = jnp.mean(gated * gated, axis=-1, keepdims=True)
            out_ref[0, sl, :] = gated * lax.rsqrt(ms + EPS) * g_ref[...]


def _ssd(ssd_g, cw, cb, dtb, a_exp, d_exp, g, tc):
    b, t, _ = ssd_g.shape
    nt = t // tc
    hb = tc // HALO

    def tile_of(j):
        return jnp.where(j < nt, j, 2 * nt - 1 - j)

    kern = functools.partial(_ssd_kernel, nt=nt, tc=tc)
    return pl.pallas_call(
        kern,
        grid=(b, 2 * nt),
        in_specs=[
            pl.BlockSpec((1, tc, SSD_XBC), lambda i, j: (i, tile_of(j), 0)),
            pl.BlockSpec((1, HALO, SSD_XBC), lambda i, j: (i, jnp.maximum(tile_of(j) * hb - 1, 0), 0)),
            pl.BlockSpec((1, HALO, SSD_XBC), lambda i, j: (i, jnp.minimum((tile_of(j) + 1) * hb, t // HALO - 1), 0)),
            pl.BlockSpec((1, tc, GROUP_W), lambda i, j: (i, tile_of(j), SSD_XBC // GROUP_W)),
            pl.BlockSpec((1, tc, GROUP_W), lambda i, j: (i, tile_of(j), SSD_XBC // GROUP_W + 1 + j // nt)),
            pl.BlockSpec((HALO, SSD_XBC), lambda i, j: (0, 0)),
            pl.BlockSpec((1, SSD_XBC), lambda i, j: (0, 0)),
            pl.BlockSpec((1, 1, GROUP_W), lambda i, j: (j // nt, 0, 0)),
            pl.BlockSpec((1, 1, GROUP_W), lambda i, j: (j // nt, 0, 0)),
            pl.BlockSpec((1, GROUP_W), lambda i, j: (0, 0)),
            pl.BlockSpec((1, GROUP_W), lambda i, j: (0, 0)),
        ],
        out_specs=pl.BlockSpec((1, tc, GROUP_W), lambda i, j: (i, jnp.where(j < nt, nt - 1, 2 * nt - 1 - j), 0)),
        out_shape=jax.ShapeDtypeStruct((b, t, GROUP_W), F32),
        scratch_shapes=[pltpu.VMEM((t, GROUP_W), F32),
                        pltpu.VMEM((SSD_STATE, GROUP_W), F32),
                        pltpu.VMEM((tc + 2 * HALO, SSD_XBC), F32)],
        compiler_params=_params(dimension_semantics=("arbitrary", "arbitrary")),
        name="ssd",
    )(ssd_g, ssd_g, ssd_g, ssd_g, ssd_g, cw, cb, dtb, a_exp, d_exp, g)


GLA_QK_W = GLA_HEADS * GLA_DK
GLA_V_W = GLA_HEADS * GLA_DV


def _gla_kernel(q_ref, k_ref, v_ref, r_ref, lr_ref, w2_ref, gb_ref, g_ref, avg_ref,
                out_ref, of_ref, s_ref, *, nt, tc):
    j = pl.program_id(1)
    tile = jnp.where(j < nt, j, 2 * nt - 1 - j)

    @pl.when((j == 0) | (j == nt))
    def _():
        s_ref[...] = jnp.zeros_like(s_ref)

    logit = _dot(lr_ref[0].astype(BF16), w2_ref[0]) + gb_ref[0]
    logg = -_softplus(-logit) * (1.0 / GLA_TAU)
    q = q_ref[0] * (GLA_DK ** -0.5)
    k = k_ref[0]
    v = v_ref[0]
    qk_head = lax.broadcasted_iota(jnp.int32, (1, GLA_QK_W), 1) // GLA_DK
    v_head = lax.broadcasted_iota(jnp.int32, (1, GLA_V_W), 1) // GLA_DV
    blockdiag = (lax.broadcasted_iota(jnp.int32, (GLA_QK_W, GLA_V_W), 0) // GLA_DK
                 == lax.broadcasted_iota(jnp.int32, (GLA_QK_W, GLA_V_W), 1) // GLA_DV)

    def chunk(c, reverse):
        sl = slice(c * CHUNK, (c + 1) * CHUNK)
        tri = _tri(reverse)
        gcs = _split_dot(tri.astype(BF16), logg[sl])
        last = 0 if reverse else CHUNK - 1
        tot = gcs[last:last + 1]
        tot_col = gcs.T[:, last:last + 1]
        qg = q[sl] * jnp.exp(gcs)
        kg = (k[sl] * jnp.exp(-gcs)).astype(BF16)
        v_c = v[sl]
        o = jnp.zeros((CHUNK, GLA_V_W), F32)
        for hd in range(GLA_HEADS):
            q_h = jnp.where(qk_head == hd, qg, 0.0).astype(BF16)
            att = jnp.where(tri, _dot_nt(q_h, kg), 0.0)
            v_h = jnp.where(v_head == hd, v_c, 0.0).astype(BF16)
            o = o + _dot(att.astype(BF16), v_h)
        s_in = s_ref[...]
        o = o + _dot(qg.astype(BF16), s_in.astype(BF16))
        kd = (k[sl] * jnp.exp(tot - gcs)).astype(BF16)
        st = jnp.where(blockdiag, _dot_tn(kd, v_c.astype(BF16)), 0.0)
        s_ref[...] = s_in * jnp.exp(tot_col) + st
        return o

    nc = tc // CHUNK

    @pl.when(j < nt)
    def _():
        for c in range(nc):
            start = pl.multiple_of(tile * tc + c * CHUNK, CHUNK)
            of_ref[pl.ds(start, CHUNK), :] = chunk(c, False)

    @pl.when(j >= nt)
    def _():
        for c in reversed(range(nc)):
            sl = slice(c * CHUNK, (c + 1) * CHUNK)
            start = pl.multiple_of(tile * tc + c * CHUNK, CHUNK)
            o = of_ref[pl.ds(start, CHUNK), :] + chunk(c, True)
            ms = _split_dot_right(o * o, avg_ref[...])
            out_ref[0, sl, :] = o * lax.rsqrt(ms + EPS) * g_ref[...] * _silu(r_ref[0, sl, :])


def _split_dot_right(x, m_bf16):
    x1 = x.astype(BF16)
    r1 = x - x1.astype(F32)
    x2 = r1.astype(BF16)
    x3 = (r1 - x2.astype(F32)).astype(BF16)
    return _dot(x1, m_bf16) + _dot(x2, m_bf16) + _dot(x3, m_bf16)


def _gla(gla_g, w2p, gb, g, avg, tc):
    b, t, _ = gla_g.shape
    nt = t // tc

    def tile_of(j):
        return jnp.where(j < nt, j, 2 * nt - 1 - j)

    kern = functools.partial(_gla_kernel, nt=nt, tc=tc)
    return pl.pallas_call(
        kern,
        grid=(b, 2 * nt),
        in_specs=[
            pl.BlockSpec((1, tc, GLA_QK_W), lambda i, j: (i, tile_of(j), 0)),
            pl.BlockSpec((1, tc, GLA_QK_W), lambda i, j: (i, tile_of(j), 1)),
            pl.BlockSpec((1, tc, GLA_V_W), lambda i, j: (i, tile_of(j), 1)),
            pl.BlockSpec((1, tc, GLA_V_W), lambda i, j: (i, tile_of(j), 2)),
            pl.BlockSpec((1, tc, LANE), lambda i, j: (i, tile_of(j), (2 * GLA_QK_W + 2 * GLA_V_W) // LANE)),
            pl.BlockSpec((1, LANE, GLA_QK_W), lambda i, j: (j // nt, 0, 0)),
            pl.BlockSpec((1, 1, GLA_QK_W), lambda i, j: (j // nt, 0, 0)),
            pl.BlockSpec((1, GLA_V_W), lambda i, j: (0, 0)),
            pl.BlockSpec((GLA_V_W, GLA_V_W), lambda i, j: (0, 0)),
        ],
        out_specs=pl.BlockSpec((1, tc, GLA_V_W), lambda i, j: (i, jnp.where(j < nt, nt - 1, 2 * nt - 1 - j), 0)),
        out_shape=jax.ShapeDtypeStruct((b, t, GLA_V_W), F32),
        scratch_shapes=[pltpu.VMEM((t, GLA_V_W), F32),
                        pltpu.VMEM((GLA_QK_W, GLA_V_W), F32)],
        compiler_params=_params(dimension_semantics=("arbitrary", "arbitrary")),
        name="gla",
    )(gla_g, gla_g, gla_g, gla_g, gla_g, w2p, gb, g, avg)


def _rope(y, cos, sin, half):
    lane = lax.broadcasted_iota(jnp.int32, (1, LANE), 1)
    lo = (lane % (2 * half)) < half
    rot = jnp.where(lo, pltpu.roll(y, LANE - half, 1), pltpu.roll(y, half, 1))
    return y * cos + rot * sin


def _gqa_prep_kernel(q_ref, k_ref, v_ref, cos_ref, sin_ref, gq_ref, gk_ref, qo_ref, ko_ref, vt_ref):
    cos = cos_ref[...]
    sin = sin_ref[...]

    def norm_rope(x, g, scale):
        ms = jnp.sum(x * x, axis=-1, keepdims=True) * (1.0 / GQA_HEAD_DIM)
        return _rope(x * lax.rsqrt(ms + EPS) * g, cos, sin, GQA_HEAD_DIM // 4) * scale

    for h in range(GQA_HEADS):
        sl = slice(h * LANE, (h + 1) * LANE)
        qo_ref[0, :, sl] = norm_rope(q_ref[0, :, sl], gq_ref[...], GQA_HEAD_DIM ** -0.5).astype(BF16)
    for h in range(GQA_KV_HEADS):
        sl = slice(h * LANE, (h + 1) * LANE)
        ko_ref[0, :, sl] = norm_rope(k_ref[0, :, sl], gk_ref[...], 1.0).astype(BF16)
    vt = v_ref[0].T
    vt_ref[0, :, 0] = vt.reshape(GQA_KV_HEADS, GQA_HEAD_DIM, vt.shape[-1]).astype(BF16)


def _gqa_prep(gqa_g, cos, sin, gq, gk, tm):
    b, t, _ = gqa_g.shape
    qw, kw, vw = GQA_HEADS * LANE, GQA_KV_HEADS * LANE, GQA_KV_HEADS * GQA_HEAD_DIM
    return pl.pallas_call(
        _gqa_prep_kernel,
        grid=(b, t // tm),
        in_specs=[
            pl.BlockSpec((1, tm, qw), lambda i, j: (i, j, 0)),
            pl.BlockSpec((1, tm, kw), lambda i, j: (i, j, qw // kw)),
            pl.BlockSpec((1, tm, vw), lambda i, j: (i, j, (qw + kw) // vw)),
            pl.BlockSpec((tm, LANE), lambda i, j: (j, 0)),
            pl.BlockSpec((tm, LANE), lambda i, j: (j, 0)),
            pl.BlockSpec((1, LANE), lambda i, j: (0, 0)),
            pl.BlockSpec((1, LANE), lambda i, j: (0, 0)),
        ],
        out_specs=[
            pl.BlockSpec((1, tm, qw), lambda i, j: (i, j, 0)),
            pl.BlockSpec((1, tm, kw), lambda i, j: (i, j, 0)),
            pl.BlockSpec((1, GQA_KV_HEADS, 1, GQA_HEAD_DIM, tm), lambda i, j: (i, 0, j, 0, 0)),
        ],
        out_shape=[
            jax.ShapeDtypeStruct((b, t, qw), BF16),
            jax.ShapeDtypeStruct((b, t, kw), BF16),
            jax.ShapeDtypeStruct((b, GQA_KV_HEADS, t // tm, GQA_HEAD_DIM, tm), BF16),
        ],
        compiler_params=_params(dimension_semantics=("arbitrary", "arbitrary")),
        name="gqa_prep",
    )(gqa_g, gqa_g, gqa_g, cos, sin, gq, gk)


def _mla_prep_kernel(cq_ref, ckv_ref, kr_ref, cos_ref, sin_ref, gq_ref, gkv_ref, wuq_ref, wuk_ref, wuv_ref,
                     qo_ref, ko_ref, vt_ref):
    cos = cos_ref[...]
    sin = sin_ref[...]

    def rms(x, g):
        ms = jnp.mean(x * x, axis=-1, keepdims=True)
        return (x * lax.rsqrt(ms + EPS) * g).astype(BF16)

    q = _dot(rms(cq_ref[0], gq_ref[...]), wuq_ref[...])
    ckv = rms(ckv_ref[0], gkv_ref[...])
    kn = _dot(ckv, wuk_ref[...])
    v = _dot(ckv, wuv_ref[...])
    k_rope = _rope(kr_ref[0], cos, sin, MLA_ROPE // 4)
    for h in range(MLA_HEADS):
        sl = slice(h * LANE, (h + 1) * LANE)
        qo_ref[0, :, sl] = (_rope(q[:, sl], cos, sin, MLA_ROPE // 4) * (MLA_QK ** -0.5)).astype(BF16)
        ko_ref[0, :, sl] = (kn[:, sl] + k_rope).astype(BF16)
    vt = v.T
    vt_ref[0, :, 0] = vt.reshape(MLA_HEADS, MLA_V, vt.shape[-1]).astype(BF16)


def _mla_prep(mla_g, cos, sin, gq, gkv, wuq, wuk, wuv, tm):
    b, t, _ = mla_g.shape
    hw = MLA_HEADS * LANE
    return pl.pallas_call(
        _mla_prep_kernel,
        grid=(b, t // tm),
        in_specs=[
            pl.BlockSpec((1, tm, MLA_Q_LORA), lambda i, j: (i, j, 0)),
            pl.BlockSpec((1, tm, MLA_KV_LORA), lambda i, j: (i, j, MLA_Q_LORA // MLA_KV_LORA)),
            pl.BlockSpec((1, tm, LANE), lambda i, j: (i, j, (MLA_Q_LORA + MLA_KV_LORA) // LANE)),
            pl.BlockSpec((tm, LANE), lambda i, j: (j, 0)),
            pl.BlockSpec((tm, LANE), lambda i, j: (j, 0)),
            pl.BlockSpec((1, MLA_Q_LORA), lambda i, j: (0, 0)),
            pl.BlockSpec((1, MLA_KV_LORA), lambda i, j: (0, 0)),
            pl.BlockSpec((MLA_Q_LORA, hw), lambda i, j: (0, 0)),
            pl.BlockSpec((MLA_KV_LORA, hw), lambda i, j: (0, 0)),
            pl.BlockSpec((MLA_KV_LORA, MLA_HEADS * MLA_V), lambda i, j: (0, 0)),
        ],
        out_specs=[
            pl.BlockSpec((1, tm, hw), lambda i, j: (i, j, 0)),
            pl.BlockSpec((1, tm, hw), lambda i, j: (i, j, 0)),
            pl.BlockSpec((1, MLA_HEADS, 1, MLA_V, tm), lambda i, j: (i, 0, j, 0, 0)),
        ],
        out_shape=[
            jax.ShapeDtypeStruct((b, t, hw), BF16),
            jax.ShapeDtypeStruct((b, t, hw), BF16),
            jax.ShapeDtypeStruct((b, MLA_HEADS, t // tm, MLA_V, tm), BF16),
        ],
        compiler_params=_params(dimension_semantics=("arbitrary", "arbitrary")),
        name="mla_prep",
    )(mla_g, mla_g, mla_g, cos, sin, gq, gkv, wuq, wuk, wuv)


def _attn_kernel(q_ref, k_ref, vt_ref, o_ref, acc_ref, *, n_kv, rep, tq, tk, dv):
    nk = k_ref.shape[1] // tk
    width = rep * tq
    for g in range(n_kv):
        qg = jnp.concatenate([q_ref[0, :, (g * rep + r) * LANE:(g * rep + r + 1) * LANE] for r in range(rep)], axis=0)

        def body(i, carry, g=g, qg=qg):
            m, l, acc = carry
            ks = pl.multiple_of(i * tk, tk)
            s = _dot_nt(k_ref[0, pl.ds(ks, tk), g * LANE:(g + 1) * LANE], qg)
            m_new = jnp.maximum(m, jnp.max(s, axis=0, keepdims=True))
            alpha = jnp.exp(m - m_new)
            p = jnp.exp(s - m_new)
            l = l * alpha + jnp.sum(p, axis=0, keepdims=True)
            acc = acc * alpha + _dot(vt_ref[0, g, i], p.astype(BF16))
            return m_new, l, acc

        init = (jnp.full((1, width), -jnp.inf, F32), jnp.zeros((1, width), F32), jnp.zeros((dv, width), F32))
        _, l, acc = lax.fori_loop(0, nk, body, init)
        o = acc * (1.0 / l)
        for r in range(rep):
            hd = g * rep + r
            acc_ref[hd * dv:(hd + 1) * dv, :] = o[:, r * tq:(r + 1) * tq]
    o_ref[0] = acc_ref[...].T


def _attention(q, k, vt, n_kv, rep, tq, tk, dv):
    b, t, qw = q.shape
    kw = k.shape[2]
    ow = n_kv * rep * dv
    kern = functools.partial(_attn_kernel, n_kv=n_kv, rep=rep, tq=tq, tk=tk, dv=dv)
    return pl.pallas_call(
        kern,
        grid=(b, t // tq),
        in_specs=[
            pl.BlockSpec((1, tq, qw), lambda i, j: (i, j, 0)),
            pl.BlockSpec((1, t, kw), lambda i, j: (i, 0, 0)),
            pl.BlockSpec((1, n_kv, t // tk, dv, tk), lambda i, j: (i, 0, 0, 0, 0)),
        ],
        out_specs=pl.BlockSpec((1, tq, ow), lambda i, j: (i, j, 0)),
        out_shape=jax.ShapeDtypeStruct((b, t, ow), F32),
        scratch_shapes=[pltpu.VMEM((ow, tq), F32)],
        compiler_params=_params(dimension_semantics=("arbitrary", "arbitrary")),
        name="attention",
    )(q, k, vt)


FF_CHUNK = 1024


def _outmlp_kernel(x_ref, m0_ref, m1_ref, m2_ref, m3_ref, wo_ref, g2_ref, w1_ref, w2_ref, gf_ref, o_ref, *, final):
    x1 = x_ref[...]
    for i, m_ref in enumerate((m0_ref, m1_ref, m2_ref, m3_ref)):
        x1 = x1 + _dot(m_ref[...].astype(BF16), wo_ref[i * GROUP_W:(i + 1) * GROUP_W, :])
    ms = jnp.mean(x1 * x1, axis=-1, keepdims=True)
    h = (x1 * lax.rsqrt(ms + EPS) * g2_ref[...]).astype(BF16)
    acc = None
    for c in range(D_FF // FF_CHUNK):
        sl = slice(c * FF_CHUNK, (c + 1) * FF_CHUNK)
        u = jnp.maximum(_dot(h, w1_ref[:, sl]), 0.0)
        part = _dot((u * u).astype(BF16), w2_ref[sl, :])
        acc = part if acc is None else acc + part
    y = x1 + acc
    if final:
        ms = jnp.mean(y * y, axis=-1, keepdims=True)
        y = y * lax.rsqrt(ms + EPS) * gf_ref[...]
    o_ref[...] = y


def _outmlp(x2d, mixes, wo, g2, w1, w2, gf, final, tm):
    n = x2d.shape[0]
    const = lambda i: (0, 0)
    single = pl.Buffered(1)
    kern = functools.partial(_outmlp_kernel, final=final)
    return pl.pallas_call(
        kern,
        grid=(n // tm,),
        in_specs=[pl.BlockSpec((tm, D_MODEL), lambda i: (i, 0))]
                 + [pl.BlockSpec((tm, GROUP_W), lambda i: (i, 0)) for _ in range(4)]
                 + [pl.BlockSpec((D_MODEL, D_MODEL), const, pipeline_mode=single),
                    pl.BlockSpec((1, D_MODEL), const),
                    pl.BlockSpec((D_MODEL, D_FF), const, pipeline_mode=single),
                    pl.BlockSpec((D_FF, D_MODEL), const, pipeline_mode=single),
                    pl.BlockSpec((1, D_MODEL), const)],
        out_specs=pl.BlockSpec((tm, D_MODEL), lambda i: (i, 0)),
        out_shape=jax.ShapeDtypeStruct((n, D_MODEL), F32),
        compiler_params=_params(dimension_semantics=("arbitrary",)),
        name="outmlp",
    )(x2d, *mixes, wo, g2.reshape(1, D_MODEL), w1, w2, gf.reshape(1, D_MODEL))


def _rope_tables(t):
    pos = jnp.arange(t, dtype=jnp.int32)
    row = (pos // GRID_W).astype(F32)
    col = (pos % GRID_W).astype(F32)

    def block(p, n):
        inv_freq = ROPE_THETA ** (-jnp.arange(n, dtype=F32) / n)
        ang = p[:, None] * inv_freq[None, :]
        c, s = jnp.cos(ang), jnp.sin(ang)
        return jnp.concatenate([c, c], axis=1), jnp.concatenate([-s, s], axis=1)

    def table(n, lead):
        cr, sr = block(row, n)
        cc, sc = block(col, n)
        tail = LANE - lead - 4 * n
        cos = jnp.concatenate([jnp.ones((t, lead), F32), cr, cc, jnp.ones((t, tail), F32)], axis=1)
        sin = jnp.concatenate([jnp.zeros((t, lead), F32), sr, sc, jnp.zeros((t, tail), F32)], axis=1)
        return cos, sin

    return table(GQA_HEAD_DIM // 4, 0), table(MLA_ROPE // 4, MLA_NOPE)


def _pad_lanes(v, width=LANE):
    return jnp.concatenate([v, jnp.zeros((width - v.shape[0],), v.dtype)]).reshape(1, width)


def _layer_params(i, p):
    (norm1_g, w_in, ssd_conv_w, ssd_conv_b, ssd_dt_bias, ssd_a_log, ssd_d, ssd_norm_g,
     gqa_q_norm_g, gqa_k_norm_g, gla_gate_w2, gla_gate_b, gla_norm_g, mla_q_norm_g, mla_w_uq,
     mla_kv_norm_g, mla_w_ukv, w_out, norm2_g, w_ff1, w_ff2) = [a[i] for a in p]
    out = {}
    out["norm1_g"] = norm1_g
    out["w_all"] = _gather_cols(w_in, _IN_COLS).astype(BF16)
    out["conv_w"] = jnp.concatenate([ssd_conv_w, jnp.zeros((HALO - SSD_CONV_W, SSD_XBC), F32)], axis=0)
    out["conv_b"] = ssd_conv_b.reshape(1, SSD_XBC)
    expand = lambda a: jnp.repeat(a, SSD_HEAD_DIM, axis=-1)
    out["dt_bias"] = expand(ssd_dt_bias).reshape(2, 1, GROUP_W)
    out["a_neg"] = expand(-jnp.exp(ssd_a_log)).reshape(2, 1, GROUP_W)
    out["ssd_d"] = expand(ssd_d).reshape(1, GROUP_W)
    out["ssd_norm_g"] = ssd_norm_g.reshape(1, GROUP_W)
    out["gq"] = _pad_lanes(gqa_q_norm_g)
    out["gk"] = _pad_lanes(gqa_k_norm_g)
    w2p = jnp.zeros((2, LANE, GLA_QK_W), F32)
    for d in range(2):
        w2p = w2p.at[d, d * GLA_LOWRANK:(d + 1) * GLA_LOWRANK, :].set(gla_gate_w2[d])
    out["gla_w2"] = w2p.astype(BF16)
    out["gla_b"] = gla_gate_b.reshape(2, 1, GLA_QK_W)
    out["gla_norm_g"] = jnp.tile(gla_norm_g, GLA_HEADS).reshape(1, GLA_V_W)
    out["mla_gq"] = mla_q_norm_g.reshape(1, MLA_Q_LORA)
    out["mla_gkv"] = mla_kv_norm_g.reshape(1, MLA_KV_LORA)
    uq_cols, uk_cols, uv_cols = [], [], []
    for h in range(MLA_HEADS):
        uq_cols += list(range(h * MLA_QK, (h + 1) * MLA_QK)) + [-1] * (LANE - MLA_QK)
        base = h * (MLA_NOPE + MLA_V)
        uk_cols += list(range(base, base + MLA_NOPE)) + [-1] * (LANE - MLA_NOPE)
        uv_cols += list(range(base + MLA_NOPE, base + MLA_NOPE + MLA_V))
    out["wuq"] = _gather_cols(mla_w_uq, np.asarray(uq_cols, np.int32)).astype(BF16)
    out["wuk"] = _gather_cols(mla_w_ukv, np.asarray(uk_cols, np.int32)).astype(BF16)
    out["wuv"] = _gather_cols(mla_w_ukv, np.asarray(uv_cols, np.int32)).astype(BF16)
    out["w_out"] = w_out.astype(BF16)
    out["norm2_g"] = norm2_g
    out["w_ff1"] = w_ff1.astype(BF16)
    out["w_ff2"] = w_ff2.astype(BF16)
    return out


TILE_PREF = dict(tm=256, tc=256, tp=512, tq_gqa=256, tq_mla=512, tmlp=512)


def _tiles(t):
    pick = lambda pref: max(c for c in (128, 256, 512, 1024) if c <= pref and t % c == 0)
    return {name: pick(pref) for name, pref in TILE_PREF.items()}


def _trunk(x, layers, final_norm_g):
    b, t, d = x.shape
    n = b * t
    ts = _tiles(t)
    tk = ts["tp"]
    (cos_g, sin_g), (cos_m, sin_m) = _rope_tables(t)
    avg = jnp.asarray(np.kron(np.eye(GLA_HEADS), np.full((GLA_DV, GLA_DV), 1.0 / GLA_DV)), BF16)
    x2d = x.reshape(n, d)
    for i, lp in enumerate(layers):
        ssd_g, gqa_g, gla_g, mla_g = _inproj(x2d, lp["norm1_g"], lp["w_all"], ts["tm"])
        ssd_out = _ssd(ssd_g.reshape(b, t, SSD_W), lp["conv_w"], lp["conv_b"], lp["dt_bias"], lp["a_neg"],
                       lp["ssd_d"], lp["ssd_norm_g"], ts["tc"])
        gq, gk, gvt = _gqa_prep(gqa_g.reshape(b, t, GQA_W), cos_g, sin_g, lp["gq"], lp["gk"], tk)
        gqa_out = _attention(gq, gk, gvt, GQA_KV_HEADS, GQA_HEADS // GQA_KV_HEADS, ts["tq_gqa"], tk, GQA_HEAD_DIM)
        gla_out = _gla(gla_g.reshape(b, t, GLA_W), lp["gla_w2"], lp["gla_b"], lp["gla_norm_g"], avg, ts["tc"])
        mq, mk, mvt = _mla_prep(mla_g.reshape(b, t, MLA_W), cos_m, sin_m, lp["mla_gq"], lp["mla_gkv"],
                                lp["wuq"], lp["wuk"], lp["wuv"], tk)
        mla_out = _attention(mq, mk, mvt, MLA_HEADS, 1, ts["tq_mla"], tk, MLA_V)
        mixes = [m.reshape(n, GROUP_W) for m in (ssd_out, gqa_out, gla_out, mla_out)]
        x2d = _outmlp(x2d, mixes, lp["w_out"], lp["norm2_g"], lp["w_ff1"], lp["w_ff2"], final_norm_g,
                      i == len(layers) - 1, ts["tmlp"])
    return x2d.reshape(b, t, d)


def kernel(x_prompt, x_sample, norm1_g, w_in, ssd_conv_w, ssd_conv_b, ssd_dt_bias, ssd_a_log, ssd_d, ssd_norm_g,
           gqa_q_norm_g, gqa_k_norm_g, gla_gate_w2, gla_gate_b, gla_norm_g, mla_q_norm_g, mla_w_uq,
           mla_kv_norm_g, mla_w_ukv, w_out, norm2_g, w_ff1, w_ff2, final_norm_g):
    stacked = (norm1_g, w_in, ssd_conv_w, ssd_conv_b, ssd_dt_bias, ssd_a_log, ssd_d, ssd_norm_g,
               gqa_q_norm_g, gqa_k_norm_g, gla_gate_w2, gla_gate_b, gla_norm_g, mla_q_norm_g, mla_w_uq,
               mla_kv_norm_g, mla_w_ukv, w_out, norm2_g, w_ff1, w_ff2)
    layers = [_layer_params(i, stacked) for i in range(norm1_g.shape[0])]
    return (_trunk(x_prompt, layers, final_norm_g), _trunk(x_sample, layers, final_norm_g))
```

```python
import functools

import numpy as np
import jax
import jax.numpy as jnp
from jax import lax
from jax.experimental import pallas as pl
from jax.experimental.pallas import tpu as pltpu

F32 = jnp.float32
BF16 = jnp.bfloat16

D_MODEL = 1024
DEPTH = 2
GRID_W = 64
CHUNK = 128
ROPE_THETA = 10000.0
EPS = 1e-6
GROUP_W = D_MODEL // 4
D_FF = 4 * D_MODEL

SSD_HEADS = 4
SSD_HEAD_DIM = GROUP_W // SSD_HEADS
SSD_GROUPS = 2
SSD_STATE = 128
SSD_CONV_W = 5
SSD_XBC = GROUP_W + 2 * SSD_GROUPS * SSD_STATE

GQA_HEADS = 4
GQA_KV_HEADS = 2
GQA_HEAD_DIM = GROUP_W // GQA_HEADS

GLA_HEADS = 4
GLA_DV = GROUP_W // GLA_HEADS
GLA_DK = GLA_DV // 2
GLA_LOWRANK = 16
GLA_TAU = 16.0

MLA_HEADS = 4
MLA_Q_LORA = 256
MLA_KV_LORA = 128
MLA_NOPE = 64
MLA_ROPE = 32
MLA_V = GROUP_W // MLA_HEADS
MLA_QK = MLA_NOPE + MLA_ROPE

IN_SIZES = (GROUP_W, SSD_XBC, 2 * SSD_HEADS,
            GQA_HEADS * GQA_HEAD_DIM, GQA_KV_HEADS * GQA_HEAD_DIM, GQA_KV_HEADS * GQA_HEAD_DIM,
            GLA_HEADS * GLA_DK, GLA_HEADS * GLA_DK, GLA_HEADS * GLA_DV, GLA_HEADS * GLA_DV, 2 * GLA_LOWRANK,
            MLA_Q_LORA, MLA_KV_LORA, MLA_ROPE)
IN_OFFS = tuple(int(v) for v in np.concatenate([[0], np.cumsum(IN_SIZES)]))

LOG2E = 1.4426950408889634
LANE = 128
HALO = 8
VMEM_LIMIT = 52 * 1024 * 1024

SSD_W = SSD_XBC + GROUP_W + 2 * GROUP_W
GQA_W = GQA_HEADS * LANE + GQA_KV_HEADS * LANE + GQA_KV_HEADS * GQA_HEAD_DIM
GLA_W = 2 * GLA_HEADS * GLA_DK + 2 * GLA_HEADS * GLA_DV + LANE
MLA_W = MLA_Q_LORA + MLA_KV_LORA + LANE
PROJ_W = SSD_W + GQA_W + GLA_W + MLA_W


def _in_proj_columns():
    o = IN_OFFS
    cols = []
    cols += list(range(o[1], o[2]))
    cols += list(range(o[0], o[1]))
    for d in range(2):
        for h in range(SSD_HEADS):
            cols += [o[2] + d * SSD_HEADS + h] * SSD_HEAD_DIM
    for h in range(GQA_HEADS):
        cols += list(range(o[3] + h * GQA_HEAD_DIM, o[3] + (h + 1) * GQA_HEAD_DIM)) + [-1] * (LANE - GQA_HEAD_DIM)
    for h in range(GQA_KV_HEADS):
        cols += list(range(o[4] + h * GQA_HEAD_DIM, o[4] + (h + 1) * GQA_HEAD_DIM)) + [-1] * (LANE - GQA_HEAD_DIM)
    cols += list(range(o[5], o[6]))
    cols += list(range(o[6], o[10]))
    cols += list(range(o[10], o[11])) + [-1] * (LANE - 2 * GLA_LOWRANK)
    cols += list(range(o[11], o[13]))
    cols += [-1] * MLA_NOPE + list(range(o[13], o[14])) + [-1] * (LANE - MLA_QK)
    cols = np.asarray(cols, np.int32)
    assert cols.shape[0] == PROJ_W
    return cols


_IN_COLS = _in_proj_columns()


def _gather_cols(w, cols):
    picked = jnp.take(w, jnp.asarray(np.maximum(cols, 0)), axis=1)
    return jnp.where(jnp.asarray(cols >= 0)[None, :], picked, 0.0)


def _params(**kw):
    return pltpu.CompilerParams(vmem_limit_bytes=VMEM_LIMIT, **kw)


def _silu(x):
    return x * (1.0 / (1.0 + jnp.exp(-x)))


def _softplus(x):
    return jnp.maximum(x, 0.0) + jnp.log1p(jnp.exp(-jnp.abs(x)))


def _dot(a, b):
    return jnp.dot(a, b, preferred_element_type=F32)


def _dot_nt(a, b):
    return lax.dot_general(a, b, (((1,), (1,)), ((), ())), preferred_element_type=F32)


def _dot_tn(a, b):
    return lax.dot_general(a, b, (((0,), (0,)), ((), ())), preferred_element_type=F32)


def _split_dot(m_bf16, x):
    x1 = x.astype(BF16)
    r1 = x - x1.astype(F32)
    x2 = r1.astype(BF16)
    x3 = (r1 - x2.astype(F32)).astype(BF16)
    return _dot(m_bf16, x1) + _dot(m_bf16, x2) + _dot(m_bf16, x3)


def _tri(reverse):
    li = lax.broadcasted_iota(jnp.int32, (CHUNK, CHUNK), 0)
    si = lax.broadcasted_iota(jnp.int32, (CHUNK, CHUNK), 1)
    return (si >= li) if reverse else (si <= li)


def _inproj_kernel(x_ref, g_ref, w_ref, ssd_ref, gqa_ref, gla_ref, mla_ref):
    x = x_ref[...]
    ms = jnp.mean(x * x, axis=-1, keepdims=True)
    h = (x * lax.rsqrt(ms + EPS) * g_ref[...]).astype(BF16)
    c0 = 0
    for ref, w in ((ssd_ref, SSD_W), (gqa_ref, GQA_W), (gla_ref, GLA_W), (mla_ref, MLA_W)):
        ref[...] = _dot(h, w_ref[:, c0:c0 + w])
        c0 += w


def _inproj(x2d, g, w_all, tm):
    n = x2d.shape[0]
    out_w = (SSD_W, GQA_W, GLA_W, MLA_W)
    return pl.pallas_call(
        _inproj_kernel,
        grid=(n // tm,),
        in_specs=[pl.BlockSpec((tm, D_MODEL), lambda i: (i, 0)),
                  pl.BlockSpec((1, D_MODEL), lambda i: (0, 0)),
                  pl.BlockSpec((D_MODEL, PROJ_W), lambda i: (0, 0), pipeline_mode=pl.Buffered(1))],
        out_specs=[pl.BlockSpec((tm, w), lambda i: (i, 0)) for w in out_w],
        out_shape=[jax.ShapeDtypeStruct((n, w), F32) for w in out_w],
        compiler_params=_params(dimension_semantics=("arbitrary",)),
        name="inproj",
    )(x2d, g.reshape(1, D_MODEL), w_all)


def _ssd_kernel(xbc_ref, prev_ref, next_ref, z_ref, dtx_ref, cw_ref, cb_ref, dtb_ref, a_ref, d_ref, g_ref,
                out_ref, yf_ref, h_ref, xe_ref, *, nt, tc):
    j = pl.program_id(1)
    tile = jnp.where(j < nt, j, 2 * nt - 1 - j)

    @pl.when((j == 0) | (j == nt))
    def _():
        h_ref[...] = jnp.zeros_like(h_ref)

    xe_ref[HALO:HALO + tc, :] = xbc_ref[0]
    xe_ref[0:HALO, :] = jnp.where(tile > 0, prev_ref[0], 0.0)
    xe_ref[HALO + tc:2 * HALO + tc, :] = jnp.where(tile < nt - 1, next_ref[0], 0.0)
    pad = SSD_CONV_W // 2
    acc = cb_ref[...] + cw_ref[0:1, :] * xe_ref[HALO - pad:HALO - pad + tc, :]
    for w in range(1, SSD_CONV_W):
        acc = acc + cw_ref[w:w + 1, :] * xe_ref[HALO - pad + w:HALO - pad + w + tc, :]
    xbc = _silu(acc)
    xs = xbc[:, :GROUP_W]
    dt = _softplus(dtx_ref[0] + dtb_ref[0])
    a_dt = dt * a_ref[0]
    xdt = xs * dt
    lane_head = lax.broadcasted_iota(jnp.int32, (1, GROUP_W), 1) // SSD_HEAD_DIM

    def chunk(c, reverse):
        sl = slice(c * CHUNK, (c + 1) * CHUNK)
        tri = _tri(reverse)
        acs = _split_dot(tri.astype(BF16), a_dt[sl])
        tot = acs[0:1] if reverse else acs[CHUNK - 1:CHUNK]
        acs_t = acs.T
        x_c = xdt[sl]
        b_c = [xbc[sl, GROUP_W + g * SSD_STATE:GROUP_W + (g + 1) * SSD_STATE].astype(BF16)
               for g in range(SSD_GROUPS)]
        c_c = [xbc[sl, GROUP_W + (SSD_GROUPS + g) * SSD_STATE:GROUP_W + (SSD_GROUPS + g + 1) * SSD_STATE].astype(BF16)
               for g in range(SSD_GROUPS)]
        y = jnp.zeros((CHUNK, GROUP_W), F32)
        for g in range(SSD_GROUPS):
            scores = _dot_nt(c_c[g], b_c[g])
            for r in range(SSD_HEADS // SSD_GROUPS):
                hd = g * (SSD_HEADS // SSD_GROUPS) + r
                col = jnp.broadcast_to(acs[:, hd * SSD_HEAD_DIM:hd * SSD_HEAD_DIM + 1], (CHUNK, CHUNK))
                row = jnp.broadcast_to(acs_t[hd * SSD_HEAD_DIM:hd * SSD_HEAD_DIM + 1, :], (CHUNK, CHUNK))
                decay = jnp.exp(jnp.where(tri, col - row, -jnp.inf))
                x_h = jnp.where(lane_head == hd, x_c, 0.0).astype(BF16)
                y = y + _dot((scores * decay).astype(BF16), x_h)
        h_in = h_ref[...]
        half = GROUP_W // SSD_GROUPS
        y_off = jnp.concatenate([_dot(c_c[g], h_in[:, g * half:(g + 1) * half].astype(BF16))
                                 for g in range(SSD_GROUPS)], axis=1)
        y = y + y_off * jnp.exp(acs)
        x_d = (x_c * jnp.exp(tot - acs)).astype(BF16)
        st = jnp.concatenate([_dot_tn(b_c[g], x_d[:, g * half:(g + 1) * half]) for g in range(SSD_GROUPS)], axis=1)
        h_ref[...] = h_in * jnp.exp(tot) + st
        return y

    nc = tc // CHUNK

    @pl.when(j < nt)
    def _():
        for c in range(nc):
            start = pl.multiple_of(tile * tc + c * CHUNK, CHUNK)
            yf_ref[pl.ds(start, CHUNK), :] = chunk(c, False)

    @pl.when(j >= nt)
    def _():
        for c in reversed(range(nc)):
            sl = slice(c * CHUNK, (c + 1) * CHUNK)
            start = pl.multiple_of(tile * tc + c * CHUNK, CHUNK)
            y = yf_ref[pl.ds(start, CHUNK), :] + chunk(c, True) + d_ref[...] * xs[sl]
            gated = y * _silu(z_ref[0, sl, :])
            ms = jnp.mean(gated * gated, axis=-1, keepdims=True)
            out_ref[0, sl, :] = gated * lax.rsqrt(ms + EPS) * g_ref[...]


def _ssd(ssd_g, cw, cb, dtb, a_exp, d_exp, g, tc):
    b, t, _ = ssd_g.shape
    nt = t // tc
    hb = tc // HALO

    def tile_of(j):
        return jnp.where(j < nt, j, 2 * nt - 1 - j)

    kern = functools.partial(_ssd_kernel, nt=nt, tc=tc)
    return pl.pallas_call(
        kern,
        grid=(b, 2 * nt),
        in_specs=[
            pl.BlockSpec((1, tc, SSD_XBC), lambda i, j: (i, tile_of(j), 0)),
            pl.BlockSpec((1, HALO, SSD_XBC), lambda i, j: (i, jnp.maximum(tile_of(j) * hb - 1, 0), 0)),
            pl.BlockSpec((1, HALO, SSD_XBC), lambda i, j: (i, jnp.minimum((tile_of(j) + 1) * hb, t // HALO - 1), 0)),
            pl.BlockSpec((1, tc, GROUP_W), lambda i, j: (i, tile_of(j), SSD_XBC // GROUP_W)),
            pl.BlockSpec((1, tc, GROUP_W), lambda i, j: (i, tile_of(j), SSD_XBC // GROUP_W + 1 + j // nt)),
            pl.BlockSpec((HALO, SSD_XBC), lambda i, j: (0, 0)),
            pl.BlockSpec((1, SSD_XBC), lambda i, j: (0, 0)),
            pl.BlockSpec((1, 1, GROUP_W), lambda i, j: (j // nt, 0, 0)),
            pl.BlockSpec((1, 1, GROUP_W), lambda i, j: (j // nt, 0, 0)),
            pl.BlockSpec((1, GROUP_W), lambda i, j: (0, 0)),
            pl.BlockSpec((1, GROUP_W), lambda i, j: (0, 0)),
        ],
        out_specs=pl.BlockSpec((1, tc, GROUP_W), lambda i, j: (i, jnp.where(j < nt, nt - 1, 2 * nt - 1 - j), 0)),
        out_shape=jax.ShapeDtypeStruct((b, t, GROUP_W), F32),
        scratch_shapes=[pltpu.VMEM((t, GROUP_W), F32),
                        pltpu.VMEM((SSD_STATE, GROUP_W), F32),
                        pltpu.VMEM((tc + 2 * HALO, SSD_XBC), F32)],
        compiler_params=_params(dimension_semantics=("arbitrary", "arbitrary")),
        name="ssd",
    )(ssd_g, ssd_g, ssd_g, ssd_g, ssd_g, cw, cb, dtb, a_exp, d_exp, g)


GLA_QK_W = GLA_HEADS * GLA_DK
GLA_V_W = GLA_HEADS * GLA_DV


def _gla_kernel(q_ref, k_ref, v_ref, r_ref, lr_ref, w2_ref, gb_ref, g_ref, avg_ref,
                out_ref, of_ref, s_ref, *, nt, tc):
    j = pl.program_id(1)
    tile = jnp.where(j < nt, j, 2 * nt - 1 - j)

    @pl.when((j == 0) | (j == nt))
    def _():
        s_ref[...] = jnp.zeros_like(s_ref)

    logit = _dot(lr_ref[0].astype(BF16), w2_ref[0]) + gb_ref[0]
    logg = -_softplus(-logit) * (1.0 / GLA_TAU)
    q = q_ref[0] * (GLA_DK ** -0.5)
    k = k_ref[0]
    v = v_ref[0]
    qk_head = lax.broadcasted_iota(jnp.int32, (1, GLA_QK_W), 1) // GLA_DK
    v_head = lax.broadcasted_iota(jnp.int32, (1, GLA_V_W), 1) // GLA_DV
    blockdiag = (lax.broadcasted_iota(jnp.int32, (GLA_QK_W, GLA_V_W), 0) // GLA_DK
                 == lax.broadcasted_iota(jnp.int32, (GLA_QK_W, GLA_V_W), 1) // GLA_DV)

    def chunk(c, reverse):
        sl = slice(c * CHUNK, (c + 1) * CHUNK)
        tri = _tri(reverse)
        gcs = _split_dot(tri.astype(BF16), logg[sl])
        last = 0 if reverse else CHUNK - 1
        tot = gcs[last:last + 1]
        tot_col = gcs.T[:, last:last + 1]
        qg = q[sl] * jnp.exp(gcs)
        kg = (k[sl] * jnp.exp(-gcs)).astype(BF16)
        v_c = v[sl]
        o = jnp.zeros((CHUNK, GLA_V_W), F32)
        for hd in range(GLA_HEADS):
            q_h = jnp.where(qk_head == hd, qg, 0.0).astype(BF16)
            att = jnp.where(tri, _dot_nt(q_h, kg), 0.0)
            v_h = jnp.where(v_head == hd, v_c, 0.0).astype(BF16)
            o = o + _dot(att.astype(BF16), v_h)
        s_in = s_ref[...]
        o = o + _dot(qg.astype(BF16), s_in.astype(BF16))
        kd = (k[sl] * jnp.exp(tot - gcs)).astype(BF16)
        st = jnp.where(blockdiag, _dot_tn(kd, v_c.astype(BF16)), 0.0)
        s_ref[...] = s_in * jnp.exp(tot_col) + st
        return o

    nc = tc // CHUNK

    @pl.when(j < nt)
    def _():
        for c in range(nc):
            start = pl.multiple_of(tile * tc + c * CHUNK, CHUNK)
            of_ref[pl.ds(start, CHUNK), :] = chunk(c, False)

    @pl.when(j >= nt)
    def _():
        for c in reversed(range(nc)):
            sl = slice(c * CHUNK, (c + 1) * CHUNK)
            start = pl.multiple_of(tile * tc + c * CHUNK, CHUNK)
            o = of_ref[pl.ds(start, CHUNK), :] + chunk(c, True)
            ms = _split_dot_right(o * o, avg_ref[...])
            out_ref[0, sl, :] = o * lax.rsqrt(ms + EPS) * g_ref[...] * _silu(r_ref[0, sl, :])


def _split_dot_right(x, m_bf16):
    x1 = x.astype(BF16)
    r1 = x - x1.astype(F32)
    x2 = r1.astype(BF16)
    x3 = (r1 - x2.astype(F32)).astype(BF16)
    return _dot(x1, m_bf16) + _dot(x2, m_bf16) + _dot(x3, m_bf16)


def _gla(gla_g, w2p, gb, g, avg, tc):
    b, t, _ = gla_g.shape
    nt = t // tc

    def tile_of(j):
        return jnp.where(j < nt, j, 2 * nt - 1 - j)

    kern = functools.partial(_gla_kernel, nt=nt, tc=tc)
    return pl.pallas_call(
        kern,
        grid=(b, 2 * nt),
        in_specs=[
            pl.BlockSpec((1, tc, GLA_QK_W), lambda i, j: (i, tile_of(j), 0)),
            pl.BlockSpec((1, tc, GLA_QK_W), lambda i, j: (i, tile_of(j), 1)),
            pl.BlockSpec((1, tc, GLA_V_W), lambda i, j: (i, tile_of(j), 1)),
            pl.BlockSpec((1, tc, GLA_V_W), lambda i, j: (i, tile_of(j), 2)),
            pl.BlockSpec((1, tc, LANE), lambda i, j: (i, tile_of(j), (2 * GLA_QK_W + 2 * GLA_V_W) // LANE)),
            pl.BlockSpec((1, LANE, GLA_QK_W), lambda i, j: (j // nt, 0, 0)),
            pl.BlockSpec((1, 1, GLA_QK_W), lambda i, j: (j // nt, 0, 0)),
            pl.BlockSpec((1, GLA_V_W), lambda i, j: (0, 0)),
            pl.BlockSpec((GLA_V_W, GLA_V_W), lambda i, j: (0, 0)),
        ],
        out_specs=pl.BlockSpec((1, tc, GLA_V_W), lambda i, j: (i, jnp.where(j < nt, nt - 1, 2 * nt - 1 - j), 0)),
        out_shape=jax.ShapeDtypeStruct((b, t, GLA_V_W), F32),
        scratch_shapes=[pltpu.VMEM((t, GLA_V_W), F32),
                        pltpu.VMEM((GLA_QK_W, GLA_V_W), F32)],
        compiler_params=_params(dimension_semantics=("arbitrary", "arbitrary")),
        name="gla",
    )(gla_g, gla_g, gla_g, gla_g, gla_g, w2p, gb, g, avg)


def _rope(y, cos, sin, half):
    lane = lax.broadcasted_iota(jnp.int32, (1, LANE), 1)
    lo = (lane % (2 * half)) < half
    rot = jnp.where(lo, pltpu.roll(y, LANE - half, 1), pltpu.roll(y, half, 1))
    return y * cos + rot * sin


ONES_LANE = LANE - 1
VT_ROWS = 80
SCORE_BOUND_LIMIT = 48.0


def _finish_keys(k):
    kf = k.astype(BF16).astype(F32)
    norm = jnp.sqrt(jnp.max(jnp.sum(kf * kf, axis=-1, keepdims=True), axis=0, keepdims=True))
    lane = lax.broadcasted_iota(jnp.int32, (1, LANE), 1)
    return jnp.where(lane == ONES_LANE, 1.0, k).astype(BF16), norm


def _update_kmax(kmax_ref, norms):
    @pl.when(pl.program_id(1) == 0)
    def _():
        kmax_ref[...] = jnp.zeros_like(kmax_ref)

    lane = lax.broadcasted_iota(jnp.int32, (1, LANE), 1)
    upd = jnp.zeros((1, LANE), F32)
    for g, norm in enumerate(norms):
        upd = jnp.where(lane == g, norm, upd)
    kmax_ref[0] = jnp.maximum(kmax_ref[0], upd)


def _vt_rows(v, heads, dv):
    tm = v.shape[0]
    vt = v.T.reshape(heads, dv, tm)
    row = lax.broadcasted_iota(jnp.int32, (heads, VT_ROWS - dv, tm), 1)
    return jnp.concatenate([vt, jnp.where(row == 0, 1.0, 0.0)], axis=1).astype(BF16)


def _gqa_prep_kernel(q_ref, k_ref, v_ref, cos_ref, sin_ref, gq_ref, gk_ref, qo_ref, ko_ref, vt_ref, kmax_ref):
    cos = cos_ref[...]
    sin = sin_ref[...]

    def norm_rope(x, g, scale):
        ms = jnp.sum(x * x, axis=-1, keepdims=True) * (1.0 / GQA_HEAD_DIM)
        return _rope(x * lax.rsqrt(ms + EPS) * g, cos, sin, GQA_HEAD_DIM // 4) * scale

    for h in range(GQA_HEADS):
        sl = slice(h * LANE, (h + 1) * LANE)
        qo_ref[0, :, sl] = norm_rope(q_ref[0, :, sl], gq_ref[...], GQA_HEAD_DIM ** -0.5 * LOG2E).astype(BF16)
    norms = []
    for h in range(GQA_KV_HEADS):
        sl = slice(h * LANE, (h + 1) * LANE)
        ko_ref[0, :, sl], norm = _finish_keys(norm_rope(k_ref[0, :, sl], gk_ref[...], 1.0))
        norms.append(norm)
    _update_kmax(kmax_ref, norms)
    vt_ref[0, :, 0] = _vt_rows(v_ref[0], GQA_KV_HEADS, GQA_HEAD_DIM)


def _gqa_prep(gqa_g, cos, sin, gq, gk, tm):
    b, t, _ = gqa_g.shape
    qw, kw, vw = GQA_HEADS * LANE, GQA_KV_HEADS * LANE, GQA_KV_HEADS * GQA_HEAD_DIM
    return pl.pallas_call(
        _gqa_prep_kernel,
        grid=(b, t // tm),
        in_specs=[
            pl.BlockSpec((1, tm, qw), lambda i, j: (i, j, 0)),
            pl.BlockSpec((1, tm, kw), lambda i, j: (i, j, qw // kw)),
            pl.BlockSpec((1, tm, vw), lambda i, j: (i, j, (qw + kw) // vw)),
            pl.BlockSpec((tm, LANE), lambda i, j: (j, 0)),
            pl.BlockSpec((tm, LANE), lambda i, j: (j, 0)),
            pl.BlockSpec((1, LANE), lambda i, j: (0, 0)),
            pl.BlockSpec((1, LANE), lambda i, j: (0, 0)),
        ],
        out_specs=[
            pl.BlockSpec((1, tm, qw), lambda i, j: (i, j, 0)),
            pl.BlockSpec((1, tm, kw), lambda i, j: (i, j, 0)),
            pl.BlockSpec((1, GQA_KV_HEADS, 1, VT_ROWS, tm), lambda i, j: (i, 0, j, 0, 0)),
            pl.BlockSpec((1, 8, LANE), lambda i, j: (i, 0, 0)),
        ],
        out_shape=[
            jax.ShapeDtypeStruct((b, t, qw), BF16),
            jax.ShapeDtypeStruct((b, t, kw), BF16),
            jax.ShapeDtypeStruct((b, GQA_KV_HEADS, t // tm, VT_ROWS, tm), BF16),
            jax.ShapeDtypeStruct((b, 8, LANE), F32),
        ],
        compiler_params=_params(dimension_semantics=("arbitrary", "arbitrary")),
        name="gqa_prep",
    )(gqa_g, gqa_g, gqa_g, cos, sin, gq, gk)


def _mla_prep_kernel(cq_ref, ckv_ref, kr_ref, cos_ref, sin_ref, gq_ref, gkv_ref, wuq_ref, wuk_ref, wuv_ref,
                     qo_ref, ko_ref, vt_ref, kmax_ref):
    cos = cos_ref[...]
    sin = sin_ref[...]

    def rms(x, g):
        ms = jnp.mean(x * x, axis=-1, keepdims=True)
        return (x * lax.rsqrt(ms + EPS) * g).astype(BF16)

    q = _dot(rms(cq_ref[0], gq_ref[...]), wuq_ref[...])
    ckv = rms(ckv_ref[0], gkv_ref[...])
    kn = _dot(ckv, wuk_ref[...])
    v = _dot(ckv, wuv_ref[...])
    k_rope = _rope(kr_ref[0], cos, sin, MLA_ROPE // 4)
    norms = []
    for h in range(MLA_HEADS):
        sl = slice(h * LANE, (h + 1) * LANE)
        qo_ref[0, :, sl] = (_rope(q[:, sl], cos, sin, MLA_ROPE // 4) * (MLA_QK ** -0.5 * LOG2E)).astype(BF16)
        ko_ref[0, :, sl], norm = _finish_keys(kn[:, sl] + k_rope)
        norms.append(norm)
    _update_kmax(kmax_ref, norms)
    vt_ref[0, :, 0] = _vt_rows(v, MLA_HEADS, MLA_V)


def _mla_prep(mla_g, cos, sin, gq, gkv, wuq, wuk, wuv, tm):
    b, t, _ = mla_g.shape
    hw = MLA_HEADS * LANE
    return pl.pallas_call(
        _mla_prep_kernel,
        grid=(b, t // tm),
        in_specs=[
            pl.BlockSpec((1, tm, MLA_Q_LORA), lambda i, j: (i, j, 0)),
            pl.BlockSpec((1, tm, MLA_KV_LORA), lambda i, j: (i, j, MLA_Q_LORA // MLA_KV_LORA)),
            pl.BlockSpec((1, tm, LANE), lambda i, j: (i, j, (MLA_Q_LORA + MLA_KV_LORA) // LANE)),
            pl.BlockSpec((tm, LANE), lambda i, j: (j, 0)),
            pl.BlockSpec((tm, LANE), lambda i, j: (j, 0)),
            pl.BlockSpec((1, MLA_Q_LORA), lambda i, j: (0, 0)),
            pl.BlockSpec((1, MLA_KV_LORA), lambda i, j: (0, 0)),
            pl.BlockSpec((MLA_Q_LORA, hw), lambda i, j: (0, 0)),
            pl.BlockSpec((MLA_KV_LORA, hw), lambda i, j: (0, 0)),
            pl.BlockSpec((MLA_KV_LORA, MLA_HEADS * MLA_V), lambda i, j: (0, 0)),
        ],
        out_specs=[
            pl.BlockSpec((1, tm, hw), lambda i, j: (i, j, 0)),
            pl.BlockSpec((1, tm, hw), lambda i, j: (i, j, 0)),
            pl.BlockSpec((1, MLA_HEADS, 1, VT_ROWS, tm), lambda i, j: (i, 0, j, 0, 0)),
            pl.BlockSpec((1, 8, LANE), lambda i, j: (i, 0, 0)),
        ],
        out_shape=[
            jax.ShapeDtypeStruct((b, t, hw), BF16),
            jax.ShapeDtypeStruct((b, t, hw), BF16),
            jax.ShapeDtypeStruct((b, MLA_HEADS, t // tm, VT_ROWS, tm), BF16),
            jax.ShapeDtypeStruct((b, 8, LANE), F32),
        ],
        compiler_params=_params(dimension_semantics=("arbitrary", "arbitrary")),
        name="mla_prep",
    )(mla_g, mla_g, mla_g, cos, sin, gq, gkv, wuq, wuk, wuv)


def _attn_kernel(q_ref, k_ref, vt_ref, kmax_ref, o_ref, qa_ref, m_ref, acc_ref, *, n_q, rep, tk, dv):
    nk = k_ref.shape[1] // tk
    lane = lax.broadcasted_iota(jnp.int32, (1, LANE), 1)
    kmax = kmax_ref[0, 0:1, :]
    bound_max = None
    for h in range(n_q):
        qf = q_ref[0, :, h * LANE:(h + 1) * LANE].astype(F32)
        kg = jnp.max(jnp.where(lane == h // rep, kmax, 0.0), axis=-1, keepdims=True)
        bound = jnp.sqrt(jnp.sum(qf * qf, axis=-1, keepdims=True)) * kg
        qa_ref[h] = jnp.where(lane == ONES_LANE, -bound, qf).astype(BF16)
        top = jnp.max(bound)
        bound_max = top if bound_max is None else jnp.maximum(bound_max, top)
    acc_ref[...] = jnp.zeros(acc_ref.shape, F32)

    @pl.when(bound_max <= SCORE_BOUND_LIMIT)
    def _():
        def body(i, carry):
            ks = pl.multiple_of(i * tk, tk)
            for h in range(n_q):
                g = h // rep
                s = _dot_nt(k_ref[0, pl.ds(ks, tk), g * LANE:(g + 1) * LANE], qa_ref[h])
                acc_ref[h] += _dot(vt_ref[0, g, i], jnp.exp2(s).astype(BF16))
            return carry

        lax.fori_loop(0, nk, body, 0, unroll=2)

    @pl.when(bound_max > SCORE_BOUND_LIMIT)
    def _():
        m_ref[...] = jnp.full(m_ref.shape, -jnp.inf, F32)

        def body(i, carry):
            ks = pl.multiple_of(i * tk, tk)
            for h in range(n_q):
                g = h // rep
                s = _dot_nt(k_ref[0, pl.ds(ks, tk), g * LANE:(g + 1) * LANE], q_ref[0, :, h * LANE:(h + 1) * LANE])
                m_old = m_ref[h]
                m_new = jnp.maximum(m_old, jnp.max(s, axis=0, keepdims=True))
                p = jnp.exp2(s - m_new)
                acc_ref[h] = acc_ref[h] * jnp.exp2(m_old - m_new) + _dot(vt_ref[0, g, i], p.astype(BF16))
                m_ref[h] = m_new
            return carry

        lax.fori_loop(0, nk, body, 0)

    o = jnp.concatenate([acc_ref[h, 0:dv, :] * (1.0 / acc_ref[h, dv:dv + 1, :]) for h in range(n_q)], axis=0)
    o_ref[0] = o.T


def _attention(q, k, vt, kmax, rep, tq, tk, dv):
    b, t, qw = q.shape
    kw = k.shape[2]
    n_q = qw // LANE
    n_kv = n_q // rep
    ow = n_q * dv
    kern = functools.partial(_attn_kernel, n_q=n_q, rep=rep, tk=tk, dv=dv)
    return pl.pallas_call(
        kern,
        grid=(b, t // tq),
        in_specs=[
            pl.BlockSpec((1, tq, qw), lambda i, j: (i, j, 0)),
            pl.BlockSpec((1, t, kw), lambda i, j: (i, 0, 0)),
            pl.BlockSpec((1, n_kv, t // tk, VT_ROWS, tk), lambda i, j: (i, 0, 0, 0, 0)),
            pl.BlockSpec((1, 8, LANE), lambda i, j: (i, 0, 0)),
        ],
        out_specs=pl.BlockSpec((1, tq, ow), lambda i, j: (i, j, 0)),
        out_shape=jax.ShapeDtypeStruct((b, t, ow), F32),
        scratch_shapes=[pltpu.VMEM((n_q, tq, LANE), BF16), pltpu.VMEM((n_q, 1, tq), F32),
                        pltpu.VMEM((n_q, VT_ROWS, tq), F32)],
        compiler_params=_params(dimension_semantics=("arbitrary", "arbitrary")),
        name="attention",
    )(q, k, vt, kmax)


FF_CHUNK = 1024


def _outmlp_kernel(x_ref, m0_ref, m1_ref, m2_ref, m3_ref, wo_ref, g2_ref, w1_ref, w2_ref, gf_ref, o_ref, *, final):
    x1 = x_ref[...]
    for i, m_ref in enumerate((m0_ref, m1_ref, m2_ref, m3_ref)):
        x1 = x1 + _dot(m_ref[...].astype(BF16), wo_ref[i * GROUP_W:(i + 1) * GROUP_W, :])
    ms = jnp.mean(x1 * x1, axis=-1, keepdims=True)
    h = (x1 * lax.rsqrt(ms + EPS) * g2_ref[...]).astype(BF16)
    acc = None
    for c in range(D_FF // FF_CHUNK):
        sl = slice(c * FF_CHUNK, (c + 1) * FF_CHUNK)
        u = jnp.maximum(_dot(h, w1_ref[:, sl]), 0.0)
        part = _dot((u * u).astype(BF16), w2_ref[sl, :])
        acc = part if acc is None else acc + part
    y = x1 + acc
    if final:
        ms = jnp.mean(y * y, axis=-1, keepdims=True)
        y = y * lax.rsqrt(ms + EPS) * gf_ref[...]
    o_ref[...] = y


def _outmlp(x2d, mixes, wo, g2, w1, w2, gf, final, tm):
    n = x2d.shape[0]
    const = lambda i: (0, 0)
    single = pl.Buffered(1)
    kern = functools.partial(_outmlp_kernel, final=final)
    return pl.pallas_call(
        kern,
        grid=(n // tm,),
        in_specs=[pl.BlockSpec((tm, D_MODEL), lambda i: (i, 0))]
                 + [pl.BlockSpec((tm, GROUP_W), lambda i: (i, 0)) for _ in range(4)]
                 + [pl.BlockSpec((D_MODEL, D_MODEL), const, pipeline_mode=single),
                    pl.BlockSpec((1, D_MODEL), const),
                    pl.BlockSpec((D_MODEL, D_FF), const, pipeline_mode=single),
                    pl.BlockSpec((D_FF, D_MODEL), const, pipeline_mode=single),
                    pl.BlockSpec((1, D_MODEL), const)],
        out_specs=pl.BlockSpec((tm, D_MODEL), lambda i: (i, 0)),
        out_shape=jax.ShapeDtypeStruct((n, D_MODEL), F32),
        compiler_params=_params(dimension_semantics=("arbitrary",)),
        name="outmlp",
    )(x2d, *mixes, wo, g2.reshape(1, D_MODEL), w1, w2, gf.reshape(1, D_MODEL))


def _rope_tables(t):
    pos = jnp.arange(t, dtype=jnp.int32)
    row = (pos // GRID_W).astype(F32)
    col = (pos % GRID_W).astype(F32)

    def block(p, n):
        inv_freq = ROPE_THETA ** (-jnp.arange(n, dtype=F32) / n)
        ang = p[:, None] * inv_freq[None, :]
        c, s = jnp.cos(ang), jnp.sin(ang)
        return jnp.concatenate([c, c], axis=1), jnp.concatenate([-s, s], axis=1)

    def table(n, lead):
        cr, sr = block(row, n)
        cc, sc = block(col, n)
        tail = LANE - lead - 4 * n
        cos = jnp.concatenate([jnp.ones((t, lead), F32), cr, cc, jnp.ones((t, tail), F32)], axis=1)
        sin = jnp.concatenate([jnp.zeros((t, lead), F32), sr, sc, jnp.zeros((t, tail), F32)], axis=1)
        return cos, sin

    return table(GQA_HEAD_DIM // 4, 0), table(MLA_ROPE // 4, MLA_NOPE)


def _pad_lanes(v, width=LANE):
    return jnp.concatenate([v, jnp.zeros((width - v.shape[0],), v.dtype)]).reshape(1, width)


def _layer_params(i, p):
    (norm1_g, w_in, ssd_conv_w, ssd_conv_b, ssd_dt_bias, ssd_a_log, ssd_d, ssd_norm_g,
     gqa_q_norm_g, gqa_k_norm_g, gla_gate_w2, gla_gate_b, gla_norm_g, mla_q_norm_g, mla_w_uq,
     mla_kv_norm_g, mla_w_ukv, w_out, norm2_g, w_ff1, w_ff2) = [a[i] for a in p]
    out = {}
    out["norm1_g"] = norm1_g
    out["w_all"] = _gather_cols(w_in, _IN_COLS).astype(BF16)
    out["conv_w"] = jnp.concatenate([ssd_conv_w, jnp.zeros((HALO - SSD_CONV_W, SSD_XBC), F32)], axis=0)
    out["conv_b"] = ssd_conv_b.reshape(1, SSD_XBC)
    expand = lambda a: jnp.repeat(a, SSD_HEAD_DIM, axis=-1)
    out["dt_bias"] = expand(ssd_dt_bias).reshape(2, 1, GROUP_W)
    out["a_neg"] = expand(-jnp.exp(ssd_a_log)).reshape(2, 1, GROUP_W)
    out["ssd_d"] = expand(ssd_d).reshape(1, GROUP_W)
    out["ssd_norm_g"] = ssd_norm_g.reshape(1, GROUP_W)
    out["gq"] = _pad_lanes(gqa_q_norm_g)
    out["gk"] = _pad_lanes(gqa_k_norm_g)
    w2p = jnp.zeros((2, LANE, GLA_QK_W), F32)
    for d in range(2):
        w2p = w2p.at[d, d * GLA_LOWRANK:(d + 1) * GLA_LOWRANK, :].set(gla_gate_w2[d])
    out["gla_w2"] = w2p.astype(BF16)
    out["gla_b"] = gla_gate_b.reshape(2, 1, GLA_QK_W)
    out["gla_norm_g"] = jnp.tile(gla_norm_g, GLA_HEADS).reshape(1, GLA_V_W)
    out["mla_gq"] = mla_q_norm_g.reshape(1, MLA_Q_LORA)
    out["mla_gkv"] = mla_kv_norm_g.reshape(1, MLA_KV_LORA)
    uq_cols, uk_cols, uv_cols = [], [], []
    for h in range(MLA_HEADS):
        uq_cols += list(range(h * MLA_QK, (h + 1) * MLA_QK)) + [-1] * (LANE - MLA_QK)
        base = h * (MLA_NOPE + MLA_V)
        uk_cols += list(range(base, base + MLA_NOPE)) + [-1] * (LANE - MLA_NOPE)
        uv_cols += list(range(base + MLA_NOPE, base + MLA_NOPE + MLA_V))
    out["wuq"] = _gather_cols(mla_w_uq, np.asarray(uq_cols, np.int32)).astype(BF16)
    out["wuk"] = _gather_cols(mla_w_ukv, np.asarray(uk_cols, np.int32)).astype(BF16)
    out["wuv"] = _gather_cols(mla_w_ukv, np.asarray(uv_cols, np.int32)).astype(BF16)
    out["w_out"] = w_out.astype(BF16)
    out["norm2_g"] = norm2_g
    out["w_ff1"] = w_ff1.astype(BF16)
    out["w_ff2"] = w_ff2.astype(BF16)
    return out


TILE_PREF = dict(tm=256, tc=256, tp=512, tq_gqa=512, tq_mla=512, tmlp=512)


def _tiles(t):
    pick = lambda pref: max(c for c in (128, 256, 512, 1024) if c <= pref and t % c == 0)
    return {name: pick(pref) for name, pref in TILE_PREF.items()}


def _trunk(x, layers, final_norm_g):
    b, t, d = x.shape
    n = b * t
    ts = _tiles(t)
    tk = ts["tp"]
    (cos_g, sin_g), (cos_m, sin_m) = _rope_tables(t)
    avg = jnp.asarray(np.kron(np.eye(GLA_HEADS), np.full((GLA_DV, GLA_DV), 1.0 / GLA_DV)), BF16)
    x2d = x.reshape(n, d)
    for i, lp in enumerate(layers):
        ssd_g, gqa_g, gla_g, mla_g = _inproj(x2d, lp["norm1_g"], lp["w_all"], ts["tm"])
        ssd_out = _ssd(ssd_g.reshape(b, t, SSD_W), lp["conv_w"], lp["conv_b"], lp["dt_bias"], lp["a_neg"],
                       lp["ssd_d"], lp["ssd_norm_g"], ts["tc"])
        gq, gk, gvt, gkmax = _gqa_prep(gqa_g.reshape(b, t, GQA_W), cos_g, sin_g, lp["gq"], lp["gk"], tk)
        gqa_out = _attention(gq, gk, gvt, gkmax, GQA_HEADS // GQA_KV_HEADS, ts["tq_gqa"], tk, GQA_HEAD_DIM)
        gla_out = _gla(gla_g.reshape(b, t, GLA_W), lp["gla_w2"], lp["gla_b"], lp["gla_norm_g"], avg, ts["tc"])
        mq, mk, mvt, mkmax = _mla_prep(mla_g.reshape(b, t, MLA_W), cos_m, sin_m, lp["mla_gq"], lp["mla_gkv"],
                                       lp["wuq"], lp["wuk"], lp["wuv"], tk)
        mla_out = _attention(mq, mk, mvt, mkmax, 1, ts["tq_mla"], tk, MLA_V)
        mixes = [m.reshape(n, GROUP_W) for m in (ssd_out, gqa_out, gla_out, mla_out)]
        x2d = _outmlp(x2d, mixes, lp["w_out"], lp["norm2_g"], lp["w_ff1"], lp["w_ff2"], final_norm_g,
                      i == len(layers) - 1, ts["tmlp"])
    return x2d.reshape(b, t, d)


def kernel(x_prompt, x_sample, norm1_g, w_in, ssd_conv_w, ssd_conv_b, ssd_dt_bias, ssd_a_log, ssd_d, ssd_norm_g,
           gqa_q_norm_g, gqa_k_norm_g, gla_gate_w2, gla_gate_b, gla_norm_g, mla_q_norm_g, mla_w_uq,
           mla_kv_norm_g, mla_w_ukv, w_out, norm2_g, w_ff1, w_ff2, final_norm_g):
    stacked = (norm1_g, w_in, ssd_conv_w, ssd_conv_b, ssd_dt_bias, ssd_a_log, ssd_d, ssd_norm_g,
               gqa_q_norm_g, gqa_k_norm_g, gla_gate_w2, gla_gate_b, gla_norm_g, mla_q_norm_g, mla_w_uq,
               mla_kv_norm_g, mla_w_ukv, w_out, norm2_g, w_ff1, w_ff2)
    layers = [_layer_params(i, stacked) for i in range(norm1_g.shape[0])]
    return (_trunk(x_prompt, layers, final_norm_g), _trunk(x_sample, layers, final_norm_g))
```

```python
import functools

import numpy as np
import jax
import jax.numpy as jnp
from jax import lax
from jax.experimental import pallas as pl
from jax.experimental.pallas import tpu as pltpu

F32 = jnp.float32
BF16 = jnp.bfloat16

D_MODEL = 1024
DEPTH = 2
GRID_W = 64
CHUNK = 128
ROPE_THETA = 10000.0
EPS = 1e-6
GROUP_W = D_MODEL // 4
D_FF = 4 * D_MODEL

SSD_HEADS = 4
SSD_HEAD_DIM = GROUP_W // SSD_HEADS
SSD_GROUPS = 2
SSD_STATE = 128
SSD_CONV_W = 5
SSD_XBC = GROUP_W + 2 * SSD_GROUPS * SSD_STATE

GQA_HEADS = 4
GQA_KV_HEADS = 2
GQA_HEAD_DIM = GROUP_W // GQA_HEADS

GLA_HEADS = 4
GLA_DV = GROUP_W // GLA_HEADS
GLA_DK = GLA_DV // 2
GLA_LOWRANK = 16
GLA_TAU = 16.0

MLA_HEADS = 4
MLA_Q_LORA = 256
MLA_KV_LORA = 128
MLA_NOPE = 64
MLA_ROPE = 32
MLA_V = GROUP_W // MLA_HEADS
MLA_QK = MLA_NOPE + MLA_ROPE

IN_SIZES = (GROUP_W, SSD_XBC, 2 * SSD_HEADS,
            GQA_HEADS * GQA_HEAD_DIM, GQA_KV_HEADS * GQA_HEAD_DIM, GQA_KV_HEADS * GQA_HEAD_DIM,
            GLA_HEADS * GLA_DK, GLA_HEADS * GLA_DK, GLA_HEADS * GLA_DV, GLA_HEADS * GLA_DV, 2 * GLA_LOWRANK,
            MLA_Q_LORA, MLA_KV_LORA, MLA_ROPE)
IN_OFFS = tuple(int(v) for v in np.concatenate([[0], np.cumsum(IN_SIZES)]))

LOG2E = 1.4426950408889634
LANE = 128
HALO = 8
VMEM_LIMIT = 52 * 1024 * 1024

SSD_W = SSD_XBC + GROUP_W + 2 * GROUP_W
GQA_W = GQA_HEADS * LANE + GQA_KV_HEADS * LANE + GQA_KV_HEADS * GQA_HEAD_DIM
GLA_W = 2 * GLA_HEADS * GLA_DK + 2 * GLA_HEADS * GLA_DV + LANE
MLA_W = MLA_Q_LORA + MLA_KV_LORA + LANE
PROJ_W = SSD_W + GQA_W + GLA_W + MLA_W


def _in_proj_columns():
    o = IN_OFFS
    cols = []
    cols += list(range(o[1], o[2]))
    cols += list(range(o[0], o[1]))
    for d in range(2):
        for h in range(SSD_HEADS):
            cols += [o[2] + d * SSD_HEADS + h] * SSD_HEAD_DIM
    for h in range(GQA_HEADS):
        cols += list(range(o[3] + h * GQA_HEAD_DIM, o[3] + (h + 1) * GQA_HEAD_DIM)) + [-1] * (LANE - GQA_HEAD_DIM)
    for h in range(GQA_KV_HEADS):
        cols += list(range(o[4] + h * GQA_HEAD_DIM, o[4] + (h + 1) * GQA_HEAD_DIM)) + [-1] * (LANE - GQA_HEAD_DIM)
    cols += list(range(o[5], o[6]))
    cols += list(range(o[6], o[10]))
    cols += list(range(o[10], o[11])) + [-1] * (LANE - 2 * GLA_LOWRANK)
    cols += list(range(o[11], o[13]))
    cols += [-1] * MLA_NOPE + list(range(o[13], o[14])) + [-1] * (LANE - MLA_QK)
    cols = np.asarray(cols, np.int32)
    assert cols.shape[0] == PROJ_W
    return cols


_IN_COLS = _in_proj_columns()


def _gather_cols(w, cols):
    picked = jnp.take(w, jnp.asarray(np.maximum(cols, 0)), axis=1)
    return jnp.where(jnp.asarray(cols >= 0)[None, :], picked, 0.0)


def _params(**kw):
    return pltpu.CompilerParams(vmem_limit_bytes=VMEM_LIMIT, **kw)


def _silu(x):
    return x * (1.0 / (1.0 + jnp.exp(-x)))


def _softplus(x):
    return jnp.maximum(x, 0.0) + jnp.log1p(jnp.exp(-jnp.abs(x)))


def _dot(a, b):
    return jnp.dot(a, b, preferred_element_type=F32)


def _dot_nt(a, b):
    return lax.dot_general(a, b, (((1,), (1,)), ((), ())), preferred_element_type=F32)


def _dot_tn(a, b):
    return lax.dot_general(a, b, (((0,), (0,)), ((), ())), preferred_element_type=F32)


def _split_dot(m_bf16, x):
    x1 = x.astype(BF16)
    r1 = x - x1.astype(F32)
    x2 = r1.astype(BF16)
    x3 = (r1 - x2.astype(F32)).astype(BF16)
    return _dot(m_bf16, x1) + _dot(m_bf16, x2) + _dot(m_bf16, x3)


def _tri(reverse):
    li = lax.broadcasted_iota(jnp.int32, (CHUNK, CHUNK), 0)
    si = lax.broadcasted_iota(jnp.int32, (CHUNK, CHUNK), 1)
    return (si >= li) if reverse else (si <= li)


def _inproj_kernel(x_ref, g_ref, w_ref, ssd_ref, gqa_ref, gla_ref, mla_ref):
    x = x_ref[...]
    ms = jnp.mean(x * x, axis=-1, keepdims=True)
    h = (x * lax.rsqrt(ms + EPS) * g_ref[...]).astype(BF16)
    c0 = 0
    for ref, w in ((ssd_ref, SSD_W), (gqa_ref, GQA_W), (gla_ref, GLA_W), (mla_ref, MLA_W)):
        ref[...] = _dot(h, w_ref[:, c0:c0 + w])
        c0 += w


def _inproj(x2d, g, w_all, tm):
    n = x2d.shape[0]
    out_w = (SSD_W, GQA_W, GLA_W, MLA_W)
    return pl.pallas_call(
        _inproj_kernel,
        grid=(n // tm,),
        in_specs=[pl.BlockSpec((tm, D_MODEL), lambda i: (i, 0)),
                  pl.BlockSpec((1, D_MODEL), lambda i: (0, 0)),
                  pl.BlockSpec((D_MODEL, PROJ_W), lambda i: (0, 0), pipeline_mode=pl.Buffered(1))],
        out_specs=[pl.BlockSpec((tm, w), lambda i: (i, 0)) for w in out_w],
        out_shape=[jax.ShapeDtypeStruct((n, w), F32) for w in out_w],
        compiler_params=_params(dimension_semantics=("arbitrary",)),
        name="inproj",
    )(x2d, g.reshape(1, D_MODEL), w_all)


def _ssd_kernel(xbc_ref, prev_ref, next_ref, z_ref, dtx_ref, cw_ref, cb_ref, dtb_ref, a_ref, d_ref, g_ref,
                out_ref, yf_ref, h_ref, xe_ref, *, nt, tc):
    j = pl.program_id(1)
    tile = jnp.where(j < nt, j, 2 * nt - 1 - j)

    @pl.when((j == 0) | (j == nt))
    def _():
        h_ref[...] = jnp.zeros_like(h_ref)

    xe_ref[HALO:HALO + tc, :] = xbc_ref[0]
    xe_ref[0:HALO, :] = jnp.where(tile > 0, prev_ref[0], 0.0)
    xe_ref[HALO + tc:2 * HALO + tc, :] = jnp.where(tile < nt - 1, next_ref[0], 0.0)
    pad = SSD_CONV_W // 2
    acc = cb_ref[...] + cw_ref[0:1, :] * xe_ref[HALO - pad:HALO - pad + tc, :]
    for w in range(1, SSD_CONV_W):
        acc = acc + cw_ref[w:w + 1, :] * xe_ref[HALO - pad + w:HALO - pad + w + tc, :]
    xbc = _silu(acc)
    xs = xbc[:, :GROUP_W]
    dt = _softplus(dtx_ref[0] + dtb_ref[0])
    a_dt = dt * a_ref[0]
    xdt = xs * dt
    lane_head = lax.broadcasted_iota(jnp.int32, (1, GROUP_W), 1) // SSD_HEAD_DIM

    def chunk(c, reverse):
        sl = slice(c * CHUNK, (c + 1) * CHUNK)
        tri = _tri(reverse)
        acs = _split_dot(tri.astype(BF16), a_dt[sl])
        tot = acs[0:1] if reverse else acs[CHUNK - 1:CHUNK]
        acs_t = acs.T
        x_c = xdt[sl]
        b_c = [xbc[sl, GROUP_W + g * SSD_STATE:GROUP_W + (g + 1) * SSD_STATE].astype(BF16)
               for g in range(SSD_GROUPS)]
        c_c = [xbc[sl, GROUP_W + (SSD_GROUPS + g) * SSD_STATE:GROUP_W + (SSD_GROUPS + g + 1) * SSD_STATE].astype(BF16)
               for g in range(SSD_GROUPS)]
        y = jnp.zeros((CHUNK, GROUP_W), F32)
        for g in range(SSD_GROUPS):
            scores = _dot_nt(c_c[g], b_c[g])
            for r in range(SSD_HEADS // SSD_GROUPS):
                hd = g * (SSD_HEADS // SSD_GROUPS) + r
                col = jnp.broadcast_to(acs[:, hd * SSD_HEAD_DIM:hd * SSD_HEAD_DIM + 1], (CHUNK, CHUNK))
                row = jnp.broadcast_to(acs_t[hd * SSD_HEAD_DIM:hd * SSD_HEAD_DIM + 1, :], (CHUNK, CHUNK))
                decay = jnp.exp(jnp.where(tri, col - row, -jnp.inf))
                x_h = jnp.where(lane_head == hd, x_c, 0.0).astype(BF16)
                y = y + _dot((scores * decay).astype(BF16), x_h)
        h_in = h_ref[...]
        half = GROUP_W // SSD_GROUPS
        y_off = jnp.concatenate([_dot(c_c[g], h_in[:, g * half:(g + 1) * half].astype(BF16))
                                 for g in range(SSD_GROUPS)], axis=1)
        y = y + y_off * jnp.exp(acs)
        x_d = (x_c * jnp.exp(tot - acs)).astype(BF16)
        st = jnp.concatenate([_dot_tn(b_c[g], x_d[:, g * half:(g + 1) * half]) for g in range(SSD_GROUPS)], axis=1)
        h_ref[...] = h_in * jnp.exp(tot) + st
        return y

    nc = tc // CHUNK

    @pl.when(j < nt)
    def _():
        for c in range(nc):
            start = pl.multiple_of(tile * tc + c * CHUNK, CHUNK)
            yf_ref[pl.ds(start, CHUNK), :] = chunk(c, False)

    @pl.when(j >= nt)
    def _():
        for c in reversed(range(nc)):
            sl = slice(c * CHUNK, (c + 1) * CHUNK)
            start = pl.multiple_of(tile * tc + c * CHUNK, CHUNK)
            y = yf_ref[pl.ds(start, CHUNK), :] + chunk(c, True) + d_ref[...] * xs[sl]
            gated = y * _silu(z_ref[0, sl, :])
            ms = jnp.mean(gated * gated, axis=-1, keepdims=True)
            out_ref[0, sl, :] = gated * lax.rsqrt(ms + EPS) * g_ref[...]


def _ssd(ssd_g, cw, cb, dtb, a_exp, d_exp, g, tc):
    b, t, _ = ssd_g.shape
    nt = t // tc
    hb = tc // HALO

    def tile_of(j):
        return jnp.where(j < nt, j, 2 * nt - 1 - j)

    kern = functools.partial(_ssd_kernel, nt=nt, tc=tc)
    return pl.pallas_call(
        kern,
        grid=(b, 2 * nt),
        in_specs=[
            pl.BlockSpec((1, tc, SSD_XBC), lambda i, j: (i, tile_of(j), 0)),
            pl.BlockSpec((1, HALO, SSD_XBC), lambda i, j: (i, jnp.maximum(tile_of(j) * hb - 1, 0), 0)),
            pl.BlockSpec((1, HALO, SSD_XBC), lambda i, j: (i, jnp.minimum((tile_of(j) + 1) * hb, t // HALO - 1), 0)),
            pl.BlockSpec((1, tc, GROUP_W), lambda i, j: (i, tile_of(j), SSD_XBC // GROUP_W)),
            pl.BlockSpec((1, tc, GROUP_W), lambda i, j: (i, tile_of(j), SSD_XBC // GROUP_W + 1 + j // nt)),
            pl.BlockSpec((HALO, SSD_XBC), lambda i, j: (0, 0)),
            pl.BlockSpec((1, SSD_XBC), lambda i, j: (0, 0)),
            pl.BlockSpec((1, 1, GROUP_W), lambda i, j: (j // nt, 0, 0)),
            pl.BlockSpec((1, 1, GROUP_W), lambda i, j: (j // nt, 0, 0)),
            pl.BlockSpec((1, GROUP_W), lambda i, j: (0, 0)),
            pl.BlockSpec((1, GROUP_W), lambda i, j: (0, 0)),
        ],
        out_specs=pl.BlockSpec((1, tc, GROUP_W), lambda i, j: (i, jnp.where(j < nt, nt - 1, 2 * nt - 1 - j), 0)),
        out_shape=jax.ShapeDtypeStruct((b, t, GROUP_W), F32),
        scratch_shapes=[pltpu.VMEM((t, GROUP_W), F32),
                        pltpu.VMEM((SSD_STATE, GROUP_W), F32),
                        pltpu.VMEM((tc + 2 * HALO, SSD_XBC), F32)],
        compiler_params=_params(dimension_semantics=("arbitrary", "arbitrary")),
        name="ssd",
    )(ssd_g, ssd_g, ssd_g, ssd_g, ssd_g, cw, cb, dtb, a_exp, d_exp, g)


GLA_QK_W = GLA_HEADS * GLA_DK
GLA_V_W = GLA_HEADS * GLA_DV


def _gla_kernel(q_ref, k_ref, v_ref, r_ref, lr_ref, w2_ref, gb_ref, g_ref, avg_ref,
                out_ref, of_ref, s_ref, *, nt, tc):
    j = pl.program_id(1)
    tile = jnp.where(j < nt, j, 2 * nt - 1 - j)

    @pl.when((j == 0) | (j == nt))
    def _():
        s_ref[...] = jnp.zeros_like(s_ref)

    logit = _dot(lr_ref[0].astype(BF16), w2_ref[0]) + gb_ref[0]
    logg = -_softplus(-logit) * (1.0 / GLA_TAU)
    q = q_ref[0] * (GLA_DK ** -0.5)
    k = k_ref[0]
    v = v_ref[0]
    qk_head = lax.broadcasted_iota(jnp.int32, (1, GLA_QK_W), 1) // GLA_DK
    v_head = lax.broadcasted_iota(jnp.int32, (1, GLA_V_W), 1) // GLA_DV
    blockdiag = (lax.broadcasted_iota(jnp.int32, (GLA_QK_W, GLA_V_W), 0) // GLA_DK
                 == lax.broadcasted_iota(jnp.int32, (GLA_QK_W, GLA_V_W), 1) // GLA_DV)

    def chunk(c, reverse):
        sl = slice(c * CHUNK, (c + 1) * CHUNK)
        tri = _tri(reverse)
        gcs = _split_dot(tri.astype(BF16), logg[sl])
        last = 0 if reverse else CHUNK - 1
        tot = gcs[last:last + 1]
        tot_col = gcs.T[:, last:last + 1]
        qg = q[sl] * jnp.exp(gcs)
        kg = (k[sl] * jnp.exp(-gcs)).astype(BF16)
        v_c = v[sl]
        o = jnp.zeros((CHUNK, GLA_V_W), F32)
        for hd in range(GLA_HEADS):
            q_h = jnp.where(qk_head == hd, qg, 0.0).astype(BF16)
            att = jnp.where(tri, _dot_nt(q_h, kg), 0.0)
            v_h = jnp.where(v_head == hd, v_c, 0.0).astype(BF16)
            o = o + _dot(att.astype(BF16), v_h)
        s_in = s_ref[...]
        o = o + _dot(qg.astype(BF16), s_in.astype(BF16))
        kd = (k[sl] * jnp.exp(tot - gcs)).astype(BF16)
        st = jnp.where(blockdiag, _dot_tn(kd, v_c.astype(BF16)), 0.0)
        s_ref[...] = s_in * jnp.exp(tot_col) + st
        return o

    nc = tc // CHUNK

    @pl.when(j < nt)
    def _():
        for c in range(nc):
            start = pl.multiple_of(tile * tc + c * CHUNK, CHUNK)
            of_ref[pl.ds(start, CHUNK), :] = chunk(c, False)

    @pl.when(j >= nt)
    def _():
        for c in reversed(range(nc)):
            sl = slice(c * CHUNK, (c + 1) * CHUNK)
            start = pl.multiple_of(tile * tc + c * CHUNK, CHUNK)
            o = of_ref[pl.ds(start, CHUNK), :] + chunk(c, True)
            ms = _split_dot_right(o * o, avg_ref[...])
            out_ref[0, sl, :] = o * lax.rsqrt(ms + EPS) * g_ref[...] * _silu(r_ref[0, sl, :])


def _split_dot_right(x, m_bf16):
    x1 = x.astype(BF16)
    r1 = x - x1.astype(F32)
    x2 = r1.astype(BF16)
    x3 = (r1 - x2.astype(F32)).astype(BF16)
    return _dot(x1, m_bf16) + _dot(x2, m_bf16) + _dot(x3, m_bf16)


def _gla(gla_g, w2p, gb, g, avg, tc):
    b, t, _ = gla_g.shape
    nt = t // tc

    def tile_of(j):
        return jnp.where(j < nt, j, 2 * nt - 1 - j)

    kern = functools.partial(_gla_kernel, nt=nt, tc=tc)
    return pl.pallas_call(
        kern,
        grid=(b, 2 * nt),
        in_specs=[
            pl.BlockSpec((1, tc, GLA_QK_W), lambda i, j: (i, tile_of(j), 0)),
            pl.BlockSpec((1, tc, GLA_QK_W), lambda i, j: (i, tile_of(j), 1)),
            pl.BlockSpec((1, tc, GLA_V_W), lambda i, j: (i, tile_of(j), 1)),
            pl.BlockSpec((1, tc, GLA_V_W), lambda i, j: (i, tile_of(j), 2)),
            pl.BlockSpec((1, tc, LANE), lambda i, j: (i, tile_of(j), (2 * GLA_QK_W + 2 * GLA_V_W) // LANE)),
            pl.BlockSpec((1, LANE, GLA_QK_W), lambda i, j: (j // nt, 0, 0)),
            pl.BlockSpec((1, 1, GLA_QK_W), lambda i, j: (j // nt, 0, 0)),
            pl.BlockSpec((1, GLA_V_W), lambda i, j: (0, 0)),
            pl.BlockSpec((GLA_V_W, GLA_V_W), lambda i, j: (0, 0)),
        ],
        out_specs=pl.BlockSpec((1, tc, GLA_V_W), lambda i, j: (i, jnp.where(j < nt, nt - 1, 2 * nt - 1 - j), 0)),
        out_shape=jax.ShapeDtypeStruct((b, t, GLA_V_W), F32),
        scratch_shapes=[pltpu.VMEM((t, GLA_V_W), F32),
                        pltpu.VMEM((GLA_QK_W, GLA_V_W), F32)],
        compiler_params=_params(dimension_semantics=("arbitrary", "arbitrary")),
        name="gla",
    )(gla_g, gla_g, gla_g, gla_g, gla_g, w2p, gb, g, avg)


def _rope(y, cos, sin, half):
    lane = lax.broadcasted_iota(jnp.int32, (1, LANE), 1)
    lo = (lane % (2 * half)) < half
    rot = jnp.where(lo, pltpu.roll(y, LANE - half, 1), pltpu.roll(y, half, 1))
    return y * cos + rot * sin


ONES_LANE = LANE - 1
VT_ROWS = 80
SCORE_BOUND_LIMIT = 48.0


def _finish_keys(k):
    kf = k.astype(BF16).astype(F32)
    norm = jnp.sqrt(jnp.max(jnp.sum(kf * kf, axis=-1, keepdims=True), axis=0, keepdims=True))
    lane = lax.broadcasted_iota(jnp.int32, (1, LANE), 1)
    return jnp.where(lane == ONES_LANE, 1.0, k).astype(BF16), norm


def _update_kmax(kmax_ref, norms):
    @pl.when(pl.program_id(1) == 0)
    def _():
        kmax_ref[...] = jnp.zeros_like(kmax_ref)

    lane = lax.broadcasted_iota(jnp.int32, (1, LANE), 1)
    upd = jnp.zeros((1, LANE), F32)
    for g, norm in enumerate(norms):
        upd = jnp.where(lane == g, norm, upd)
    kmax_ref[0] = jnp.maximum(kmax_ref[0], upd)


def _vt_rows(v, heads, dv):
    tm = v.shape[0]
    vt = v.T.reshape(heads, dv, tm)
    row = lax.broadcasted_iota(jnp.int32, (heads, VT_ROWS - dv, tm), 1)
    return jnp.concatenate([vt, jnp.where(row == 0, 1.0, 0.0)], axis=1).astype(BF16)


def _gqa_prep_kernel(q_ref, k_ref, v_ref, cos_ref, sin_ref, gq_ref, gk_ref, qo_ref, ko_ref, vt_ref, kmax_ref):
    cos = cos_ref[...]
    sin = sin_ref[...]

    def norm_rope(x, g, scale):
        ms = jnp.sum(x * x, axis=-1, keepdims=True) * (1.0 / GQA_HEAD_DIM)
        return _rope(x * lax.rsqrt(ms + EPS) * g, cos, sin, GQA_HEAD_DIM // 4) * scale

    for h in range(GQA_HEADS):
        sl = slice(h * LANE, (h + 1) * LANE)
        qo_ref[0, :, sl] = norm_rope(q_ref[0, :, sl], gq_ref[...], GQA_HEAD_DIM ** -0.5 * LOG2E).astype(BF16)
    norms = []
    for h in range(GQA_KV_HEADS):
        sl = slice(h * LANE, (h + 1) * LANE)
        ko_ref[0, :, sl], norm = _finish_keys(norm_rope(k_ref[0, :, sl], gk_ref[...], 1.0))
        norms.append(norm)
    _update_kmax(kmax_ref, norms)
    vt_ref[0, :, 0] = _vt_rows(v_ref[0], GQA_KV_HEADS, GQA_HEAD_DIM)


def _gqa_prep(gqa_g, cos, sin, gq, gk, tm):
    b, t, _ = gqa_g.shape
    qw, kw, vw = GQA_HEADS * LANE, GQA_KV_HEADS * LANE, GQA_KV_HEADS * GQA_HEAD_DIM
    return pl.pallas_call(
        _gqa_prep_kernel,
        grid=(b, t // tm),
        in_specs=[
            pl.BlockSpec((1, tm, qw), lambda i, j: (i, j, 0)),
            pl.BlockSpec((1, tm, kw), lambda i, j: (i, j, qw // kw)),
            pl.BlockSpec((1, tm, vw), lambda i, j: (i, j, (qw + kw) // vw)),
            pl.BlockSpec((tm, LANE), lambda i, j: (j, 0)),
            pl.BlockSpec((tm, LANE), lambda i, j: (j, 0)),
            pl.BlockSpec((1, LANE), lambda i, j: (0, 0)),
            pl.BlockSpec((1, LANE), lambda i, j: (0, 0)),
        ],
        out_specs=[
            pl.BlockSpec((1, tm, qw), lambda i, j: (i, j, 0)),
            pl.BlockSpec((1, tm, kw), lambda i, j: (i, j, 0)),
            pl.BlockSpec((1, GQA_KV_HEADS, 1, VT_ROWS, tm), lambda i, j: (i, 0, j, 0, 0)),
            pl.BlockSpec((1, 8, LANE), lambda i, j: (i, 0, 0)),
        ],
        out_shape=[
            jax.ShapeDtypeStruct((b, t, qw), BF16),
            jax.ShapeDtypeStruct((b, t, kw), BF16),
            jax.ShapeDtypeStruct((b, GQA_KV_HEADS, t // tm, VT_ROWS, tm), BF16),
            jax.ShapeDtypeStruct((b, 8, LANE), F32),
        ],
        compiler_params=_params(dimension_semantics=("arbitrary", "arbitrary")),
        name="gqa_prep",
    )(gqa_g, gqa_g, gqa_g, cos, sin, gq, gk)


def _mla_prep_kernel(cq_ref, ckv_ref, kr_ref, cos_ref, sin_ref, gq_ref, gkv_ref, wuq_ref, wuk_ref, wuv_ref,
                     qo_ref, ko_ref, vt_ref, kmax_ref):
    cos = cos_ref[...]
    sin = sin_ref[...]

    def rms(x, g):
        ms = jnp.mean(x * x, axis=-1, keepdims=True)
        return (x * lax.rsqrt(ms + EPS) * g).astype(BF16)

    q = _dot(rms(cq_ref[0], gq_ref[...]), wuq_ref[...])
    ckv = rms(ckv_ref[0], gkv_ref[...])
    kn = _dot(ckv, wuk_ref[...])
    v = _dot(ckv, wuv_ref[...])
    k_rope = _rope(kr_ref[0], cos, sin, MLA_ROPE // 4)
    norms = []
    for h in range(MLA_HEADS):
        sl = slice(h * LANE, (h + 1) * LANE)
        qo_ref[0, :, sl] = (_rope(q[:, sl], cos, sin, MLA_ROPE // 4) * (MLA_QK ** -0.5 * LOG2E)).astype(BF16)
        ko_ref[0, :, sl], norm = _finish_keys(kn[:, sl] + k_rope)
        norms.append(norm)
    _update_kmax(kmax_ref, norms)
    vt_ref[0, :, 0] = _vt_rows(v, MLA_HEADS, MLA_V)


def _mla_prep(mla_g, cos, sin, gq, gkv, wuq, wuk, wuv, tm):
    b, t, _ = mla_g.shape
    hw = MLA_HEADS * LANE
    return pl.pallas_call(
        _mla_prep_kernel,
        grid=(b, t // tm),
        in_specs=[
            pl.BlockSpec((1, tm, MLA_Q_LORA), lambda i, j: (i, j, 0)),
            pl.BlockSpec((1, tm, MLA_KV_LORA), lambda i, j: (i, j, MLA_Q_LORA // MLA_KV_LORA)),
            pl.BlockSpec((1, tm, LANE), lambda i, j: (i, j, (MLA_Q_LORA + MLA_KV_LORA) // LANE)),
            pl.BlockSpec((tm, LANE), lambda i, j: (j, 0)),
            pl.BlockSpec((tm, LANE), lambda i, j: (j, 0)),
            pl.BlockSpec((1, MLA_Q_LORA), lambda i, j: (0, 0)),
            pl.BlockSpec((1, MLA_KV_LORA), lambda i, j: (0, 0)),
            pl.BlockSpec((MLA_Q_LORA, hw), lambda i, j: (0, 0)),
            pl.BlockSpec((MLA_KV_LORA, hw), lambda i, j: (0, 0)),
            pl.BlockSpec((MLA_KV_LORA, MLA_HEADS * MLA_V), lambda i, j: (0, 0)),
        ],
        out_specs=[
            pl.BlockSpec((1, tm, hw), lambda i, j: (i, j, 0)),
            pl.BlockSpec((1, tm, hw), lambda i, j: (i, j, 0)),
            pl.BlockSpec((1, MLA_HEADS, 1, VT_ROWS, tm), lambda i, j: (i, 0, j, 0, 0)),
            pl.BlockSpec((1, 8, LANE), lambda i, j: (i, 0, 0)),
        ],
        out_shape=[
            jax.ShapeDtypeStruct((b, t, hw), BF16),
            jax.ShapeDtypeStruct((b, t, hw), BF16),
            jax.ShapeDtypeStruct((b, MLA_HEADS, t // tm, VT_ROWS, tm), BF16),
            jax.ShapeDtypeStruct((b, 8, LANE), F32),
        ],
        compiler_params=_params(dimension_semantics=("arbitrary", "arbitrary")),
        name="mla_prep",
    )(mla_g, mla_g, mla_g, cos, sin, gq, gkv, wuq, wuk, wuv)


def _attn_kernel(q_ref, k_ref, vt_ref, kmax_ref, o_ref, qa_ref, m_ref, acc_ref, *, n_q, rep, tk, dv):
    nk = k_ref.shape[1] // tk
    lane = lax.broadcasted_iota(jnp.int32, (1, LANE), 1)
    kmax = kmax_ref[0, 0:1, :]
    bound_max = None
    for h in range(n_q):
        qf = q_ref[0, :, h * LANE:(h + 1) * LANE].astype(F32)
        kg = jnp.max(jnp.where(lane == h // rep, kmax, 0.0), axis=-1, keepdims=True)
        bound = jnp.sqrt(jnp.sum(qf * qf, axis=-1, keepdims=True)) * kg
        qa_ref[h] = jnp.where(lane == ONES_LANE, -bound, qf).astype(BF16)
        top = jnp.max(bound)
        bound_max = top if bound_max is None else jnp.maximum(bound_max, top)
    acc_ref[...] = jnp.zeros(acc_ref.shape, F32)

    @pl.when(bound_max <= SCORE_BOUND_LIMIT)
    def _():
        def scores(i, h):
            ks = pl.multiple_of(i * tk, tk)
            g = h // rep
            return _dot_nt(k_ref[0, pl.ds(ks, tk), g * LANE:(g + 1) * LANE], qa_ref[h])

        group = max(u for u in (1, 2, 4) if nk % u == 0)

        def body(ii, carry):
            steps = [(ii * group + u, h) for u in range(group) for h in range(n_q)]
            s_next = scores(*steps[0])
            for n, (i, h) in enumerate(steps):
                s = s_next
                if n + 1 < len(steps):
                    s_next = scores(*steps[n + 1])
                acc_ref[h] += _dot(vt_ref[0, h // rep, i], jnp.exp2(s).astype(BF16))
            return carry

        lax.fori_loop(0, nk // group, body, 0)

    @pl.when(bound_max > SCORE_BOUND_LIMIT)
    def _():
        m_ref[...] = jnp.full(m_ref.shape, -jnp.inf, F32)

        def body(i, carry):
            ks = pl.multiple_of(i * tk, tk)
            for h in range(n_q):
                g = h // rep
                s = _dot_nt(k_ref[0, pl.ds(ks, tk), g * LANE:(g + 1) * LANE], q_ref[0, :, h * LANE:(h + 1) * LANE])
                m_old = m_ref[h]
                m_new = jnp.maximum(m_old, jnp.max(s, axis=0, keepdims=True))
                p = jnp.exp2(s - m_new)
                acc_ref[h] = acc_ref[h] * jnp.exp2(m_old - m_new) + _dot(vt_ref[0, g, i], p.astype(BF16))
                m_ref[h] = m_new
            return carry

        lax.fori_loop(0, nk, body, 0)

    o = jnp.concatenate([acc_ref[h, 0:dv, :] * (1.0 / acc_ref[h, dv:dv + 1, :]) for h in range(n_q)], axis=0)
    o_ref[0] = o.T


def _attention(q, k, vt, kmax, rep, tq, tk, dv):
    b, t, qw = q.shape
    kw = k.shape[2]
    n_q = qw // LANE
    n_kv = n_q // rep
    ow = n_q * dv
    kern = functools.partial(_attn_kernel, n_q=n_q, rep=rep, tk=tk, dv=dv)
    return pl.pallas_call(
        kern,
        grid=(b, t // tq),
        in_specs=[
            pl.BlockSpec((1, tq, qw), lambda i, j: (i, j, 0)),
            pl.BlockSpec((1, t, kw), lambda i, j: (i, 0, 0)),
            pl.BlockSpec((1, n_kv, t // tk, VT_ROWS, tk), lambda i, j: (i, 0, 0, 0, 0)),
            pl.BlockSpec((1, 8, LANE), lambda i, j: (i, 0, 0)),
        ],
        out_specs=pl.BlockSpec((1, tq, ow), lambda i, j: (i, j, 0)),
        out_shape=jax.ShapeDtypeStruct((b, t, ow), F32),
        scratch_shapes=[pltpu.VMEM((n_q, tq, LANE), BF16), pltpu.VMEM((n_q, 1, tq), F32),
                        pltpu.VMEM((n_q, VT_ROWS, tq), F32)],
        compiler_params=_params(dimension_semantics=("arbitrary", "arbitrary")),
        name="attention",
    )(q, k, vt, kmax)


FF_CHUNK = 1024


def _outmlp_kernel(x_ref, m0_ref, m1_ref, m2_ref, m3_ref, wo_ref, g2_ref, w1_ref, w2_ref, gf_ref, o_ref, *, final):
    x1 = x_ref[...]
    for i, m_ref in enumerate((m0_ref, m1_ref, m2_ref, m3_ref)):
        x1 = x1 + _dot(m_ref[...].astype(BF16), wo_ref[i * GROUP_W:(i + 1) * GROUP_W, :])
    ms = jnp.mean(x1 * x1, axis=-1, keepdims=True)
    h = (x1 * lax.rsqrt(ms + EPS) * g2_ref[...]).astype(BF16)
    acc = None
    for c in range(D_FF // FF_CHUNK):
        sl = slice(c * FF_CHUNK, (c + 1) * FF_CHUNK)
        u = jnp.maximum(_dot(h, w1_ref[:, sl]), 0.0)
        part = _dot((u * u).astype(BF16), w2_ref[sl, :])
        acc = part if acc is None else acc + part
    y = x1 + acc
    if final:
        ms = jnp.mean(y * y, axis=-1, keepdims=True)
        y = y * lax.rsqrt(ms + EPS) * gf_ref[...]
    o_ref[...] = y


def _outmlp(x2d, mixes, wo, g2, w1, w2, gf, final, tm):
    n = x2d.shape[0]
    const = lambda i: (0, 0)
    single = pl.Buffered(1)
    kern = functools.partial(_outmlp_kernel, final=final)
    return pl.pallas_call(
        kern,
        grid=(n // tm,),
        in_specs=[pl.BlockSpec((tm, D_MODEL), lambda i: (i, 0))]
                 + [pl.BlockSpec((tm, GROUP_W), lambda i: (i, 0)) for _ in range(4)]
                 + [pl.BlockSpec((D_MODEL, D_MODEL), const, pipeline_mode=single),
                    pl.BlockSpec((1, D_MODEL), const),
                    pl.BlockSpec((D_MODEL, D_FF), const, pipeline_mode=single),
                    pl.BlockSpec((D_FF, D_MODEL), const, pipeline_mode=single),
                    pl.BlockSpec((1, D_MODEL), const)],
        out_specs=pl.BlockSpec((tm, D_MODEL), lambda i: (i, 0)),
        out_shape=jax.ShapeDtypeStruct((n, D_MODEL), F32),
        compiler_params=_params(dimension_semantics=("arbitrary",)),
        name="outmlp",
    )(x2d, *mixes, wo, g2.reshape(1, D_MODEL), w1, w2, gf.reshape(1, D_MODEL))


def _rope_tables(t):
    pos = jnp.arange(t, dtype=jnp.int32)
    row = (pos // GRID_W).astype(F32)
    col = (pos % GRID_W).astype(F32)

    def block(p, n):
        inv_freq = ROPE_THETA ** (-jnp.arange(n, dtype=F32) / n)
        ang = p[:, None] * inv_freq[None, :]
        c, s = jnp.cos(ang), jnp.sin(ang)
        return jnp.concatenate([c, c], axis=1), jnp.concatenate([-s, s], axis=1)

    def table(n, lead):
        cr, sr = block(row, n)
        cc, sc = block(col, n)
        tail = LANE - lead - 4 * n
        cos = jnp.concatenate([jnp.ones((t, lead), F32), cr, cc, jnp.ones((t, tail), F32)], axis=1)
        sin = jnp.concatenate([jnp.zeros((t, lead), F32), sr, sc, jnp.zeros((t, tail), F32)], axis=1)
        return cos, sin

    return table(GQA_HEAD_DIM // 4, 0), table(MLA_ROPE // 4, MLA_NOPE)


def _pad_lanes(v, width=LANE):
    return jnp.concatenate([v, jnp.zeros((width - v.shape[0],), v.dtype)]).reshape(1, width)


def _layer_params(i, p):
    (norm1_g, w_in, ssd_conv_w, ssd_conv_b, ssd_dt_bias, ssd_a_log, ssd_d, ssd_norm_g,
     gqa_q_norm_g, gqa_k_norm_g, gla_gate_w2, gla_gate_b, gla_norm_g, mla_q_norm_g, mla_w_uq,
     mla_kv_norm_g, mla_w_ukv, w_out, norm2_g, w_ff1, w_ff2) = [a[i] for a in p]
    out = {}
    out["norm1_g"] = norm1_g
    out["w_all"] = _gather_cols(w_in, _IN_COLS).astype(BF16)
    out["conv_w"] = jnp.concatenate([ssd_conv_w, jnp.zeros((HALO - SSD_CONV_W, SSD_XBC), F32)], axis=0)
    out["conv_b"] = ssd_conv_b.reshape(1, SSD_XBC)
    expand = lambda a: jnp.repeat(a, SSD_HEAD_DIM, axis=-1)
    out["dt_bias"] = expand(ssd_dt_bias).reshape(2, 1, GROUP_W)
    out["a_neg"] = expand(-jnp.exp(ssd_a_log)).reshape(2, 1, GROUP_W)
    out["ssd_d"] = expand(ssd_d).reshape(1, GROUP_W)
    out["ssd_norm_g"] = ssd_norm_g.reshape(1, GROUP_W)
    out["gq"] = _pad_lanes(gqa_q_norm_g)
    out["gk"] = _pad_lanes(gqa_k_norm_g)
    w2p = jnp.zeros((2, LANE, GLA_QK_W), F32)
    for d in range(2):
        w2p = w2p.at[d, d * GLA_LOWRANK:(d + 1) * GLA_LOWRANK, :].set(gla_gate_w2[d])
    out["gla_w2"] = w2p.astype(BF16)
    out["gla_b"] = gla_gate_b.reshape(2, 1, GLA_QK_W)
    out["gla_norm_g"] = jnp.tile(gla_norm_g, GLA_HEADS).reshape(1, GLA_V_W)
    out["mla_gq"] = mla_q_norm_g.reshape(1, MLA_Q_LORA)
    out["mla_gkv"] = mla_kv_norm_g.reshape(1, MLA_KV_LORA)
    uq_cols, uk_cols, uv_cols = [], [], []
    for h in range(MLA_HEADS):
        uq_cols += list(range(h * MLA_QK, (h + 1) * MLA_QK)) + [-1] * (LANE - MLA_QK)
        base = h * (MLA_NOPE + MLA_V)
        uk_cols += list(range(base, base + MLA_NOPE)) + [-1] * (LANE - MLA_NOPE)
        uv_cols += list(range(base + MLA_NOPE, base + MLA_NOPE + MLA_V))
    out["wuq"] = _gather_cols(mla_w_uq, np.asarray(uq_cols, np.int32)).astype(BF16)
    out["wuk"] = _gather_cols(mla_w_ukv, np.asarray(uk_cols, np.int32)).astype(BF16)
    out["wuv"] = _gather_cols(mla_w_ukv, np.asarray(uv_cols, np.int32)).astype(BF16)
    out["w_out"] = w_out.astype(BF16)
    out["norm2_g"] = norm2_g
    out["w_ff1"] = w_ff1.astype(BF16)
    out["w_ff2"] = w_ff2.astype(BF16)
    return out


TILE_PREF = dict(tm=256, tc=256, tp=512, tq_gqa=512, tq_mla=512, tmlp=512)


def _tiles(t):
    pick = lambda pref: max(c for c in (128, 256, 512, 1024) if c <= pref and t % c == 0)
    return {name: pick(pref) for name, pref in TILE_PREF.items()}


def _trunk(x, layers, final_norm_g):
    b, t, d = x.shape
    n = b * t
    ts = _tiles(t)
    tk = ts["tp"]
    (cos_g, sin_g), (cos_m, sin_m) = _rope_tables(t)
    avg = jnp.asarray(np.kron(np.eye(GLA_HEADS), np.full((GLA_DV, GLA_DV), 1.0 / GLA_DV)), BF16)
    x2d = x.reshape(n, d)
    for i, lp in enumerate(layers):
        ssd_g, gqa_g, gla_g, mla_g = _inproj(x2d, lp["norm1_g"], lp["w_all"], ts["tm"])
        ssd_out = _ssd(ssd_g.reshape(b, t, SSD_W), lp["conv_w"], lp["conv_b"], lp["dt_bias"], lp["a_neg"],
                       lp["ssd_d"], lp["ssd_norm_g"], ts["tc"])
        gq, gk, gvt, gkmax = _gqa_prep(gqa_g.reshape(b, t, GQA_W), cos_g, sin_g, lp["gq"], lp["gk"], tk)
        gqa_out = _attention(gq, gk, gvt, gkmax, GQA_HEADS // GQA_KV_HEADS, ts["tq_gqa"], tk, GQA_HEAD_DIM)
        gla_out = _gla(gla_g.reshape(b, t, GLA_W), lp["gla_w2"], lp["gla_b"], lp["gla_norm_g"], avg, ts["tc"])
        mq, mk, mvt, mkmax = _mla_prep(mla_g.reshape(b, t, MLA_W), cos_m, sin_m, lp["mla_gq"], lp["mla_gkv"],
                                       lp["wuq"], lp["wuk"], lp["wuv"], tk)
        mla_out = _attention(mq, mk, mvt, mkmax, 1, ts["tq_mla"], tk, MLA_V)
        mixes = [m.reshape(n, GROUP_W) for m in (ssd_out, gqa_out, gla_out, mla_out)]
        x2d = _outmlp(x2d, mixes, lp["w_out"], lp["norm2_g"], lp["w_ff1"], lp["w_ff2"], final_norm_g,
                      i == len(layers) - 1, ts["tmlp"])
    return x2d.reshape(b, t, d)


def kernel(x_prompt, x_sample, norm1_g, w_in, ssd_conv_w, ssd_conv_b, ssd_dt_bias, ssd_a_log, ssd_d, ssd_norm_g,
           gqa_q_norm_g, gqa_k_norm_g, gla_gate_w2, gla_gate_b, gla_norm_g, mla_q_norm_g, mla_w_uq,
           mla_kv_norm_g, mla_w_ukv, w_out, norm2_g, w_ff1, w_ff2, final_norm_g):
    stacked = (norm1_g, w_in, ssd_conv_w, ssd_conv_b, ssd_dt_bias, ssd_a_log, ssd_d, ssd_norm_g,
               gqa_q_norm_g, gqa_k_norm_g, gla_gate_w2, gla_gate_b, gla_norm_g, mla_q_norm_g, mla_w_uq,
               mla_kv_norm_g, mla_w_ukv, w_out, norm2_g, w_ff1, w_ff2)
    layers = [_layer_params(i, stacked) for i in range(norm1_g.shape[0])]
    return (_trunk(x_prompt, layers, final_norm_g), _trunk(x_sample, layers, final_norm_g))
```

```python
import functools

import numpy as np
import jax
import jax.numpy as jnp
from jax import lax
from jax.experimental import pallas as pl
from jax.experimental.pallas import tpu as pltpu

F32 = jnp.float32
BF16 = jnp.bfloat16

D_MODEL = 1024
DEPTH = 2
GRID_W = 64
CHUNK = 128
ROPE_THETA = 10000.0
EPS = 1e-6
GROUP_W = D_MODEL // 4
D_FF = 4 * D_MODEL

SSD_HEADS = 4
SSD_HEAD_DIM = GROUP_W // SSD_HEADS
SSD_GROUPS = 2
SSD_STATE = 128
SSD_CONV_W = 5
SSD_XBC = GROUP_W + 2 * SSD_GROUPS * SSD_STATE

GQA_HEADS = 4
GQA_KV_HEADS = 2
GQA_HEAD_DIM = GROUP_W // GQA_HEADS

GLA_HEADS = 4
GLA_DV = GROUP_W // GLA_HEADS
GLA_DK = GLA_DV // 2
GLA_LOWRANK = 16
GLA_TAU = 16.0

MLA_HEADS = 4
MLA_Q_LORA = 256
MLA_KV_LORA = 128
MLA_NOPE = 64
MLA_ROPE = 32
MLA_V = GROUP_W // MLA_HEADS
MLA_QK = MLA_NOPE + MLA_ROPE

IN_SIZES = (GROUP_W, SSD_XBC, 2 * SSD_HEADS,
            GQA_HEADS * GQA_HEAD_DIM, GQA_KV_HEADS * GQA_HEAD_DIM, GQA_KV_HEADS * GQA_HEAD_DIM,
            GLA_HEADS * GLA_DK, GLA_HEADS * GLA_DK, GLA_HEADS * GLA_DV, GLA_HEADS * GLA_DV, 2 * GLA_LOWRANK,
            MLA_Q_LORA, MLA_KV_LORA, MLA_ROPE)
IN_OFFS = tuple(int(v) for v in np.concatenate([[0], np.cumsum(IN_SIZES)]))

LOG2E = 1.4426950408889634
LANE = 128
HALO = 8
VMEM_LIMIT = 52 * 1024 * 1024

SSD_W = SSD_XBC + GROUP_W + 2 * GROUP_W
GQA_W = GQA_HEADS * LANE + GQA_KV_HEADS * LANE + GQA_KV_HEADS * GQA_HEAD_DIM
GLA_W = 2 * GLA_HEADS * GLA_DK + 2 * GLA_HEADS * GLA_DV + LANE
MLA_W = MLA_Q_LORA + MLA_KV_LORA + LANE
PROJ_W = SSD_W + GQA_W + GLA_W + MLA_W


def _in_proj_columns():
    o = IN_OFFS
    cols = []
    cols += list(range(o[1], o[2]))
    cols += list(range(o[0], o[1]))
    for d in range(2):
        for h in range(SSD_HEADS):
            cols += [o[2] + d * SSD_HEADS + h] * SSD_HEAD_DIM
    for h in range(GQA_HEADS):
        cols += list(range(o[3] + h * GQA_HEAD_DIM, o[3] + (h + 1) * GQA_HEAD_DIM)) + [-1] * (LANE - GQA_HEAD_DIM)
    for h in range(GQA_KV_HEADS):
        cols += list(range(o[4] + h * GQA_HEAD_DIM, o[4] + (h + 1) * GQA_HEAD_DIM)) + [-1] * (LANE - GQA_HEAD_DIM)
    cols += list(range(o[5], o[6]))
    cols += list(range(o[6], o[10]))
    cols += list(range(o[10], o[11])) + [-1] * (LANE - 2 * GLA_LOWRANK)
    cols += list(range(o[11], o[13]))
    cols += [-1] * MLA_NOPE + list(range(o[13], o[14])) + [-1] * (LANE - MLA_QK)
    cols = np.asarray(cols, np.int32)
    assert cols.shape[0] == PROJ_W
    return cols


_IN_COLS = _in_proj_columns()


def _gather_cols(w, cols):
    picked = jnp.take(w, jnp.asarray(np.maximum(cols, 0)), axis=1)
    return jnp.where(jnp.asarray(cols >= 0)[None, :], picked, 0.0)


def _params(**kw):
    return pltpu.CompilerParams(vmem_limit_bytes=VMEM_LIMIT, **kw)


def _silu(x):
    return x * (1.0 / (1.0 + jnp.exp(-x)))


def _softplus(x):
    return jnp.maximum(x, 0.0) + jnp.log1p(jnp.exp(-jnp.abs(x)))


def _dot(a, b):
    return jnp.dot(a, b, preferred_element_type=F32)


def _dot_nt(a, b):
    return lax.dot_general(a, b, (((1,), (1,)), ((), ())), preferred_element_type=F32)


def _dot_tn(a, b):
    return lax.dot_general(a, b, (((0,), (0,)), ((), ())), preferred_element_type=F32)


def _split_dot(m_bf16, x):
    x1 = x.astype(BF16)
    r1 = x - x1.astype(F32)
    x2 = r1.astype(BF16)
    x3 = (r1 - x2.astype(F32)).astype(BF16)
    return _dot(m_bf16, x1) + _dot(m_bf16, x2) + _dot(m_bf16, x3)


def _tri(reverse):
    li = lax.broadcasted_iota(jnp.int32, (CHUNK, CHUNK), 0)
    si = lax.broadcasted_iota(jnp.int32, (CHUNK, CHUNK), 1)
    return (si >= li) if reverse else (si <= li)


def _inproj_kernel(x_ref, g_ref, w_ref, ssd_ref, gqa_ref, gla_ref, mla_ref):
    x = x_ref[...]
    ms = jnp.mean(x * x, axis=-1, keepdims=True)
    h = (x * lax.rsqrt(ms + EPS) * g_ref[...]).astype(BF16)
    c0 = 0
    for ref, w in ((ssd_ref, SSD_W), (gqa_ref, GQA_W), (gla_ref, GLA_W), (mla_ref, MLA_W)):
        ref[...] = _dot(h, w_ref[:, c0:c0 + w])
        c0 += w


def _inproj(x2d, g, w_all, tm):
    n = x2d.shape[0]
    out_w = (SSD_W, GQA_W, GLA_W, MLA_W)
    return pl.pallas_call(
        _inproj_kernel,
        grid=(n // tm,),
        in_specs=[pl.BlockSpec((tm, D_MODEL), lambda i: (i, 0)),
                  pl.BlockSpec((1, D_MODEL), lambda i: (0, 0)),
                  pl.BlockSpec((D_MODEL, PROJ_W), lambda i: (0, 0), pipeline_mode=pl.Buffered(1))],
        out_specs=[pl.BlockSpec((tm, w), lambda i: (i, 0)) for w in out_w],
        out_shape=[jax.ShapeDtypeStruct((n, w), F32) for w in out_w],
        compiler_params=_params(dimension_semantics=("arbitrary",)),
        name="inproj",
    )(x2d, g.reshape(1, D_MODEL), w_all)


def _ssd_kernel(xbc_ref, prev_ref, next_ref, z_ref, dtx_ref, cw_ref, cb_ref, dtb_ref, a_ref, d_ref, g_ref,
                out_ref, yf_ref, xs_ref, bc_ref, h_ref, xe_ref, *, nt, tc):
    j = pl.program_id(1)
    tile = jnp.where(j < nt, j, 2 * nt - 1 - j)
    base = pl.multiple_of(tile * tc, tc)
    nc = tc // CHUNK
    half = GROUP_W // SSD_GROUPS

    @pl.when((j == 0) | (j == nt))
    def _():
        h_ref[...] = jnp.zeros_like(h_ref)

    @pl.when(j < nt)
    def _():
        xe_ref[HALO:HALO + tc, :] = xbc_ref[0]
        xe_ref[0:HALO, :] = jnp.where(tile > 0, prev_ref[0], 0.0)
        xe_ref[HALO + tc:2 * HALO + tc, :] = jnp.where(tile < nt - 1, next_ref[0], 0.0)
        pad = SSD_CONV_W // 2
        acc = cb_ref[...] + cw_ref[0:1, :] * xe_ref[HALO - pad:HALO - pad + tc, :]
        for w in range(1, SSD_CONV_W):
            acc = acc + cw_ref[w:w + 1, :] * xe_ref[HALO - pad + w:HALO - pad + w + tc, :]
        xbc = _silu(acc)
        xs_ref[pl.ds(base, tc), :] = xbc[:, :GROUP_W]
        bc_ref[pl.ds(base, tc), :] = xbc[:, GROUP_W:].astype(BF16)

    xs = xs_ref[pl.ds(base, tc), :]
    dt = _softplus(dtx_ref[0] + dtb_ref[0])
    a_dt = dt * a_ref[0]
    xdt = xs * dt
    lane_head = lax.broadcasted_iota(jnp.int32, (1, GROUP_W), 1) // SSD_HEAD_DIM

    def scan(reverse):
        tri = _tri(reverse)
        tri_bf = tri.astype(BF16)
        chunks = range(nc)
        groups = range(SSD_GROUPS)
        per_group = SSD_HEADS // SSD_GROUPS
        sls = [slice(c * CHUNK, (c + 1) * CHUNK) for c in chunks]
        rows = [pl.ds(base + c * CHUNK, CHUNK) for c in chunks]
        acs = [_split_dot(tri_bf, a_dt[sl]) for sl in sls]
        b_c = [[bc_ref[rows[c], g * SSD_STATE:(g + 1) * SSD_STATE] for g in groups] for c in chunks]
        c_c = [[bc_ref[rows[c], (SSD_GROUPS + g) * SSD_STATE:(SSD_GROUPS + g + 1) * SSD_STATE] for g in groups]
               for c in chunks]
        scores = [[_dot_nt(c_c[c][g], b_c[c][g]) for g in groups] for c in chunks]
        tot = [a[0:1] if reverse else a[CHUNK - 1:CHUNK] for a in acs]
        acs_t = [a.T for a in acs]
        x_d = [(xdt[sls[c]] * jnp.exp2(tot[c] - acs[c])).astype(BF16) for c in chunks]
        st = [jnp.concatenate([_dot_tn(b_c[c][g], x_d[c][:, g * half:(g + 1) * half]) for g in groups], axis=1)
              for c in chunks]
        weights = []
        for c in chunks:
            w_c = []
            for hd in range(SSD_HEADS):
                col = jnp.broadcast_to(acs[c][:, hd * SSD_HEAD_DIM:hd * SSD_HEAD_DIM + 1], (CHUNK, CHUNK))
                row = jnp.broadcast_to(acs_t[c][hd * SSD_HEAD_DIM:hd * SSD_HEAD_DIM + 1, :], (CHUNK, CHUNK))
                decay = jnp.exp2(jnp.where(tri, col - row, -jnp.inf))
                w_c.append((scores[c][hd // per_group] * decay).astype(BF16))
            weights.append(w_c)
        y = []
        for c in chunks:
            acc = None
            for hd in range(SSD_HEADS):
                part = _dot(weights[c][hd], jnp.where(lane_head == hd, xdt[sls[c]], 0.0).astype(BF16))
                acc = part if acc is None else acc + part
            y.append(acc)
        h = h_ref[...]
        h_in = [None] * nc
        for c in (reversed(chunks) if reverse else chunks):
            h_in[c] = h
            h = h * jnp.exp2(tot[c]) + st[c]
        h_ref[...] = h
        y_off = [jnp.concatenate([_dot(c_c[c][g], h_in[c][:, g * half:(g + 1) * half].astype(BF16)) for g in groups],
                                 axis=1) for c in chunks]
        return [y[c] + y_off[c] * jnp.exp2(acs[c]) for c in chunks]

    @pl.when(j < nt)
    def _():
        for c, y in enumerate(scan(False)):
            yf_ref[pl.ds(base + c * CHUNK, CHUNK), :] = y

    @pl.when(j >= nt)
    def _():
        for c, yb in enumerate(scan(True)):
            sl = slice(c * CHUNK, (c + 1) * CHUNK)
            y = yf_ref[pl.ds(base + c * CHUNK, CHUNK), :] + yb + d_ref[...] * xs[sl]
            gated = y * _silu(z_ref[0, sl, :])
            ms = jnp.mean(gated * gated, axis=-1, keepdims=True)
            out_ref[0, sl, :] = gated * lax.rsqrt(ms + EPS) * g_ref[...]


def _ssd(ssd_g, cw, cb, dtb, a_exp, d_exp, g, tc):
    b, t, _ = ssd_g.shape
    nt = t // tc
    hb = tc // HALO

    def tile_of(j):
        return jnp.where(j < nt, j, 2 * nt - 1 - j)

    def conv_tile(j):
        return jnp.minimum(j, nt - 1)

    kern = functools.partial(_ssd_kernel, nt=nt, tc=tc)
    return pl.pallas_call(
        kern,
        grid=(b, 2 * nt),
        in_specs=[
            pl.BlockSpec((1, tc, SSD_XBC), lambda i, j: (i, conv_tile(j), 0)),
            pl.BlockSpec((1, HALO, SSD_XBC), lambda i, j: (i, jnp.maximum(conv_tile(j) * hb - 1, 0), 0)),
            pl.BlockSpec((1, HALO, SSD_XBC), lambda i, j: (i, jnp.minimum((conv_tile(j) + 1) * hb, t // HALO - 1), 0)),
            pl.BlockSpec((1, tc, GROUP_W), lambda i, j: (i, tile_of(j), SSD_XBC // GROUP_W)),
            pl.BlockSpec((1, tc, GROUP_W), lambda i, j: (i, tile_of(j), SSD_XBC // GROUP_W + 1 + j // nt)),
            pl.BlockSpec((HALO, SSD_XBC), lambda i, j: (0, 0)),
            pl.BlockSpec((1, SSD_XBC), lambda i, j: (0, 0)),
            pl.BlockSpec((1, 1, GROUP_W), lambda i, j: (j // nt, 0, 0)),
            pl.BlockSpec((1, 1, GROUP_W), lambda i, j: (j // nt, 0, 0)),
            pl.BlockSpec((1, GROUP_W), lambda i, j: (0, 0)),
            pl.BlockSpec((1, GROUP_W), lambda i, j: (0, 0)),
        ],
        out_specs=pl.BlockSpec((1, tc, GROUP_W), lambda i, j: (i, jnp.where(j < nt, nt - 1, 2 * nt - 1 - j), 0)),
        out_shape=jax.ShapeDtypeStruct((b, t, GROUP_W), F32),
        scratch_shapes=[pltpu.VMEM((t, GROUP_W), F32),
                        pltpu.VMEM((t, GROUP_W), F32),
                        pltpu.VMEM((t, 2 * SSD_GROUPS * SSD_STATE), BF16),
                        pltpu.VMEM((SSD_STATE, GROUP_W), F32),
                        pltpu.VMEM((tc + 2 * HALO, SSD_XBC), F32)],
        compiler_params=_params(dimension_semantics=("arbitrary", "arbitrary")),
        name="ssd",
    )(ssd_g, ssd_g, ssd_g, ssd_g, ssd_g, cw, cb, dtb, a_exp, d_exp, g)


GLA_QK_W = GLA_HEADS * GLA_DK
GLA_V_W = GLA_HEADS * GLA_DV


def _gla_kernel(q_ref, k_ref, v_ref, r_ref, lr_ref, w2_ref, gb_ref, g_ref, avg_ref,
                out_ref, of_ref, s_ref, *, nt, tc):
    j = pl.program_id(1)
    tile = jnp.where(j < nt, j, 2 * nt - 1 - j)

    @pl.when((j == 0) | (j == nt))
    def _():
        s_ref[...] = jnp.zeros_like(s_ref)

    base = pl.multiple_of(tile * tc, tc)
    nc = tc // CHUNK
    logit = _dot(lr_ref[0].astype(BF16), w2_ref[0]) + gb_ref[0]
    logg = -_softplus(-logit) * (LOG2E / GLA_TAU)
    q = q_ref[0] * (GLA_DK ** -0.5)
    k = k_ref[0]
    v = v_ref[0]
    qk_head = lax.broadcasted_iota(jnp.int32, (1, GLA_QK_W), 1) // GLA_DK
    v_head = lax.broadcasted_iota(jnp.int32, (1, GLA_V_W), 1) // GLA_DV
    blockdiag = (lax.broadcasted_iota(jnp.int32, (GLA_QK_W, GLA_V_W), 0) // GLA_DK
                 == lax.broadcasted_iota(jnp.int32, (GLA_QK_W, GLA_V_W), 1) // GLA_DV)

    def scan(reverse):
        tri = _tri(reverse)
        tri_bf = tri.astype(BF16)
        last = 0 if reverse else CHUNK - 1
        chunks = range(nc)
        sls = [slice(c * CHUNK, (c + 1) * CHUNK) for c in chunks]
        gcs = [_split_dot(tri_bf, logg[sl]) for sl in sls]
        tot = [g[last:last + 1] for g in gcs]
        dec_col = [jnp.exp2(g.T[:, last:last + 1]) for g in gcs]
        qg = [q[sl] * jnp.exp2(g) for sl, g in zip(sls, gcs)]
        kg = [(k[sl] * jnp.exp2(-g)).astype(BF16) for sl, g in zip(sls, gcs)]
        kd = [(k[sl] * jnp.exp2(t - g)).astype(BF16) for sl, g, t in zip(sls, gcs, tot)]
        raw = [[_dot_nt(jnp.where(qk_head == hd, qg[c], 0.0).astype(BF16), kg[c]) for hd in range(GLA_HEADS)]
               for c in chunks]
        st = [jnp.where(blockdiag, _dot_tn(kd[c], v[sls[c]].astype(BF16)), 0.0) for c in chunks]
        att = [[jnp.where(tri, raw[c][hd], 0.0).astype(BF16) for hd in range(GLA_HEADS)] for c in chunks]
        o = []
        for c in chunks:
            acc = None
            for hd in range(GLA_HEADS):
                part = _dot(att[c][hd], jnp.where(v_head == hd, v[sls[c]], 0.0).astype(BF16))
                acc = part if acc is None else acc + part
            o.append(acc)
        s = s_ref[...]
        s_in = [None] * nc
        for c in (reversed(chunks) if reverse else chunks):
            s_in[c] = s
            s = s * dec_col[c] + st[c]
        s_ref[...] = s
        return [o[c] + _dot(qg[c].astype(BF16), s_in[c].astype(BF16)) for c in chunks]

    @pl.when(j < nt)
    def _():
        for c, o in enumerate(scan(False)):
            of_ref[pl.ds(base + c * CHUNK, CHUNK), :] = o

    @pl.when(j >= nt)
    def _():
        for c, ob in enumerate(scan(True)):
            sl = slice(c * CHUNK, (c + 1) * CHUNK)
            o = of_ref[pl.ds(base + c * CHUNK, CHUNK), :] + ob
            ms = _split_dot_right(o * o, avg_ref[...])
            out_ref[0, sl, :] = o * lax.rsqrt(ms + EPS) * g_ref[...] * _silu(r_ref[0, sl, :])


def _split_dot_right(x, m_bf16):
    x1 = x.astype(BF16)
    r1 = x - x1.astype(F32)
    x2 = r1.astype(BF16)
    x3 = (r1 - x2.astype(F32)).astype(BF16)
    return _dot(x1, m_bf16) + _dot(x2, m_bf16) + _dot(x3, m_bf16)


def _gla(gla_g, w2p, gb, g, avg, tc):
    b, t, _ = gla_g.shape
    nt = t // tc

    def tile_of(j):
        return jnp.where(j < nt, j, 2 * nt - 1 - j)

    kern = functools.partial(_gla_kernel, nt=nt, tc=tc)
    return pl.pallas_call(
        kern,
        grid=(b, 2 * nt),
        in_specs=[
            pl.BlockSpec((1, tc, GLA_QK_W), lambda i, j: (i, tile_of(j), 0)),
            pl.BlockSpec((1, tc, GLA_QK_W), lambda i, j: (i, tile_of(j), 1)),
            pl.BlockSpec((1, tc, GLA_V_W), lambda i, j: (i, tile_of(j), 1)),
            pl.BlockSpec((1, tc, GLA_V_W), lambda i, j: (i, tile_of(j), 2)),
            pl.BlockSpec((1, tc, LANE), lambda i, j: (i, tile_of(j), (2 * GLA_QK_W + 2 * GLA_V_W) // LANE)),
            pl.BlockSpec((1, LANE, GLA_QK_W), lambda i, j: (j // nt, 0, 0)),
            pl.BlockSpec((1, 1, GLA_QK_W), lambda i, j: (j // nt, 0, 0)),
            pl.BlockSpec((1, GLA_V_W), lambda i, j: (0, 0)),
            pl.BlockSpec((GLA_V_W, GLA_V_W), lambda i, j: (0, 0)),
        ],
        out_specs=pl.BlockSpec((1, tc, GLA_V_W), lambda i, j: (i, jnp.where(j < nt, nt - 1, 2 * nt - 1 - j), 0)),
        out_shape=jax.ShapeDtypeStruct((b, t, GLA_V_W), F32),
        scratch_shapes=[pltpu.VMEM((t, GLA_V_W), F32),
                        pltpu.VMEM((GLA_QK_W, GLA_V_W), F32)],
        compiler_params=_params(dimension_semantics=("arbitrary", "arbitrary")),
        name="gla",
    )(gla_g, gla_g, gla_g, gla_g, gla_g, w2p, gb, g, avg)


def _rope(y, cos, sin, half):
    lane = lax.broadcasted_iota(jnp.int32, (1, LANE), 1)
    lo = (lane % (2 * half)) < half
    rot = jnp.where(lo, pltpu.roll(y, LANE - half, 1), pltpu.roll(y, half, 1))
    return y * cos + rot * sin


ONES_LANE = LANE - 1
VT_ROWS = 80
SCORE_BOUND_LIMIT = 48.0


def _finish_keys(k):
    kf = k.astype(BF16).astype(F32)
    norm = jnp.sqrt(jnp.max(jnp.sum(kf * kf, axis=-1, keepdims=True), axis=0, keepdims=True))
    lane = lax.broadcasted_iota(jnp.int32, (1, LANE), 1)
    return jnp.where(lane == ONES_LANE, 1.0, k).astype(BF16), norm


def _update_kmax(kmax_ref, norms):
    @pl.when(pl.program_id(1) == 0)
    def _():
        kmax_ref[...] = jnp.zeros_like(kmax_ref)

    lane = lax.broadcasted_iota(jnp.int32, (1, LANE), 1)
    upd = jnp.zeros((1, LANE), F32)
    for g, norm in enumerate(norms):
        upd = jnp.where(lane == g, norm, upd)
    kmax_ref[0] = jnp.maximum(kmax_ref[0], upd)


def _vt_rows(v, heads, dv):
    tm = v.shape[0]
    vt = v.T.reshape(heads, dv, tm)
    row = lax.broadcasted_iota(jnp.int32, (heads, VT_ROWS - dv, tm), 1)
    return jnp.concatenate([vt, jnp.where(row == 0, 1.0, 0.0)], axis=1).astype(BF16)


def _gqa_prep_kernel(q_ref, k_ref, v_ref, cos_ref, sin_ref, gq_ref, gk_ref, qo_ref, ko_ref, vt_ref, kmax_ref):
    cos = cos_ref[...]
    sin = sin_ref[...]

    def norm_rope(x, g, scale):
        ms = jnp.sum(x * x, axis=-1, keepdims=True) * (1.0 / GQA_HEAD_DIM)
        return _rope(x * lax.rsqrt(ms + EPS) * g, cos, sin, GQA_HEAD_DIM // 4) * scale

    for h in range(GQA_HEADS):
        sl = slice(h * LANE, (h + 1) * LANE)
        qo_ref[0, :, sl] = norm_rope(q_ref[0, :, sl], gq_ref[...], GQA_HEAD_DIM ** -0.5 * LOG2E).astype(BF16)
    norms = []
    for h in range(GQA_KV_HEADS):
        sl = slice(h * LANE, (h + 1) * LANE)
        ko_ref[0, :, sl], norm = _finish_keys(norm_rope(k_ref[0, :, sl], gk_ref[...], 1.0))
        norms.append(norm)
    _update_kmax(kmax_ref, norms)
    vt_ref[0, :, 0] = _vt_rows(v_ref[0], GQA_KV_HEADS, GQA_HEAD_DIM)


def _gqa_prep(gqa_g, cos, sin, gq, gk, tm):
    b, t, _ = gqa_g.shape
    qw, kw, vw = GQA_HEADS * LANE, GQA_KV_HEADS * LANE, GQA_KV_HEADS * GQA_HEAD_DIM
    return pl.pallas_call(
        _gqa_prep_kernel,
        grid=(b, t // tm),
        in_specs=[
            pl.BlockSpec((1, tm, qw), lambda i, j: (i, j, 0)),
            pl.BlockSpec((1, tm, kw), lambda i, j: (i, j, qw // kw)),
            pl.BlockSpec((1, tm, vw), lambda i, j: (i, j, (qw + kw) // vw)),
            pl.BlockSpec((tm, LANE), lambda i, j: (j, 0)),
            pl.BlockSpec((tm, LANE), lambda i, j: (j, 0)),
            pl.BlockSpec((1, LANE), lambda i, j: (0, 0)),
            pl.BlockSpec((1, LANE), lambda i, j: (0, 0)),
        ],
        out_specs=[
            pl.BlockSpec((1, tm, qw), lambda i, j: (i, j, 0)),
            pl.BlockSpec((1, tm, kw), lambda i, j: (i, j, 0)),
            pl.BlockSpec((1, GQA_KV_HEADS, 1, VT_ROWS, tm), lambda i, j: (i, 0, j, 0, 0)),
            pl.BlockSpec((1, 8, LANE), lambda i, j: (i, 0, 0)),
        ],
        out_shape=[
            jax.ShapeDtypeStruct((b, t, qw), BF16),
            jax.ShapeDtypeStruct((b, t, kw), BF16),
            jax.ShapeDtypeStruct((b, GQA_KV_HEADS, t // tm, VT_ROWS, tm), BF16),
            jax.ShapeDtypeStruct((b, 8, LANE), F32),
        ],
        compiler_params=_params(dimension_semantics=("arbitrary", "arbitrary")),
        name="gqa_prep",
    )(gqa_g, gqa_g, gqa_g, cos, sin, gq, gk)


def _mla_prep_kernel(cq_ref, ckv_ref, kr_ref, cos_ref, sin_ref, gq_ref, gkv_ref, wuq_ref, wuk_ref, wuv_ref,
                     qo_ref, ko_ref, vt_ref, kmax_ref):
    cos = cos_ref[...]
    sin = sin_ref[...]

    def rms(x, g):
        ms = jnp.mean(x * x, axis=-1, keepdims=True)
        return (x * lax.rsqrt(ms + EPS) * g).astype(BF16)

    q = _dot(rms(cq_ref[0], gq_ref[...]), wuq_ref[...])
    ckv = rms(ckv_ref[0], gkv_ref[...])
    kn = _dot(ckv, wuk_ref[...])
    v = _dot(ckv, wuv_ref[...])
    k_rope = _rope(kr_ref[0], cos, sin, MLA_ROPE // 4)
    norms = []
    for h in range(MLA_HEADS):
        sl = slice(h * LANE, (h + 1) * LANE)
        qo_ref[0, :, sl] = (_rope(q[:, sl], cos, sin, MLA_ROPE // 4) * (MLA_QK ** -0.5 * LOG2E)).astype(BF16)
        ko_ref[0, :, sl], norm = _finish_keys(kn[:, sl] + k_rope)
        norms.append(norm)
    _update_kmax(kmax_ref, norms)
    vt_ref[0, :, 0] = _vt_rows(v, MLA_HEADS, MLA_V)


def _mla_prep(mla_g, cos, sin, gq, gkv, wuq, wuk, wuv, tm):
    b, t, _ = mla_g.shape
    hw = MLA_HEADS * LANE
    return pl.pallas_call(
        _mla_prep_kernel,
        grid=(b, t // tm),
        in_specs=[
            pl.BlockSpec((1, tm, MLA_Q_LORA), lambda i, j: (i, j, 0)),
            pl.BlockSpec((1, tm, MLA_KV_LORA), lambda i, j: (i, j, MLA_Q_LORA // MLA_KV_LORA)),
            pl.BlockSpec((1, tm, LANE), lambda i, j: (i, j, (MLA_Q_LORA + MLA_KV_LORA) // LANE)),
            pl.BlockSpec((tm, LANE), lambda i, j: (j, 0)),
            pl.BlockSpec((tm, LANE), lambda i, j: (j, 0)),
            pl.BlockSpec((1, MLA_Q_LORA), lambda i, j: (0, 0)),
            pl.BlockSpec((1, MLA_KV_LORA), lambda i, j: (0, 0)),
            pl.BlockSpec((MLA_Q_LORA, hw), lambda i, j: (0, 0)),
            pl.BlockSpec((MLA_KV_LORA, hw), lambda i, j: (0, 0)),
            pl.BlockSpec((MLA_KV_LORA, MLA_HEADS * MLA_V), lambda i, j: (0, 0)),
        ],
        out_specs=[
            pl.BlockSpec((1, tm, hw), lambda i, j: (i, j, 0)),
            pl.BlockSpec((1, tm, hw), lambda i, j: (i, j, 0)),
            pl.BlockSpec((1, MLA_HEADS, 1, VT_ROWS, tm), lambda i, j: (i, 0, j, 0, 0)),
            pl.BlockSpec((1, 8, LANE), lambda i, j: (i, 0, 0)),
        ],
        out_shape=[
            jax.ShapeDtypeStruct((b, t, hw), BF16),
            jax.ShapeDtypeStruct((b, t, hw), BF16),
            jax.ShapeDtypeStruct((b, MLA_HEADS, t // tm, VT_ROWS, tm), BF16),
            jax.ShapeDtypeStruct((b, 8, LANE), F32),
        ],
        compiler_params=_params(dimension_semantics=("arbitrary", "arbitrary")),
        name="mla_prep",
    )(mla_g, mla_g, mla_g, cos, sin, gq, gkv, wuq, wuk, wuv)


def _attn_kernel(q_ref, k_ref, vt_ref, kmax_ref, o_ref, qa_ref, m_ref, acc_ref, *, n_q, rep, tk, dv):
    nk = k_ref.shape[1] // tk
    lane = lax.broadcasted_iota(jnp.int32, (1, LANE), 1)
    kmax = kmax_ref[0, 0:1, :]
    bound_max = None
    for h in range(n_q):
        qf = q_ref[0, :, h * LANE:(h + 1) * LANE].astype(F32)
        kg = jnp.max(jnp.where(lane == h // rep, kmax, 0.0), axis=-1, keepdims=True)
        bound = jnp.sqrt(jnp.sum(qf * qf, axis=-1, keepdims=True)) * kg
        qa_ref[h] = jnp.where(lane == ONES_LANE, -bound, qf).astype(BF16)
        top = jnp.max(bound)
        bound_max = top if bound_max is None else jnp.maximum(bound_max, top)
    acc_ref[...] = jnp.zeros(acc_ref.shape, F32)

    @pl.when(bound_max <= SCORE_BOUND_LIMIT)
    def _():
        def scores(i, h):
            ks = pl.multiple_of(i * tk, tk)
            g = h // rep
            return _dot_nt(k_ref[0, pl.ds(ks, tk), g * LANE:(g + 1) * LANE], qa_ref[h])

        group = max(u for u in (1, 2, 4) if nk % u == 0)

        def body(ii, carry):
            steps = [(ii * group + u, h) for u in range(group) for h in range(n_q)]
            s_next = scores(*steps[0])
            for n, (i, h) in enumerate(steps):
                s = s_next
                if n + 1 < len(steps):
                    s_next = scores(*steps[n + 1])
                acc_ref[h] += _dot(vt_ref[0, h // rep, i], jnp.exp2(s).astype(BF16))
            return carry

        lax.fori_loop(0, nk // group, body, 0)

    @pl.when(bound_max > SCORE_BOUND_LIMIT)
    def _():
        m_ref[...] = jnp.full(m_ref.shape, -jnp.inf, F32)

        def body(i, carry):
            ks = pl.multiple_of(i * tk, tk)
            for h in range(n_q):
                g = h // rep
                s = _dot_nt(k_ref[0, pl.ds(ks, tk), g * LANE:(g + 1) * LANE], q_ref[0, :, h * LANE:(h + 1) * LANE])
                m_old = m_ref[h]
                m_new = jnp.maximum(m_old, jnp.max(s, axis=0, keepdims=True))
                p = jnp.exp2(s - m_new)
                acc_ref[h] = acc_ref[h] * jnp.exp2(m_old - m_new) + _dot(vt_ref[0, g, i], p.astype(BF16))
                m_ref[h] = m_new
            return carry

        lax.fori_loop(0, nk, body, 0)

    o = jnp.concatenate([acc_ref[h, 0:dv, :] * (1.0 / acc_ref[h, dv:dv + 1, :]) for h in range(n_q)], axis=0)
    o_ref[0] = o.T


def _attention(q, k, vt, kmax, rep, tq, tk, dv):
    b, t, qw = q.shape
    kw = k.shape[2]
    n_q = qw // LANE
    n_kv = n_q // rep
    ow = n_q * dv
    kern = functools.partial(_attn_kernel, n_q=n_q, rep=rep, tk=tk, dv=dv)
    return pl.pallas_call(
        kern,
        grid=(b, t // tq),
        in_specs=[
            pl.BlockSpec((1, tq, qw), lambda i, j: (i, j, 0)),
            pl.BlockSpec((1, t, kw), lambda i, j: (i, 0, 0)),
            pl.BlockSpec((1, n_kv, t // tk, VT_ROWS, tk), lambda i, j: (i, 0, 0, 0, 0)),
            pl.BlockSpec((1, 8, LANE), lambda i, j: (i, 0, 0)),
        ],
        out_specs=pl.BlockSpec((1, tq, ow), lambda i, j: (i, j, 0)),
        out_shape=jax.ShapeDtypeStruct((b, t, ow), F32),
        scratch_shapes=[pltpu.VMEM((n_q, tq, LANE), BF16), pltpu.VMEM((n_q, 1, tq), F32),
                        pltpu.VMEM((n_q, VT_ROWS, tq), F32)],
        compiler_params=_params(dimension_semantics=("arbitrary", "arbitrary")),
        name="attention",
    )(q, k, vt, kmax)


FF_CHUNK = 1024


def _outmlp_kernel(x_ref, m0_ref, m1_ref, m2_ref, m3_ref, wo_ref, g2_ref, w1_ref, w2_ref, gf_ref, o_ref, *, final):
    x1 = x_ref[...]
    for i, m_ref in enumerate((m0_ref, m1_ref, m2_ref, m3_ref)):
        x1 = x1 + _dot(m_ref[...].astype(BF16), wo_ref[i * GROUP_W:(i + 1) * GROUP_W, :])
    ms = jnp.mean(x1 * x1, axis=-1, keepdims=True)
    h = (x1 * lax.rsqrt(ms + EPS) * g2_ref[...]).astype(BF16)
    acc = None
    for c in range(D_FF // FF_CHUNK):
        sl = slice(c * FF_CHUNK, (c + 1) * FF_CHUNK)
        u = jnp.maximum(_dot(h, w1_ref[:, sl]), 0.0)
        part = _dot((u * u).astype(BF16), w2_ref[sl, :])
        acc = part if acc is None else acc + part
    y = x1 + acc
    if final:
        ms = jnp.mean(y * y, axis=-1, keepdims=True)
        y = y * lax.rsqrt(ms + EPS) * gf_ref[...]
    o_ref[...] = y


def _outmlp(x2d, mixes, wo, g2, w1, w2, gf, final, tm):
    n = x2d.shape[0]
    const = lambda i: (0, 0)
    single = pl.Buffered(1)
    kern = functools.partial(_outmlp_kernel, final=final)
    return pl.pallas_call(
        kern,
        grid=(n // tm,),
        in_specs=[pl.BlockSpec((tm, D_MODEL), lambda i: (i, 0))]
                 + [pl.BlockSpec((tm, GROUP_W), lambda i: (i, 0)) for _ in range(4)]
                 + [pl.BlockSpec((D_MODEL, D_MODEL), const, pipeline_mode=single),
                    pl.BlockSpec((1, D_MODEL), const),
                    pl.BlockSpec((D_MODEL, D_FF), const, pipeline_mode=single),
                    pl.BlockSpec((D_FF, D_MODEL), const, pipeline_mode=single),
                    pl.BlockSpec((1, D_MODEL), const)],
        out_specs=pl.BlockSpec((tm, D_MODEL), lambda i: (i, 0)),
        out_shape=jax.ShapeDtypeStruct((n, D_MODEL), F32),
        compiler_params=_params(dimension_semantics=("arbitrary",)),
        name="outmlp",
    )(x2d, *mixes, wo, g2.reshape(1, D_MODEL), w1, w2, gf.reshape(1, D_MODEL))


def _rope_tables(t):
    pos = jnp.arange(t, dtype=jnp.int32)
    row = (pos // GRID_W).astype(F32)
    col = (pos % GRID_W).astype(F32)

    def block(p, n):
        inv_freq = ROPE_THETA ** (-jnp.arange(n, dtype=F32) / n)
        ang = p[:, None] * inv_freq[None, :]
        c, s = jnp.cos(ang), jnp.sin(ang)
        return jnp.concatenate([c, c], axis=1), jnp.concatenate([-s, s], axis=1)

    def table(n, lead):
        cr, sr = block(row, n)
        cc, sc = block(col, n)
        tail = LANE - lead - 4 * n
        cos = jnp.concatenate([jnp.ones((t, lead), F32), cr, cc, jnp.ones((t, tail), F32)], axis=1)
        sin = jnp.concatenate([jnp.zeros((t, lead), F32), sr, sc, jnp.zeros((t, tail), F32)], axis=1)
        return cos, sin

    return table(GQA_HEAD_DIM // 4, 0), table(MLA_ROPE // 4, MLA_NOPE)


def _pad_lanes(v, width=LANE):
    return jnp.concatenate([v, jnp.zeros((width - v.shape[0],), v.dtype)]).reshape(1, width)


def _layer_params(i, p):
    (norm1_g, w_in, ssd_conv_w, ssd_conv_b, ssd_dt_bias, ssd_a_log, ssd_d, ssd_norm_g,
     gqa_q_norm_g, gqa_k_norm_g, gla_gate_w2, gla_gate_b, gla_norm_g, mla_q_norm_g, mla_w_uq,
     mla_kv_norm_g, mla_w_ukv, w_out, norm2_g, w_ff1, w_ff2) = [a[i] for a in p]
    out = {}
    out["norm1_g"] = norm1_g
    out["w_all"] = _gather_cols(w_in, _IN_COLS).astype(BF16)
    out["conv_w"] = jnp.concatenate([ssd_conv_w, jnp.zeros((HALO - SSD_CONV_W, SSD_XBC), F32)], axis=0)
    out["conv_b"] = ssd_conv_b.reshape(1, SSD_XBC)
    expand = lambda a: jnp.repeat(a, SSD_HEAD_DIM, axis=-1)
    out["dt_bias"] = expand(ssd_dt_bias).reshape(2, 1, GROUP_W)
    out["a_neg"] = expand(-jnp.exp(ssd_a_log) * LOG2E).reshape(2, 1, GROUP_W)
    out["ssd_d"] = expand(ssd_d).reshape(1, GROUP_W)
    out["ssd_norm_g"] = ssd_norm_g.reshape(1, GROUP_W)
    out["gq"] = _pad_lanes(gqa_q_norm_g)
    out["gk"] = _pad_lanes(gqa_k_norm_g)
    w2p = jnp.zeros((2, LANE, GLA_QK_W), F32)
    for d in range(2):
        w2p = w2p.at[d, d * GLA_LOWRANK:(d + 1) * GLA_LOWRANK, :].set(gla_gate_w2[d])
    out["gla_w2"] = w2p.astype(BF16)
    out["gla_b"] = gla_gate_b.reshape(2, 1, GLA_QK_W)
    out["gla_norm_g"] = jnp.tile(gla_norm_g, GLA_HEADS).reshape(1, GLA_V_W)
    out["mla_gq"] = mla_q_norm_g.reshape(1, MLA_Q_LORA)
    out["mla_gkv"] = mla_kv_norm_g.reshape(1, MLA_KV_LORA)
    uq_cols, uk_cols, uv_cols = [], [], []
    for h in range(MLA_HEADS):
        uq_cols += list(range(h * MLA_QK, (h + 1) * MLA_QK)) + [-1] * (LANE - MLA_QK)
        base = h * (MLA_NOPE + MLA_V)
        uk_cols += list(range(base, base + MLA_NOPE)) + [-1] * (LANE - MLA_NOPE)
        uv_cols += list(range(base + MLA_NOPE, base + MLA_NOPE + MLA_V))
    out["wuq"] = _gather_cols(mla_w_uq, np.asarray(uq_cols, np.int32)).astype(BF16)
    out["wuk"] = _gather_cols(mla_w_ukv, np.asarray(uk_cols, np.int32)).astype(BF16)
    out["wuv"] = _gather_cols(mla_w_ukv, np.asarray(uv_cols, np.int32)).astype(BF16)
    out["w_out"] = w_out.astype(BF16)
    out["norm2_g"] = norm2_g
    out["w_ff1"] = w_ff1.astype(BF16)
    out["w_ff2"] = w_ff2.astype(BF16)
    return out


TILE_PREF = dict(tm=256, tc=512, tp=512, tq_gqa=512, tq_mla=512, tmlp=512)


def _tiles(t):
    pick = lambda pref: max(c for c in (128, 256, 512, 1024) if c <= pref and t % c == 0)
    return {name: pick(pref) for name, pref in TILE_PREF.items()}


def _trunk(x, layers, final_norm_g):
    b, t, d = x.shape
    n = b * t
    ts = _tiles(t)
    tk = ts["tp"]
    (cos_g, sin_g), (cos_m, sin_m) = _rope_tables(t)
    avg = jnp.asarray(np.kron(np.eye(GLA_HEADS), np.full((GLA_DV, GLA_DV), 1.0 / GLA_DV)), BF16)
    x2d = x.reshape(n, d)
    for i, lp in enumerate(layers):
        ssd_g, gqa_g, gla_g, mla_g = _inproj(x2d, lp["norm1_g"], lp["w_all"], ts["tm"])
        ssd_out = _ssd(ssd_g.reshape(b, t, SSD_W), lp["conv_w"], lp["conv_b"], lp["dt_bias"], lp["a_neg"],
                       lp["ssd_d"], lp["ssd_norm_g"], ts["tc"])
        gq, gk, gvt, gkmax = _gqa_prep(gqa_g.reshape(b, t, GQA_W), cos_g, sin_g, lp["gq"], lp["gk"], tk)
        gqa_out = _attention(gq, gk, gvt, gkmax, GQA_HEADS // GQA_KV_HEADS, ts["tq_gqa"], tk, GQA_HEAD_DIM)
        gla_out = _gla(gla_g.reshape(b, t, GLA_W), lp["gla_w2"], lp["gla_b"], lp["gla_norm_g"], avg, ts["tc"])
        mq, mk, mvt, mkmax = _mla_prep(mla_g.reshape(b, t, MLA_W), cos_m, sin_m, lp["mla_gq"], lp["mla_gkv"],
                                       lp["wuq"], lp["wuk"], lp["wuv"], tk)
        mla_out = _attention(mq, mk, mvt, mkmax, 1, ts["tq_mla"], tk, MLA_V)
        mixes = [m.reshape(n, GROUP_W) for m in (ssd_out, gqa_out, gla_out, mla_out)]
        x2d = _outmlp(x2d, mixes, lp["w_out"], lp["norm2_g"], lp["w_ff1"], lp["w_ff2"], final_norm_g,
                      i == len(layers) - 1, ts["tmlp"])
    return x2d.reshape(b, t, d)


def kernel(x_prompt, x_sample, norm1_g, w_in, ssd_conv_w, ssd_conv_b, ssd_dt_bias, ssd_a_log, ssd_d, ssd_norm_g,
           gqa_q_norm_g, gqa_k_norm_g, gla_gate_w2, gla_gate_b, gla_norm_g, mla_q_norm_g, mla_w_uq,
           mla_kv_norm_g, mla_w_ukv, w_out, norm2_g, w_ff1, w_ff2, final_norm_g):
    stacked = (norm1_g, w_in, ssd_conv_w, ssd_conv_b, ssd_dt_bias, ssd_a_log, ssd_d, ssd_norm_g,
               gqa_q_norm_g, gqa_k_norm_g, gla_gate_w2, gla_gate_b, gla_norm_g, mla_q_norm_g, mla_w_uq,
               mla_kv_norm_g, mla_w_ukv, w_out, norm2_g, w_ff1, w_ff2)
    layers = [_layer_params(i, stacked) for i in range(norm1_g.shape[0])]
    return (_trunk(x_prompt, layers, final_norm_g), _trunk(x_sample, layers, final_norm_g))
```

```python
import functools

import numpy as np
import jax
import jax.numpy as jnp
from jax import lax
from jax.experimental import pallas as pl
from jax.experimental.pallas import tpu as pltpu

F32 = jnp.float32
BF16 = jnp.bfloat16

D_MODEL = 1024
DEPTH = 2
GRID_W = 64
CHUNK = 128
ROPE_THETA = 10000.0
EPS = 1e-6
GROUP_W = D_MODEL // 4
D_FF = 4 * D_MODEL

SSD_HEADS = 4
SSD_HEAD_DIM = GROUP_W // SSD_HEADS
SSD_GROUPS = 2
SSD_STATE = 128
SSD_CONV_W = 5
SSD_XBC = GROUP_W + 2 * SSD_GROUPS * SSD_STATE

GQA_HEADS = 4
GQA_KV_HEADS = 2
GQA_HEAD_DIM = GROUP_W // GQA_HEADS

GLA_HEADS = 4
GLA_DV = GROUP_W // GLA_HEADS
GLA_DK = GLA_DV // 2
GLA_LOWRANK = 16
GLA_TAU = 16.0

MLA_HEADS = 4
MLA_Q_LORA = 256
MLA_KV_LORA = 128
MLA_NOPE = 64
MLA_ROPE = 32
MLA_V = GROUP_W // MLA_HEADS
MLA_QK = MLA_NOPE + MLA_ROPE

IN_SIZES = (GROUP_W, SSD_XBC, 2 * SSD_HEADS,
            GQA_HEADS * GQA_HEAD_DIM, GQA_KV_HEADS * GQA_HEAD_DIM, GQA_KV_HEADS * GQA_HEAD_DIM,
            GLA_HEADS * GLA_DK, GLA_HEADS * GLA_DK, GLA_HEADS * GLA_DV, GLA_HEADS * GLA_DV, 2 * GLA_LOWRANK,
            MLA_Q_LORA, MLA_KV_LORA, MLA_ROPE)
IN_OFFS = tuple(int(v) for v in np.concatenate([[0], np.cumsum(IN_SIZES)]))

LOG2E = 1.4426950408889634
LANE = 128
HALO = 8
VMEM_LIMIT = 52 * 1024 * 1024

SSD_W = SSD_XBC + GROUP_W + 2 * GROUP_W
GQA_W = GQA_HEADS * LANE + GQA_KV_HEADS * LANE + GQA_KV_HEADS * GQA_HEAD_DIM
GLA_W = 2 * GLA_HEADS * GLA_DK + 2 * GLA_HEADS * GLA_DV + LANE
MLA_W = MLA_Q_LORA + MLA_KV_LORA + LANE
PROJ_W = SSD_W + GQA_W + GLA_W + MLA_W


def _in_proj_columns():
    o = IN_OFFS
    cols = []
    cols += list(range(o[1], o[2]))
    cols += list(range(o[0], o[1]))
    for d in range(2):
        for h in range(SSD_HEADS):
            cols += [o[2] + d * SSD_HEADS + h] * SSD_HEAD_DIM
    for h in range(GQA_HEADS):
        cols += list(range(o[3] + h * GQA_HEAD_DIM, o[3] + (h + 1) * GQA_HEAD_DIM)) + [-1] * (LANE - GQA_HEAD_DIM)
    for h in range(GQA_KV_HEADS):
        cols += list(range(o[4] + h * GQA_HEAD_DIM, o[4] + (h + 1) * GQA_HEAD_DIM)) + [-1] * (LANE - GQA_HEAD_DIM)
    cols += list(range(o[5], o[6]))
    cols += list(range(o[6], o[10]))
    cols += list(range(o[10], o[11])) + [-1] * (LANE - 2 * GLA_LOWRANK)
    cols += list(range(o[11], o[13]))
    cols += [-1] * MLA_NOPE + list(range(o[13], o[14])) + [-1] * (LANE - MLA_QK)
    cols = np.asarray(cols, np.int32)
    assert cols.shape[0] == PROJ_W
    return cols


_IN_COLS = _in_proj_columns()


def _gather_cols(w, cols):
    picked = jnp.take(w, jnp.asarray(np.maximum(cols, 0)), axis=1)
    return jnp.where(jnp.asarray(cols >= 0)[None, :], picked, 0.0)


def _params(**kw):
    return pltpu.CompilerParams(vmem_limit_bytes=VMEM_LIMIT, **kw)


def _silu(x):
    return x * (1.0 / (1.0 + jnp.exp(-x)))


def _softplus(x):
    return jnp.maximum(x, 0.0) + jnp.log1p(jnp.exp(-jnp.abs(x)))


def _dot(a, b):
    return jnp.dot(a, b, preferred_element_type=F32)


def _dot_nt(a, b):
    return lax.dot_general(a, b, (((1,), (1,)), ((), ())), preferred_element_type=F32)


def _dot_tn(a, b):
    return lax.dot_general(a, b, (((0,), (0,)), ((), ())), preferred_element_type=F32)


def _split_dot(m_bf16, x):
    x1 = x.astype(BF16)
    r1 = x - x1.astype(F32)
    x2 = r1.astype(BF16)
    x3 = (r1 - x2.astype(F32)).astype(BF16)
    return _dot(m_bf16, x1) + _dot(m_bf16, x2) + _dot(m_bf16, x3)


def _tri(reverse):
    li = lax.broadcasted_iota(jnp.int32, (CHUNK, CHUNK), 0)
    si = lax.broadcasted_iota(jnp.int32, (CHUNK, CHUNK), 1)
    return (si >= li) if reverse else (si <= li)


def _inproj_kernel(x_ref, g_ref, w_ref, cosg_ref, sing_ref, cosm_ref, sinm_ref, gq_ref, gk_ref,
                   mgq_ref, mgkv_ref, wuq_ref, wuk_ref, wuv_ref,
                   ssd_ref, gla_ref, gq_out, gk_out, gvt_out, gkmax_out, mq_out, mk_out, mvt_out, mkmax_out):
    x = x_ref[0]
    ms = jnp.mean(x * x, axis=-1, keepdims=True)
    h = (x * lax.rsqrt(ms + EPS) * g_ref[...]).astype(BF16)
    c_gqa, c_gla, c_mla = SSD_W, SSD_W + GQA_W, SSD_W + GQA_W + GLA_W
    mla = _dot(h, w_ref[:, c_mla:PROJ_W])
    gqa = _dot(h, w_ref[:, c_gqa:c_gla])

    def rms(y, g):
        ms_y = jnp.mean(y * y, axis=-1, keepdims=True)
        return (y * lax.rsqrt(ms_y + EPS) * g).astype(BF16)

    cq = rms(mla[:, :MLA_Q_LORA], mgq_ref[...])
    ckv = rms(mla[:, MLA_Q_LORA:MLA_Q_LORA + MLA_KV_LORA], mgkv_ref[...])
    ssd_ref[0] = _dot(h, w_ref[:, 0:SSD_W])
    q = _dot(cq, wuq_ref[...])
    kn = _dot(ckv, wuk_ref[...])
    v = _dot(ckv, wuv_ref[...])
    gla_ref[0] = _dot(h, w_ref[:, c_gla:c_mla])

    cos, sin = cosg_ref[...], sing_ref[...]
    qw, kw = GQA_HEADS * LANE, GQA_KV_HEADS * LANE
    heads = [gqa[:, hd * LANE:(hd + 1) * LANE] for hd in range(GQA_HEADS + GQA_KV_HEADS)]
    gains = [gq_ref[...]] * GQA_HEADS + [gk_ref[...]] * GQA_KV_HEADS
    inv = [lax.rsqrt(jnp.sum(y * y, axis=-1, keepdims=True) * (1.0 / GQA_HEAD_DIM) + EPS) for y in heads]
    roped = [_rope(y * r * g, cos, sin, GQA_HEAD_DIM // 4) for y, r, g in zip(heads, inv, gains)]
    for hd in range(GQA_HEADS):
        gq_out[0, :, hd * LANE:(hd + 1) * LANE] = (roped[hd] * (GQA_HEAD_DIM ** -0.5 * LOG2E)).astype(BF16)
    norms = []
    for hd in range(GQA_KV_HEADS):
        gk_out[0, :, hd * LANE:(hd + 1) * LANE], norm = _finish_keys(roped[GQA_HEADS + hd])
        norms.append(norm)
    _update_kmax(gkmax_out, norms)
    gvt_out[0, :, 0] = _vt_rows(gqa[:, qw + kw:], GQA_KV_HEADS, GQA_HEAD_DIM)

    cos, sin = cosm_ref[...], sinm_ref[...]
    k_rope = _rope(mla[:, MLA_Q_LORA + MLA_KV_LORA:], cos, sin, MLA_ROPE // 4)
    q_roped = [_rope(q[:, hd * LANE:(hd + 1) * LANE], cos, sin, MLA_ROPE // 4) for hd in range(MLA_HEADS)]
    norms = []
    for hd in range(MLA_HEADS):
        sl = slice(hd * LANE, (hd + 1) * LANE)
        mq_out[0, :, sl] = (q_roped[hd] * (MLA_QK ** -0.5 * LOG2E)).astype(BF16)
        mk_out[0, :, sl], norm = _finish_keys(kn[:, sl] + k_rope)
        norms.append(norm)
    _update_kmax(mkmax_out, norms)
    mvt_out[0, :, 0] = _vt_rows(v, MLA_HEADS, MLA_V)


def _inproj(x, lp, tables, tm):
    b, t, _ = x.shape
    (cos_g, sin_g), (cos_m, sin_m) = tables
    const = lambda i, j: (0, 0)
    tok = lambda i, j: (i, j, 0)
    tab = pl.BlockSpec((tm, LANE), lambda i, j: (j, 0))
    qw, kw, mw = GQA_HEADS * LANE, GQA_KV_HEADS * LANE, MLA_HEADS * LANE

    def vt_spec(heads):
        return pl.BlockSpec((1, heads, 1, VT_ROWS, tm), lambda i, j: (i, 0, j, 0, 0))

    kmax_spec = pl.BlockSpec((1, 8, LANE), lambda i, j: (i, 0, 0))
    return pl.pallas_call(
        _inproj_kernel,
        grid=(b, t // tm),
        in_specs=[pl.BlockSpec((1, tm, D_MODEL), tok),
                  pl.BlockSpec((1, D_MODEL), const),
                  pl.BlockSpec((D_MODEL, PROJ_W), const, pipeline_mode=pl.Buffered(1)),
                  tab, tab, tab, tab,
                  pl.BlockSpec((1, LANE), const), pl.BlockSpec((1, LANE), const),
                  pl.BlockSpec((1, MLA_Q_LORA), const), pl.BlockSpec((1, MLA_KV_LORA), const),
                  pl.BlockSpec((MLA_Q_LORA, mw), const), pl.BlockSpec((MLA_KV_LORA, mw), const),
                  pl.BlockSpec((MLA_KV_LORA, MLA_HEADS * MLA_V), const)],
        out_specs=[pl.BlockSpec((1, tm, SSD_W), tok), pl.BlockSpec((1, tm, GLA_W), tok),
                   pl.BlockSpec((1, tm, qw), tok), pl.BlockSpec((1, tm, kw), tok), vt_spec(GQA_KV_HEADS), kmax_spec,
                   pl.BlockSpec((1, tm, mw), tok), pl.BlockSpec((1, tm, mw), tok), vt_spec(MLA_HEADS), kmax_spec],
        out_shape=[jax.ShapeDtypeStruct((b, t, SSD_W), F32), jax.ShapeDtypeStruct((b, t, GLA_W), F32),
                   jax.ShapeDtypeStruct((b, t, qw), BF16), jax.ShapeDtypeStruct((b, t, kw), BF16),
                   jax.ShapeDtypeStruct((b, GQA_KV_HEADS, t // tm, VT_ROWS, tm), BF16),
                   jax.ShapeDtypeStruct((b, 8, LANE), F32),
                   jax.ShapeDtypeStruct((b, t, mw), BF16), jax.ShapeDtypeStruct((b, t, mw), BF16),
                   jax.ShapeDtypeStruct((b, MLA_HEADS, t // tm, VT_ROWS, tm), BF16),
                   jax.ShapeDtypeStruct((b, 8, LANE), F32)],
        compiler_params=_params(dimension_semantics=("arbitrary", "arbitrary")),
        name="inproj",
    )(x, lp["norm1_g"].reshape(1, D_MODEL), lp["w_all"], cos_g, sin_g, cos_m, sin_m, lp["gq"], lp["gk"],
      lp["mla_gq"], lp["mla_gkv"], lp["wuq"], lp["wuk"], lp["wuv"])


def _ssd_kernel(xbc_ref, prev_ref, next_ref, z_ref, dtx_ref, cw_ref, cb_ref, dtb_ref, a_ref, d_ref, g_ref,
                out_ref, yf_ref, xs_ref, bc_ref, h_ref, xe_ref, *, nt, tc):
    j = pl.program_id(1)
    tile = jnp.where(j < nt, j, 2 * nt - 1 - j)
    base = pl.multiple_of(tile * tc, tc)
    nc = tc // CHUNK
    half = GROUP_W // SSD_GROUPS

    @pl.when((j == 0) | (j == nt))
    def _():
        h_ref[...] = jnp.zeros_like(h_ref)

    @pl.when(j < nt)
    def _():
        xe_ref[HALO:HALO + tc, :] = xbc_ref[0]
        xe_ref[0:HALO, :] = jnp.where(tile > 0, prev_ref[0], 0.0)
        xe_ref[HALO + tc:2 * HALO + tc, :] = jnp.where(tile < nt - 1, next_ref[0], 0.0)
        pad = SSD_CONV_W // 2
        acc = cb_ref[...] + cw_ref[0:1, :] * xe_ref[HALO - pad:HALO - pad + tc, :]
        for w in range(1, SSD_CONV_W):
            acc = acc + cw_ref[w:w + 1, :] * xe_ref[HALO - pad + w:HALO - pad + w + tc, :]
        xbc = _silu(acc)
        xs_ref[pl.ds(base, tc), :] = xbc[:, :GROUP_W]
        bc_ref[pl.ds(base, tc), :] = xbc[:, GROUP_W:].astype(BF16)

    xs = xs_ref[pl.ds(base, tc), :]
    dt = _softplus(dtx_ref[0] + dtb_ref[0])
    a_dt = dt * a_ref[0]
    xdt = xs * dt
    lane_head = lax.broadcasted_iota(jnp.int32, (1, GROUP_W), 1) // SSD_HEAD_DIM

    def scan(reverse):
        tri = _tri(reverse)
        tri_bf = tri.astype(BF16)
        chunks = range(nc)
        groups = range(SSD_GROUPS)
        per_group = SSD_HEADS // SSD_GROUPS
        sls = [slice(c * CHUNK, (c + 1) * CHUNK) for c in chunks]
        rows = [pl.ds(base + c * CHUNK, CHUNK) for c in chunks]
        acs = [_split_dot(tri_bf, a_dt[sl]) for sl in sls]
        b_c = [[bc_ref[rows[c], g * SSD_STATE:(g + 1) * SSD_STATE] for g in groups] for c in chunks]
        c_c = [[bc_ref[rows[c], (SSD_GROUPS + g) * SSD_STATE:(SSD_GROUPS + g + 1) * SSD_STATE] for g in groups]
               for c in chunks]
        scores = [[_dot_nt(c_c[c][g], b_c[c][g]) for g in groups] for c in chunks]
        tot = [a[0:1] if reverse else a[CHUNK - 1:CHUNK] for a in acs]
        acs_t = [a.T for a in acs]
        x_d = [(xdt[sls[c]] * jnp.exp2(tot[c] - acs[c])).astype(BF16) for c in chunks]
        st = [jnp.concatenate([_dot_tn(b_c[c][g], x_d[c][:, g * half:(g + 1) * half]) for g in groups], axis=1)
              for c in chunks]
        weights = []
        for c in chunks:
            w_c = []
            for hd in range(SSD_HEADS):
                col = jnp.broadcast_to(acs[c][:, hd * SSD_HEAD_DIM:hd * SSD_HEAD_DIM + 1], (CHUNK, CHUNK))
                row = jnp.broadcast_to(acs_t[c][hd * SSD_HEAD_DIM:hd * SSD_HEAD_DIM + 1, :], (CHUNK, CHUNK))
                decay = jnp.exp2(jnp.where(tri, col - row, -jnp.inf))
                w_c.append((scores[c][hd // per_group] * decay).astype(BF16))
            weights.append(w_c)
        y = []
        for c in chunks:
            acc = None
            for hd in range(SSD_HEADS):
                part = _dot(weights[c][hd], jnp.where(lane_head == hd, xdt[sls[c]], 0.0).astype(BF16))
                acc = part if acc is None else acc + part
            y.append(acc)
        h = h_ref[...]
        h_in = [None] * nc
        for c in (reversed(chunks) if reverse else chunks):
            h_in[c] = h
            h = h * jnp.exp2(tot[c]) + st[c]
        h_ref[...] = h
        y_off = [jnp.concatenate([_dot(c_c[c][g], h_in[c][:, g * half:(g + 1) * half].astype(BF16)) for g in groups],
                                 axis=1) for c in chunks]
        return [y[c] + y_off[c] * jnp.exp2(acs[c]) for c in chunks]

    @pl.when(j < nt)
    def _():
        for c, y in enumerate(scan(False)):
            yf_ref[pl.ds(base + c * CHUNK, CHUNK), :] = y

    @pl.when(j >= nt)
    def _():
        for c, yb in enumerate(scan(True)):
            sl = slice(c * CHUNK, (c + 1) * CHUNK)
            y = yf_ref[pl.ds(base + c * CHUNK, CHUNK), :] + yb + d_ref[...] * xs[sl]
            gated = y * _silu(z_ref[0, sl, :])
            ms = jnp.mean(gated * gated, axis=-1, keepdims=True)
            out_ref[0, sl, :] = gated * lax.rsqrt(ms + EPS) * g_ref[...]


def _ssd(ssd_g, cw, cb, dtb, a_exp, d_exp, g, tc):
    b, t, _ = ssd_g.shape
    nt = t // tc
    hb = tc // HALO

    def tile_of(j):
        return jnp.where(j < nt, j, 2 * nt - 1 - j)

    def conv_tile(j):
        return jnp.minimum(j, nt - 1)

    kern = functools.partial(_ssd_kernel, nt=nt, tc=tc)
    return pl.pallas_call(
        kern,
        grid=(b, 2 * nt),
        in_specs=[
            pl.BlockSpec((1, tc, SSD_XBC), lambda i, j: (i, conv_tile(j), 0)),
            pl.BlockSpec((1, HALO, SSD_XBC), lambda i, j: (i, jnp.maximum(conv_tile(j) * hb - 1, 0), 0)),
            pl.BlockSpec((1, HALO, SSD_XBC), lambda i, j: (i, jnp.minimum((conv_tile(j) + 1) * hb, t // HALO - 1), 0)),
            pl.BlockSpec((1, tc, GROUP_W), lambda i, j: (i, tile_of(j), SSD_XBC // GROUP_W)),
            pl.BlockSpec((1, tc, GROUP_W), lambda i, j: (i, tile_of(j), SSD_XBC // GROUP_W + 1 + j // nt)),
            pl.BlockSpec((HALO, SSD_XBC), lambda i, j: (0, 0)),
            pl.BlockSpec((1, SSD_XBC), lambda i, j: (0, 0)),
            pl.BlockSpec((1, 1, GROUP_W), lambda i, j: (j // nt, 0, 0)),
            pl.BlockSpec((1, 1, GROUP_W), lambda i, j: (j // nt, 0, 0)),
            pl.BlockSpec((1, GROUP_W), lambda i, j: (0, 0)),
            pl.BlockSpec((1, GROUP_W), lambda i, j: (0, 0)),
        ],
        out_specs=pl.BlockSpec((1, tc, GROUP_W), lambda i, j: (i, jnp.where(j < nt, nt - 1, 2 * nt - 1 - j), 0)),
        out_shape=jax.ShapeDtypeStruct((b, t, GROUP_W), F32),
        scratch_shapes=[pltpu.VMEM((t, GROUP_W), F32),
                        pltpu.VMEM((t, GROUP_W), F32),
                        pltpu.VMEM((t, 2 * SSD_GROUPS * SSD_STATE), BF16),
                        pltpu.VMEM((SSD_STATE, GROUP_W), F32),
                        pltpu.VMEM((tc + 2 * HALO, SSD_XBC), F32)],
        compiler_params=_params(dimension_semantics=("arbitrary", "arbitrary")),
        name="ssd",
    )(ssd_g, ssd_g, ssd_g, ssd_g, ssd_g, cw, cb, dtb, a_exp, d_exp, g)


GLA_QK_W = GLA_HEADS * GLA_DK
GLA_V_W = GLA_HEADS * GLA_DV


def _gla_kernel(q_ref, k_ref, v_ref, r_ref, lr_ref, w2_ref, gb_ref, g_ref, avg_ref,
                out_ref, of_ref, s_ref, *, nt, tc):
    j = pl.program_id(1)
    tile = jnp.where(j < nt, j, 2 * nt - 1 - j)

    @pl.when((j == 0) | (j == nt))
    def _():
        s_ref[...] = jnp.zeros_like(s_ref)

    base = pl.multiple_of(tile * tc, tc)
    nc = tc // CHUNK
    logit = _dot(lr_ref[0].astype(BF16), w2_ref[0]) + gb_ref[0]
    logg = -_softplus(-logit) * (LOG2E / GLA_TAU)
    q = q_ref[0] * (GLA_DK ** -0.5)
    k = k_ref[0]
    v = v_ref[0]
    qk_head = lax.broadcasted_iota(jnp.int32, (1, GLA_QK_W), 1) // GLA_DK
    v_head = lax.broadcasted_iota(jnp.int32, (1, GLA_V_W), 1) // GLA_DV
    blockdiag = (lax.broadcasted_iota(jnp.int32, (GLA_QK_W, GLA_V_W), 0) // GLA_DK
                 == lax.broadcasted_iota(jnp.int32, (GLA_QK_W, GLA_V_W), 1) // GLA_DV)

    def scan(reverse):
        tri = _tri(reverse)
        tri_bf = tri.astype(BF16)
        last = 0 if reverse else CHUNK - 1
        chunks = range(nc)
        sls = [slice(c * CHUNK, (c + 1) * CHUNK) for c in chunks]
        gcs = [_split_dot(tri_bf, logg[sl]) for sl in sls]
        tot = [g[last:last + 1] for g in gcs]
        dec_col = [jnp.exp2(g.T[:, last:last + 1]) for g in gcs]
        qg = [q[sl] * jnp.exp2(g) for sl, g in zip(sls, gcs)]
        kg = [(k[sl] * jnp.exp2(-g)).astype(BF16) for sl, g in zip(sls, gcs)]
        kd = [(k[sl] * jnp.exp2(t - g)).astype(BF16) for sl, g, t in zip(sls, gcs, tot)]
        raw = [[_dot_nt(jnp.where(qk_head == hd, qg[c], 0.0).astype(BF16), kg[c]) for hd in range(GLA_HEADS)]
               for c in chunks]
        st = [jnp.where(blockdiag, _dot_tn(kd[c], v[sls[c]].astype(BF16)), 0.0) for c in chunks]
        att = [[jnp.where(tri, raw[c][hd], 0.0).astype(BF16) for hd in range(GLA_HEADS)] for c in chunks]
        o = []
        for c in chunks:
            acc = None
            for hd in range(GLA_HEADS):
                part = _dot(att[c][hd], jnp.where(v_head == hd, v[sls[c]], 0.0).astype(BF16))
                acc = part if acc is None else acc + part
            o.append(acc)
        s = s_ref[...]
        s_in = [None] * nc
        for c in (reversed(chunks) if reverse else chunks):
            s_in[c] = s
            s = s * dec_col[c] + st[c]
        s_ref[...] = s
        return [o[c] + _dot(qg[c].astype(BF16), s_in[c].astype(BF16)) for c in chunks]

    @pl.when(j < nt)
    def _():
        for c, o in enumerate(scan(False)):
            of_ref[pl.ds(base + c * CHUNK, CHUNK), :] = o

    @pl.when(j >= nt)
    def _():
        for c, ob in enumerate(scan(True)):
            sl = slice(c * CHUNK, (c + 1) * CHUNK)
            o = of_ref[pl.ds(base + c * CHUNK, CHUNK), :] + ob
            ms = _split_dot_right(o * o, avg_ref[...])
            out_ref[0, sl, :] = o * lax.rsqrt(ms + EPS) * g_ref[...] * _silu(r_ref[0, sl, :])


def _split_dot_right(x, m_bf16):
    x1 = x.astype(BF16)
    r1 = x - x1.astype(F32)
    x2 = r1.astype(BF16)
    x3 = (r1 - x2.astype(F32)).astype(BF16)
    return _dot(x1, m_bf16) + _dot(x2, m_bf16) + _dot(x3, m_bf16)


def _gla(gla_g, w2p, gb, g, avg, tc):
    b, t, _ = gla_g.shape
    nt = t // tc

    def tile_of(j):
        return jnp.where(j < nt, j, 2 * nt - 1 - j)

    kern = functools.partial(_gla_kernel, nt=nt, tc=tc)
    return pl.pallas_call(
        kern,
        grid=(b, 2 * nt),
        in_specs=[
            pl.BlockSpec((1, tc, GLA_QK_W), lambda i, j: (i, tile_of(j), 0)),
            pl.BlockSpec((1, tc, GLA_QK_W), lambda i, j: (i, tile_of(j), 1)),
            pl.BlockSpec((1, tc, GLA_V_W), lambda i, j: (i, tile_of(j), 1)),
            pl.BlockSpec((1, tc, GLA_V_W), lambda i, j: (i, tile_of(j), 2)),
            pl.BlockSpec((1, tc, LANE), lambda i, j: (i, tile_of(j), (2 * GLA_QK_W + 2 * GLA_V_W) // LANE)),
            pl.BlockSpec((1, LANE, GLA_QK_W), lambda i, j: (j // nt, 0, 0)),
            pl.BlockSpec((1, 1, GLA_QK_W), lambda i, j: (j // nt, 0, 0)),
            pl.BlockSpec((1, GLA_V_W), lambda i, j: (0, 0)),
            pl.BlockSpec((GLA_V_W, GLA_V_W), lambda i, j: (0, 0)),
        ],
        out_specs=pl.BlockSpec((1, tc, GLA_V_W), lambda i, j: (i, jnp.where(j < nt, nt - 1, 2 * nt - 1 - j), 0)),
        out_shape=jax.ShapeDtypeStruct((b, t, GLA_V_W), F32),
        scratch_shapes=[pltpu.VMEM((t, GLA_V_W), F32),
                        pltpu.VMEM((GLA_QK_W, GLA_V_W), F32)],
        compiler_params=_params(dimension_semantics=("arbitrary", "arbitrary")),
        name="gla",
    )(gla_g, gla_g, gla_g, gla_g, gla_g, w2p, gb, g, avg)


def _rope(y, cos, sin, half):
    lane = lax.broadcasted_iota(jnp.int32, (1, LANE), 1)
    lo = (lane % (2 * half)) < half
    rot = jnp.where(lo, pltpu.roll(y, LANE - half, 1), pltpu.roll(y, half, 1))
    return y * cos + rot * sin


ONES_LANE = LANE - 1
VT_ROWS = 80
SCORE_BOUND_LIMIT = 48.0


def _finish_keys(k):
    kf = k.astype(BF16).astype(F32)
    norm = jnp.sqrt(jnp.max(jnp.sum(kf * kf, axis=-1, keepdims=True), axis=0, keepdims=True))
    lane = lax.broadcasted_iota(jnp.int32, (1, LANE), 1)
    return jnp.where(lane == ONES_LANE, 1.0, k).astype(BF16), norm


def _update_kmax(kmax_ref, norms):
    @pl.when(pl.program_id(1) == 0)
    def _():
        kmax_ref[...] = jnp.zeros_like(kmax_ref)

    lane = lax.broadcasted_iota(jnp.int32, (1, LANE), 1)
    upd = jnp.zeros((1, LANE), F32)
    for g, norm in enumerate(norms):
        upd = jnp.where(lane == g, norm, upd)
    kmax_ref[0] = jnp.maximum(kmax_ref[0], upd)


def _vt_rows(v, heads, dv):
    tm = v.shape[0]
    vt = v.T.reshape(heads, dv, tm)
    row = lax.broadcasted_iota(jnp.int32, (heads, VT_ROWS - dv, tm), 1)
    return jnp.concatenate([vt, jnp.where(row == 0, 1.0, 0.0)], axis=1).astype(BF16)


def _attn_kernel(q_ref, k_ref, vt_ref, kmax_ref, o_ref, qa_ref, m_ref, acc_ref, *, n_q, rep, tk, dv):
    nk = k_ref.shape[1] // tk
    lane = lax.broadcasted_iota(jnp.int32, (1, LANE), 1)
    kmax = kmax_ref[0, 0:1, :]
    bound_max = None
    for h in range(n_q):
        qf = q_ref[0, :, h * LANE:(h + 1) * LANE].astype(F32)
        kg = jnp.max(jnp.where(lane == h // rep, kmax, 0.0), axis=-1, keepdims=True)
        bound = jnp.sqrt(jnp.sum(qf * qf, axis=-1, keepdims=True)) * kg
        qa_ref[h] = jnp.where(lane == ONES_LANE, -bound, qf).astype(BF16)
        top = jnp.max(bound)
        bound_max = top if bound_max is None else jnp.maximum(bound_max, top)
    acc_ref[...] = jnp.zeros(acc_ref.shape, F32)

    @pl.when(bound_max <= SCORE_BOUND_LIMIT)
    def _():
        def scores(i, h):
            ks = pl.multiple_of(i * tk, tk)
            g = h // rep
            return _dot_nt(k_ref[0, pl.ds(ks, tk), g * LANE:(g + 1) * LANE], qa_ref[h])

        group = max(u for u in (1, 2, 4) if nk % u == 0)

        def body(ii, carry):
            steps = [(ii * group + u, h) for u in range(group) for h in range(n_q)]
            s_next = scores(*steps[0])
            for n, (i, h) in enumerate(steps):
                s = s_next
                if n + 1 < len(steps):
                    s_next = scores(*steps[n + 1])
                acc_ref[h] += _dot(vt_ref[0, h // rep, i], jnp.exp2(s).astype(BF16))
            return carry

        lax.fori_loop(0, nk // group, body, 0)

    @pl.when(bound_max > SCORE_BOUND_LIMIT)
    def _():
        m_ref[...] = jnp.full(m_ref.shape, -jnp.inf, F32)

        def body(i, carry):
            ks = pl.multiple_of(i * tk, tk)
            for h in range(n_q):
                g = h // rep
                s = _dot_nt(k_ref[0, pl.ds(ks, tk), g * LANE:(g + 1) * LANE], q_ref[0, :, h * LANE:(h + 1) * LANE])
                m_old = m_ref[h]
                m_new = jnp.maximum(m_old, jnp.max(s, axis=0, keepdims=True))
                p = jnp.exp2(s - m_new)
                acc_ref[h] = acc_ref[h] * jnp.exp2(m_old - m_new) + _dot(vt_ref[0, g, i], p.astype(BF16))
                m_ref[h] = m_new
            return carry

        lax.fori_loop(0, nk, body, 0)

    o = jnp.concatenate([acc_ref[h, 0:dv, :] * (1.0 / acc_ref[h, dv:dv + 1, :]) for h in range(n_q)], axis=0)
    o_ref[0] = o.T


def _attention(q, k, vt, kmax, rep, tq, tk, dv):
    b, t, qw = q.shape
    kw = k.shape[2]
    n_q = qw // LANE
    n_kv = n_q // rep
    ow = n_q * dv
    kern = functools.partial(_attn_kernel, n_q=n_q, rep=rep, tk=tk, dv=dv)
    return pl.pallas_call(
        kern,
        grid=(b, t // tq),
        in_specs=[
            pl.BlockSpec((1, tq, qw), lambda i, j: (i, j, 0)),
            pl.BlockSpec((1, t, kw), lambda i, j: (i, 0, 0)),
            pl.BlockSpec((1, n_kv, t // tk, VT_ROWS, tk), lambda i, j: (i, 0, 0, 0, 0)),
            pl.BlockSpec((1, 8, LANE), lambda i, j: (i, 0, 0)),
        ],
        out_specs=pl.BlockSpec((1, tq, ow), lambda i, j: (i, j, 0)),
        out_shape=jax.ShapeDtypeStruct((b, t, ow), F32),
        scratch_shapes=[pltpu.VMEM((n_q, tq, LANE), BF16), pltpu.VMEM((n_q, 1, tq), F32),
                        pltpu.VMEM((n_q, VT_ROWS, tq), F32)],
        compiler_params=_params(dimension_semantics=("arbitrary", "arbitrary")),
        name="attention",
    )(q, k, vt, kmax)


FF_CHUNK = 1024


def _outmlp_kernel(x_ref, m0_ref, m1_ref, m2_ref, m3_ref, wo_ref, g2_ref, w1_ref, w2_ref, gf_ref, o_ref, *, final):
    x1 = x_ref[...]
    for i, m_ref in enumerate((m0_ref, m1_ref, m2_ref, m3_ref)):
        x1 = x1 + _dot(m_ref[...].astype(BF16), wo_ref[i * GROUP_W:(i + 1) * GROUP_W, :])
    ms = jnp.mean(x1 * x1, axis=-1, keepdims=True)
    h = (x1 * lax.rsqrt(ms + EPS) * g2_ref[...]).astype(BF16)
    acc = None
    for c in range(D_FF // FF_CHUNK):
        sl = slice(c * FF_CHUNK, (c + 1) * FF_CHUNK)
        u = jnp.maximum(_dot(h, w1_ref[:, sl]), 0.0)
        part = _dot((u * u).astype(BF16), w2_ref[sl, :])
        acc = part if acc is None else acc + part
    y = x1 + acc
    if final:
        ms = jnp.mean(y * y, axis=-1, keepdims=True)
        y = y * lax.rsqrt(ms + EPS) * gf_ref[...]
    o_ref[...] = y


def _outmlp(x2d, mixes, wo, g2, w1, w2, gf, final, tm):
    n = x2d.shape[0]
    const = lambda i: (0, 0)
    single = pl.Buffered(1)
    kern = functools.partial(_outmlp_kernel, final=final)
    return pl.pallas_call(
        kern,
        grid=(n // tm,),
        in_specs=[pl.BlockSpec((tm, D_MODEL), lambda i: (i, 0))]
                 + [pl.BlockSpec((tm, GROUP_W), lambda i: (i, 0)) for _ in range(4)]
                 + [pl.BlockSpec((D_MODEL, D_MODEL), const, pipeline_mode=single),
                    pl.BlockSpec((1, D_MODEL), const),
                    pl.BlockSpec((D_MODEL, D_FF), const, pipeline_mode=single),
                    pl.BlockSpec((D_FF, D_MODEL), const, pipeline_mode=single),
                    pl.BlockSpec((1, D_MODEL), const)],
        out_specs=pl.BlockSpec((tm, D_MODEL), lambda i: (i, 0)),
        out_shape=jax.ShapeDtypeStruct((n, D_MODEL), F32),
        compiler_params=_params(dimension_semantics=("arbitrary",)),
        name="outmlp",
    )(x2d, *mixes, wo, g2.reshape(1, D_MODEL), w1, w2, gf.reshape(1, D_MODEL))


def _rope_tables(t):
    pos = jnp.arange(t, dtype=jnp.int32)
    row = (pos // GRID_W).astype(F32)
    col = (pos % GRID_W).astype(F32)

    def block(p, n):
        inv_freq = ROPE_THETA ** (-jnp.arange(n, dtype=F32) / n)
        ang = p[:, None] * inv_freq[None, :]
        c, s = jnp.cos(ang), jnp.sin(ang)
        return jnp.concatenate([c, c], axis=1), jnp.concatenate([-s, s], axis=1)

    def table(n, lead):
        cr, sr = block(row, n)
        cc, sc = block(col, n)
        tail = LANE - lead - 4 * n
        cos = jnp.concatenate([jnp.ones((t, lead), F32), cr, cc, jnp.ones((t, tail), F32)], axis=1)
        sin = jnp.concatenate([jnp.zeros((t, lead), F32), sr, sc, jnp.zeros((t, tail), F32)], axis=1)
        return cos, sin

    return table(GQA_HEAD_DIM // 4, 0), table(MLA_ROPE // 4, MLA_NOPE)


def _pad_lanes(v, width=LANE):
    return jnp.concatenate([v, jnp.zeros((width - v.shape[0],), v.dtype)]).reshape(1, width)


def _layer_params(i, p):
    (norm1_g, w_in, ssd_conv_w, ssd_conv_b, ssd_dt_bias, ssd_a_log, ssd_d, ssd_norm_g,
     gqa_q_norm_g, gqa_k_norm_g, gla_gate_w2, gla_gate_b, gla_norm_g, mla_q_norm_g, mla_w_uq,
     mla_kv_norm_g, mla_w_ukv, w_out, norm2_g, w_ff1, w_ff2) = [a[i] for a in p]
    out = {}
    out["norm1_g"] = norm1_g
    out["w_all"] = _gather_cols(w_in, _IN_COLS).astype(BF16)
    out["conv_w"] = jnp.concatenate([ssd_conv_w, jnp.zeros((HALO - SSD_CONV_W, SSD_XBC), F32)], axis=0)
    out["conv_b"] = ssd_conv_b.reshape(1, SSD_XBC)
    expand = lambda a: jnp.repeat(a, SSD_HEAD_DIM, axis=-1)
    out["dt_bias"] = expand(ssd_dt_bias).reshape(2, 1, GROUP_W)
    out["a_neg"] = expand(-jnp.exp(ssd_a_log) * LOG2E).reshape(2, 1, GROUP_W)
    out["ssd_d"] = expand(ssd_d).reshape(1, GROUP_W)
    out["ssd_norm_g"] = ssd_norm_g.reshape(1, GROUP_W)
    out["gq"] = _pad_lanes(gqa_q_norm_g)
    out["gk"] = _pad_lanes(gqa_k_norm_g)
    w2p = jnp.zeros((2, LANE, GLA_QK_W), F32)
    for d in range(2):
        w2p = w2p.at[d, d * GLA_LOWRANK:(d + 1) * GLA_LOWRANK, :].set(gla_gate_w2[d])
    out["gla_w2"] = w2p.astype(BF16)
    out["gla_b"] = gla_gate_b.reshape(2, 1, GLA_QK_W)
    out["gla_norm_g"] = jnp.tile(gla_norm_g, GLA_HEADS).reshape(1, GLA_V_W)
    out["mla_gq"] = mla_q_norm_g.reshape(1, MLA_Q_LORA)
    out["mla_gkv"] = mla_kv_norm_g.reshape(1, MLA_KV_LORA)
    uq_cols, uk_cols, uv_cols = [], [], []
    for h in range(MLA_HEADS):
        uq_cols += list(range(h * MLA_QK, (h + 1) * MLA_QK)) + [-1] * (LANE - MLA_QK)
        base = h * (MLA_NOPE + MLA_V)
        uk_cols += list(range(base, base + MLA_NOPE)) + [-1] * (LANE - MLA_NOPE)
        uv_cols += list(range(base + MLA_NOPE, base + MLA_NOPE + MLA_V))
    out["wuq"] = _gather_cols(mla_w_uq, np.asarray(uq_cols, np.int32)).astype(BF16)
    out["wuk"] = _gather_cols(mla_w_ukv, np.asarray(uk_cols, np.int32)).astype(BF16)
    out["wuv"] = _gather_cols(mla_w_ukv, np.asarray(uv_cols, np.int32)).astype(BF16)
    out["w_out"] = w_out.astype(BF16)
    out["norm2_g"] = norm2_g
    out["w_ff1"] = w_ff1.astype(BF16)
    out["w_ff2"] = w_ff2.astype(BF16)
    return out


TILE_PREF = dict(tc=1024, tp=512, tq_gqa=512, tq_mla=512, tmlp=512)


def _tiles(t):
    pick = lambda pref: max(c for c in (128, 256, 512, 1024) if c <= pref and t % c == 0)
    return {name: pick(pref) for name, pref in TILE_PREF.items()}


def _trunk(x, layers, final_norm_g):
    b, t, d = x.shape
    n = b * t
    ts = _tiles(t)
    tk = ts["tp"]
    tables = _rope_tables(t)
    avg = jnp.asarray(np.kron(np.eye(GLA_HEADS), np.full((GLA_DV, GLA_DV), 1.0 / GLA_DV)), BF16)
    x2d = x.reshape(n, d)
    for i, lp in enumerate(layers):
        ssd_g, gla_g, gq, gk, gvt, gkmax, mq, mk, mvt, mkmax = _inproj(x2d.reshape(b, t, d), lp, tables, tk)
        ssd_out = _ssd(ssd_g, lp["conv_w"], lp["conv_b"], lp["dt_bias"], lp["a_neg"],
                       lp["ssd_d"], lp["ssd_norm_g"], ts["tc"])
        gqa_out = _attention(gq, gk, gvt, gkmax, GQA_HEADS // GQA_KV_HEADS, ts["tq_gqa"], tk, GQA_HEAD_DIM)
        gla_out = _gla(gla_g, lp["gla_w2"], lp["gla_b"], lp["gla_norm_g"], avg, ts["tc"])
        mla_out = _attention(mq, mk, mvt, mkmax, 1, ts["tq_mla"], tk, MLA_V)
        mixes = [m.reshape(n, GROUP_W) for m in (ssd_out, gqa_out, gla_out, mla_out)]
        x2d = _outmlp(x2d, mixes, lp["w_out"], lp["norm2_g"], lp["w_ff1"], lp["w_ff2"], final_norm_g,
                      i == len(layers) - 1, ts["tmlp"])
    return x2d.reshape(b, t, d)


def kernel(x_prompt, x_sample, norm1_g, w_in, ssd_conv_w, ssd_conv_b, ssd_dt_bias, ssd_a_log, ssd_d, ssd_norm_g,
           gqa_q_norm_g, gqa_k_norm_g, gla_gate_w2, gla_gate_b, gla_norm_g, mla_q_norm_g, mla_w_uq,
           mla_kv_norm_g, mla_w_ukv, w_out, norm2_g, w_ff1, w_ff2, final_norm_g):
    stacked = (norm1_g, w_in, ssd_conv_w, ssd_conv_b, ssd_dt_bias, ssd_a_log, ssd_d, ssd_norm_g,
               gqa_q_norm_g, gqa_k_norm_g, gla_gate_w2, gla_gate_b, gla_norm_g, mla_q_norm_g, mla_w_uq,
               mla_kv_norm_g, mla_w_ukv, w_out, norm2_g, w_ff1, w_ff2)
    layers = [_layer_params(i, stacked) for i in range(norm1_g.shape[0])]
    return (_trunk(x_prompt, layers, final_norm_g), _trunk(x_sample, layers, final_norm_g))
```

```python
import functools

import numpy as np
import jax
import jax.numpy as jnp
from jax import lax
from jax.experimental import pallas as pl
from jax.experimental.pallas import tpu as pltpu

F32 = jnp.float32
BF16 = jnp.bfloat16

D_MODEL = 1024
DEPTH = 2
GRID_W = 64
CHUNK = 128
ROPE_THETA = 10000.0
EPS = 1e-6
GROUP_W = D_MODEL // 4
D_FF = 4 * D_MODEL

SSD_HEADS = 4
SSD_HEAD_DIM = GROUP_W // SSD_HEADS
SSD_GROUPS = 2
SSD_STATE = 128
SSD_CONV_W = 5
SSD_XBC = GROUP_W + 2 * SSD_GROUPS * SSD_STATE

GQA_HEADS = 4
GQA_KV_HEADS = 2
GQA_HEAD_DIM = GROUP_W // GQA_HEADS

GLA_HEADS = 4
GLA_DV = GROUP_W // GLA_HEADS
GLA_DK = GLA_DV // 2
GLA_LOWRANK = 16
GLA_TAU = 16.0

MLA_HEADS = 4
MLA_Q_LORA = 256
MLA_KV_LORA = 128
MLA_NOPE = 64
MLA_ROPE = 32
MLA_V = GROUP_W // MLA_HEADS
MLA_QK = MLA_NOPE + MLA_ROPE

IN_SIZES = (GROUP_W, SSD_XBC, 2 * SSD_HEADS,
            GQA_HEADS * GQA_HEAD_DIM, GQA_KV_HEADS * GQA_HEAD_DIM, GQA_KV_HEADS * GQA_HEAD_DIM,
            GLA_HEADS * GLA_DK, GLA_HEADS * GLA_DK, GLA_HEADS * GLA_DV, GLA_HEADS * GLA_DV, 2 * GLA_LOWRANK,
            MLA_Q_LORA, MLA_KV_LORA, MLA_ROPE)
IN_OFFS = tuple(int(v) for v in np.concatenate([[0], np.cumsum(IN_SIZES)]))

LOG2E = 1.4426950408889634
LANE = 128
HALO = 8
VMEM_LIMIT = 52 * 1024 * 1024

SSD_W = SSD_XBC + GROUP_W + LANE
GQA_W = GQA_HEADS * LANE + GQA_KV_HEADS * LANE + GQA_KV_HEADS * GQA_HEAD_DIM
GLA_W = 2 * GLA_HEADS * GLA_DK + 2 * GLA_HEADS * GLA_DV + LANE
MLA_W = MLA_Q_LORA + MLA_KV_LORA + LANE
PROJ_W = SSD_W + GQA_W + GLA_W + MLA_W


def _in_proj_columns():
    o = IN_OFFS
    cols = []
    cols += list(range(o[1], o[2]))
    cols += list(range(o[0], o[1]))
    cols += list(range(o[2], o[3])) + [-1] * (LANE - 2 * SSD_HEADS)
    for h in range(GQA_HEADS):
        cols += list(range(o[3] + h * GQA_HEAD_DIM, o[3] + (h + 1) * GQA_HEAD_DIM)) + [-1] * (LANE - GQA_HEAD_DIM)
    for h in range(GQA_KV_HEADS):
        cols += list(range(o[4] + h * GQA_HEAD_DIM, o[4] + (h + 1) * GQA_HEAD_DIM)) + [-1] * (LANE - GQA_HEAD_DIM)
    cols += list(range(o[5], o[6]))
    cols += list(range(o[6], o[10]))
    cols += list(range(o[10], o[11])) + [-1] * (LANE - 2 * GLA_LOWRANK)
    cols += list(range(o[11], o[13]))
    cols += [-1] * MLA_NOPE + list(range(o[13], o[14])) + [-1] * (LANE - MLA_QK)
    cols = np.asarray(cols, np.int32)
    assert cols.shape[0] == PROJ_W
    return cols


_IN_COLS = _in_proj_columns()


def _gather_cols(w, cols):
    picked = jnp.take(w, jnp.asarray(np.maximum(cols, 0)), axis=1)
    return jnp.where(jnp.asarray(cols >= 0)[None, :], picked, 0.0)


def _params(**kw):
    return pltpu.CompilerParams(vmem_limit_bytes=VMEM_LIMIT, **kw)


def _silu(x):
    return x * (1.0 / (1.0 + jnp.exp(-x)))


def _softplus(x):
    return jnp.maximum(x, 0.0) + jnp.log1p(jnp.exp(-jnp.abs(x)))


def _dot(a, b):
    return jnp.dot(a, b, preferred_element_type=F32)


def _dot_nt(a, b):
    return lax.dot_general(a, b, (((1,), (1,)), ((), ())), preferred_element_type=F32)


def _dot_tn(a, b):
    return lax.dot_general(a, b, (((0,), (0,)), ((), ())), preferred_element_type=F32)


def _split_dot(m_bf16, x):
    x1 = x.astype(BF16)
    r1 = x - x1.astype(F32)
    x2 = r1.astype(BF16)
    x3 = (r1 - x2.astype(F32)).astype(BF16)
    return _dot(m_bf16, x1) + _dot(m_bf16, x2) + _dot(m_bf16, x3)


def _tri(reverse):
    li = lax.broadcasted_iota(jnp.int32, (CHUNK, CHUNK), 0)
    si = lax.broadcasted_iota(jnp.int32, (CHUNK, CHUNK), 1)
    return (si >= li) if reverse else (si <= li)


def _inproj_kernel(x_ref, g_ref, w_ref, cosg_ref, sing_ref, cosm_ref, sinm_ref, gq_ref, gk_ref,
                   mgq_ref, mgkv_ref, wuq_ref, wuk_ref, wuv_ref,
                   ssd_ref, gla_ref, gq_out, gk_out, gvt_out, gkmax_out, mq_out, mk_out, mvt_out, mkmax_out):
    x = x_ref[0]
    ms = jnp.mean(x * x, axis=-1, keepdims=True)
    h = (x * lax.rsqrt(ms + EPS) * g_ref[...]).astype(BF16)
    c_gqa, c_gla, c_mla = SSD_W, SSD_W + GQA_W, SSD_W + GQA_W + GLA_W
    mla = _dot(h, w_ref[:, c_mla:PROJ_W])
    gqa = _dot(h, w_ref[:, c_gqa:c_gla])

    def rms(y, g):
        ms_y = jnp.mean(y * y, axis=-1, keepdims=True)
        return (y * lax.rsqrt(ms_y + EPS) * g).astype(BF16)

    cq = rms(mla[:, :MLA_Q_LORA], mgq_ref[...])
    ckv = rms(mla[:, MLA_Q_LORA:MLA_Q_LORA + MLA_KV_LORA], mgkv_ref[...])
    ssd_ref[0] = _dot(h, w_ref[:, 0:SSD_W])
    q = _dot(cq, wuq_ref[...])
    kn = _dot(ckv, wuk_ref[...])
    v = _dot(ckv, wuv_ref[...])

    cos, sin = cosg_ref[...], sing_ref[...]
    qw, kw = GQA_HEADS * LANE, GQA_KV_HEADS * LANE
    heads = [gqa[:, hd * LANE:(hd + 1) * LANE] for hd in range(GQA_HEADS + GQA_KV_HEADS)]
    gains = [gq_ref[...]] * GQA_HEADS + [gk_ref[...]] * GQA_KV_HEADS
    inv = [lax.rsqrt(jnp.sum(y * y, axis=-1, keepdims=True) * (1.0 / GQA_HEAD_DIM) + EPS) for y in heads]
    roped = [_rope(y * r * g, cos, sin, GQA_HEAD_DIM // 4) for y, r, g in zip(heads, inv, gains)]
    for hd in range(GQA_HEADS):
        gq_out[0, :, hd * LANE:(hd + 1) * LANE] = (roped[hd] * (GQA_HEAD_DIM ** -0.5 * LOG2E)).astype(BF16)
    norms = []
    for hd in range(GQA_KV_HEADS):
        gk_out[0, :, hd * LANE:(hd + 1) * LANE], norm = _finish_keys(roped[GQA_HEADS + hd])
        norms.append(norm)
    _update_kmax(gkmax_out, norms)
    gvt_out[0, :, 0] = _vt_rows(gqa[:, qw + kw:], GQA_KV_HEADS, GQA_HEAD_DIM)
    gla_ref[0] = _dot(h, w_ref[:, c_gla:c_mla])

    cos, sin = cosm_ref[...], sinm_ref[...]
    k_rope = _rope(mla[:, MLA_Q_LORA + MLA_KV_LORA:], cos, sin, MLA_ROPE // 4)
    q_roped = [_rope(q[:, hd * LANE:(hd + 1) * LANE], cos, sin, MLA_ROPE // 4) for hd in range(MLA_HEADS)]
    norms = []
    for hd in range(MLA_HEADS):
        sl = slice(hd * LANE, (hd + 1) * LANE)
        mq_out[0, :, sl] = (q_roped[hd] * (MLA_QK ** -0.5 * LOG2E)).astype(BF16)
        mk_out[0, :, sl], norm = _finish_keys(kn[:, sl] + k_rope)
        norms.append(norm)
    _update_kmax(mkmax_out, norms)
    mvt_out[0, :, 0] = _vt_rows(v, MLA_HEADS, MLA_V)


def _inproj(x, lp, tables, tm):
    b, t, _ = x.shape
    (cos_g, sin_g), (cos_m, sin_m) = tables
    const = lambda i, j: (0, 0)
    tok = lambda i, j: (i, j, 0)
    tab = pl.BlockSpec((tm, LANE), lambda i, j: (j, 0))
    qw, kw, mw = GQA_HEADS * LANE, GQA_KV_HEADS * LANE, MLA_HEADS * LANE

    def vt_spec(heads):
        return pl.BlockSpec((1, heads, 1, VT_ROWS, tm), lambda i, j: (i, 0, j, 0, 0))

    kmax_spec = pl.BlockSpec((1, 8, LANE), lambda i, j: (i, 0, 0))
    return pl.pallas_call(
        _inproj_kernel,
        grid=(b, t // tm),
        in_specs=[pl.BlockSpec((1, tm, D_MODEL), tok),
                  pl.BlockSpec((1, D_MODEL), const),
                  pl.BlockSpec((D_MODEL, PROJ_W), const, pipeline_mode=pl.Buffered(1)),
                  tab, tab, tab, tab,
                  pl.BlockSpec((1, LANE), const), pl.BlockSpec((1, LANE), const),
                  pl.BlockSpec((1, MLA_Q_LORA), const), pl.BlockSpec((1, MLA_KV_LORA), const),
                  pl.BlockSpec((MLA_Q_LORA, mw), const), pl.BlockSpec((MLA_KV_LORA, mw), const),
                  pl.BlockSpec((MLA_KV_LORA, MLA_HEADS * MLA_V), const)],
        out_specs=[pl.BlockSpec((1, tm, SSD_W), tok), pl.BlockSpec((1, tm, GLA_W), tok),
                   pl.BlockSpec((1, tm, qw), tok), pl.BlockSpec((1, tm, kw), tok), vt_spec(GQA_KV_HEADS), kmax_spec,
                   pl.BlockSpec((1, tm, mw), tok), pl.BlockSpec((1, tm, mw), tok), vt_spec(MLA_HEADS), kmax_spec],
        out_shape=[jax.ShapeDtypeStruct((b, t, SSD_W), F32), jax.ShapeDtypeStruct((b, t, GLA_W), F32),
                   jax.ShapeDtypeStruct((b, t, qw), BF16), jax.ShapeDtypeStruct((b, t, kw), BF16),
                   jax.ShapeDtypeStruct((b, GQA_KV_HEADS, t // tm, VT_ROWS, tm), BF16),
                   jax.ShapeDtypeStruct((b, 8, LANE), F32),
                   jax.ShapeDtypeStruct((b, t, mw), BF16), jax.ShapeDtypeStruct((b, t, mw), BF16),
                   jax.ShapeDtypeStruct((b, MLA_HEADS, t // tm, VT_ROWS, tm), BF16),
                   jax.ShapeDtypeStruct((b, 8, LANE), F32)],
        compiler_params=_params(dimension_semantics=("arbitrary", "arbitrary")),
        name="inproj",
    )(x, lp["norm1_g"].reshape(1, D_MODEL), lp["w_all"], cos_g, sin_g, cos_m, sin_m, lp["gq"], lp["gk"],
      lp["mla_gq"], lp["mla_gkv"], lp["wuq"], lp["wuk"], lp["wuv"])


def _ssd_kernel(xbc_ref, prev_ref, next_ref, z_ref, dtx_ref, cw_ref, cb_ref, dtb_ref, ex_ref, a_ref, d_ref, g_ref,
                out_ref, yf_ref, xs_ref, bc_ref, h_ref, xe_ref, *, nt, tc):
    j = pl.program_id(1)
    tile = jnp.where(j < nt, j, 2 * nt - 1 - j)
    base = pl.multiple_of(tile * tc, tc)
    nc = tc // CHUNK
    half = GROUP_W // SSD_GROUPS

    @pl.when((j == 0) | (j == nt))
    def _():
        h_ref[...] = jnp.zeros_like(h_ref)

    @pl.when(j < nt)
    def _():
        xe_ref[HALO:HALO + tc, :] = xbc_ref[0]
        xe_ref[0:HALO, :] = jnp.where(tile > 0, prev_ref[0], 0.0)
        xe_ref[HALO + tc:2 * HALO + tc, :] = jnp.where(tile < nt - 1, next_ref[0], 0.0)
        pad = SSD_CONV_W // 2
        acc = cb_ref[...] + cw_ref[0:1, :] * xe_ref[HALO - pad:HALO - pad + tc, :]
        for w in range(1, SSD_CONV_W):
            acc = acc + cw_ref[w:w + 1, :] * xe_ref[HALO - pad + w:HALO - pad + w + tc, :]
        xbc = _silu(acc)
        xs_ref[pl.ds(base, tc), :] = xbc[:, :GROUP_W]
        bc_ref[pl.ds(base, tc), :] = xbc[:, GROUP_W:].astype(BF16)

    xs = xs_ref[pl.ds(base, tc), :]
    dt_heads = _softplus(dtx_ref[0] + dtb_ref[...])
    dt = _split_dot_right(dt_heads, ex_ref[0])
    a_dt = dt * a_ref[0]
    xdt = xs * dt
    lane_head = lax.broadcasted_iota(jnp.int32, (1, GROUP_W), 1) // SSD_HEAD_DIM

    def scan(reverse):
        tri = _tri(reverse)
        tri_bf = tri.astype(BF16)
        chunks = range(nc)
        groups = range(SSD_GROUPS)
        per_group = SSD_HEADS // SSD_GROUPS
        sls = [slice(c * CHUNK, (c + 1) * CHUNK) for c in chunks]
        rows = [pl.ds(base + c * CHUNK, CHUNK) for c in chunks]
        acs = [_split_dot(tri_bf, a_dt[sl]) for sl in sls]
        b_c = [[bc_ref[rows[c], g * SSD_STATE:(g + 1) * SSD_STATE] for g in groups] for c in chunks]
        c_c = [[bc_ref[rows[c], (SSD_GROUPS + g) * SSD_STATE:(SSD_GROUPS + g + 1) * SSD_STATE] for g in groups]
               for c in chunks]
        scores = [[_dot_nt(c_c[c][g], b_c[c][g]) for g in groups] for c in chunks]
        tot = [a[0:1] if reverse else a[CHUNK - 1:CHUNK] for a in acs]
        acs_t = [a.T for a in acs]
        x_d = [(xdt[sls[c]] * jnp.exp2(tot[c] - acs[c])).astype(BF16) for c in chunks]
        st = [jnp.concatenate([_dot_tn(b_c[c][g], x_d[c][:, g * half:(g + 1) * half]) for g in groups], axis=1)
              for c in chunks]
        weights = []
        for c in chunks:
            w_c = []
            for hd in range(SSD_HEADS):
                col = jnp.broadcast_to(acs[c][:, hd * SSD_HEAD_DIM:hd * SSD_HEAD_DIM + 1], (CHUNK, CHUNK))
                row = jnp.broadcast_to(acs_t[c][hd * SSD_HEAD_DIM:hd * SSD_HEAD_DIM + 1, :], (CHUNK, CHUNK))
                decay = jnp.exp2(jnp.where(tri, col - row, -jnp.inf))
                w_c.append((scores[c][hd // per_group] * decay).astype(BF16))
            weights.append(w_c)
        y = []
        for c in chunks:
            acc = None
            for hd in range(SSD_HEADS):
                part = _dot(weights[c][hd], jnp.where(lane_head == hd, xdt[sls[c]], 0.0).astype(BF16))
                acc = part if acc is None else acc + part
            y.append(acc)
        h = h_ref[...]
        h_in = [None] * nc
        for c in (reversed(chunks) if reverse else chunks):
            h_in[c] = h
            h = h * jnp.exp2(tot[c]) + st[c]
        h_ref[...] = h
        y_off = [jnp.concatenate([_dot(c_c[c][g], h_in[c][:, g * half:(g + 1) * half].astype(BF16)) for g in groups],
                                 axis=1) for c in chunks]
        return [y[c] + y_off[c] * jnp.exp2(acs[c]) for c in chunks]

    @pl.when(j < nt)
    def _():
        for c, y in enumerate(scan(False)):
            yf_ref[pl.ds(base + c * CHUNK, CHUNK), :] = y

    @pl.when(j >= nt)
    def _():
        for c, yb in enumerate(scan(True)):
            sl = slice(c * CHUNK, (c + 1) * CHUNK)
            y = yf_ref[pl.ds(base + c * CHUNK, CHUNK), :] + yb + d_ref[...] * xs[sl]
            gated = y * _silu(z_ref[0, sl, :])
            ms = jnp.mean(gated * gated, axis=-1, keepdims=True)
            out_ref[0, sl, :] = gated * lax.rsqrt(ms + EPS) * g_ref[...]


def _ssd(ssd_g, cw, cb, dtb, expand, a_exp, d_exp, g, tc):
    b, t, _ = ssd_g.shape
    nt = t // tc
    hb = tc // HALO

    def tile_of(j):
        return jnp.where(j < nt, j, 2 * nt - 1 - j)

    def conv_tile(j):
        return jnp.minimum(j, nt - 1)

    kern = functools.partial(_ssd_kernel, nt=nt, tc=tc)
    return pl.pallas_call(
        kern,
        grid=(b, 2 * nt),
        in_specs=[
            pl.BlockSpec((1, tc, SSD_XBC), lambda i, j: (i, conv_tile(j), 0)),
            pl.BlockSpec((1, HALO, SSD_XBC), lambda i, j: (i, jnp.maximum(conv_tile(j) * hb - 1, 0), 0)),
            pl.BlockSpec((1, HALO, SSD_XBC), lambda i, j: (i, jnp.minimum((conv_tile(j) + 1) * hb, t // HALO - 1), 0)),
            pl.BlockSpec((1, tc, GROUP_W), lambda i, j: (i, tile_of(j), SSD_XBC // GROUP_W)),
            pl.BlockSpec((1, tc, LANE), lambda i, j: (i, tile_of(j), (SSD_XBC + GROUP_W) // LANE)),
            pl.BlockSpec((HALO, SSD_XBC), lambda i, j: (0, 0)),
            pl.BlockSpec((1, SSD_XBC), lambda i, j: (0, 0)),
            pl.BlockSpec((1, LANE), lambda i, j: (0, 0)),
            pl.BlockSpec((1, LANE, GROUP_W), lambda i, j: (j // nt, 0, 0)),
            pl.BlockSpec((1, 1, GROUP_W), lambda i, j: (j // nt, 0, 0)),
            pl.BlockSpec((1, GROUP_W), lambda i, j: (0, 0)),
            pl.BlockSpec((1, GROUP_W), lambda i, j: (0, 0)),
        ],
        out_specs=pl.BlockSpec((1, tc, GROUP_W), lambda i, j: (i, jnp.where(j < nt, nt - 1, 2 * nt - 1 - j), 0)),
        out_shape=jax.ShapeDtypeStruct((b, t, GROUP_W), F32),
        scratch_shapes=[pltpu.VMEM((t, GROUP_W), F32),
                        pltpu.VMEM((t, GROUP_W), F32),
                        pltpu.VMEM((t, 2 * SSD_GROUPS * SSD_STATE), BF16),
                        pltpu.VMEM((SSD_STATE, GROUP_W), F32),
                        pltpu.VMEM((tc + 2 * HALO, SSD_XBC), F32)],
        compiler_params=_params(dimension_semantics=("arbitrary", "arbitrary")),
        name="ssd",
    )(ssd_g, ssd_g, ssd_g, ssd_g, ssd_g, cw, cb, dtb, expand, a_exp, d_exp, g)


GLA_QK_W = GLA_HEADS * GLA_DK
GLA_V_W = GLA_HEADS * GLA_DV


def _gla_kernel(q_ref, k_ref, v_ref, r_ref, lr_ref, w2_ref, gb_ref, g_ref, avg_ref,
                out_ref, of_ref, s_ref, *, nt, tc):
    j = pl.program_id(1)
    tile = jnp.where(j < nt, j, 2 * nt - 1 - j)

    @pl.when((j == 0) | (j == nt))
    def _():
        s_ref[...] = jnp.zeros_like(s_ref)

    base = pl.multiple_of(tile * tc, tc)
    nc = tc // CHUNK
    logit = _dot(lr_ref[0].astype(BF16), w2_ref[0]) + gb_ref[0]
    logg = -_softplus(-logit) * (LOG2E / GLA_TAU)
    q = q_ref[0] * (GLA_DK ** -0.5)
    k = k_ref[0]
    v = v_ref[0]
    qk_head = lax.broadcasted_iota(jnp.int32, (1, GLA_QK_W), 1) // GLA_DK
    v_head = lax.broadcasted_iota(jnp.int32, (1, GLA_V_W), 1) // GLA_DV
    blockdiag = (lax.broadcasted_iota(jnp.int32, (GLA_QK_W, GLA_V_W), 0) // GLA_DK
                 == lax.broadcasted_iota(jnp.int32, (GLA_QK_W, GLA_V_W), 1) // GLA_DV)

    def scan(reverse):
        tri = _tri(reverse)
        tri_bf = tri.astype(BF16)
        last = 0 if reverse else CHUNK - 1
        chunks = range(nc)
        sls = [slice(c * CHUNK, (c + 1) * CHUNK) for c in chunks]
        gcs = [_split_dot(tri_bf, logg[sl]) for sl in sls]
        tot = [g[last:last + 1] for g in gcs]
        dec_col = [jnp.exp2(g.T[:, last:last + 1]) for g in gcs]
        qg = [q[sl] * jnp.exp2(g) for sl, g in zip(sls, gcs)]
        kg = [(k[sl] * jnp.exp2(-g)).astype(BF16) for sl, g in zip(sls, gcs)]
        kd = [(k[sl] * jnp.exp2(t - g)).astype(BF16) for sl, g, t in zip(sls, gcs, tot)]
        raw = [[_dot_nt(jnp.where(qk_head == hd, qg[c], 0.0).astype(BF16), kg[c]) for hd in range(GLA_HEADS)]
               for c in chunks]
        st = [jnp.where(blockdiag, _dot_tn(kd[c], v[sls[c]].astype(BF16)), 0.0) for c in chunks]
        att = [[jnp.where(tri, raw[c][hd], 0.0).astype(BF16) for hd in range(GLA_HEADS)] for c in chunks]
        o = []
        for c in chunks:
            acc = None
            for hd in range(GLA_HEADS):
                part = _dot(att[c][hd], jnp.where(v_head == hd, v[sls[c]], 0.0).astype(BF16))
                acc = part if acc is None else acc + part
            o.append(acc)
        s = s_ref[...]
        s_in = [None] * nc
        for c in (reversed(chunks) if reverse else chunks):
            s_in[c] = s
            s = s * dec_col[c] + st[c]
        s_ref[...] = s
        return [o[c] + _dot(qg[c].astype(BF16), s_in[c].astype(BF16)) for c in chunks]

    @pl.when(j < nt)
    def _():
        for c, o in enumerate(scan(False)):
            of_ref[pl.ds(base + c * CHUNK, CHUNK), :] = o

    @pl.when(j >= nt)
    def _():
        for c, ob in enumerate(scan(True)):
            sl = slice(c * CHUNK, (c + 1) * CHUNK)
            o = of_ref[pl.ds(base + c * CHUNK, CHUNK), :] + ob
            ms = _split_dot_right(o * o, avg_ref[...])
            out_ref[0, sl, :] = o * lax.rsqrt(ms + EPS) * g_ref[...] * _silu(r_ref[0, sl, :])


def _split_dot_right(x, m_bf16):
    x1 = x.astype(BF16)
    r1 = x - x1.astype(F32)
    x2 = r1.astype(BF16)
    x3 = (r1 - x2.astype(F32)).astype(BF16)
    return _dot(x1, m_bf16) + _dot(x2, m_bf16) + _dot(x3, m_bf16)


def _gla(gla_g, w2p, gb, g, avg, tc):
    b, t, _ = gla_g.shape
    nt = t // tc

    def tile_of(j):
        return jnp.where(j < nt, j, 2 * nt - 1 - j)

    kern = functools.partial(_gla_kernel, nt=nt, tc=tc)
    return pl.pallas_call(
        kern,
        grid=(b, 2 * nt),
        in_specs=[
            pl.BlockSpec((1, tc, GLA_QK_W), lambda i, j: (i, tile_of(j), 0)),
            pl.BlockSpec((1, tc, GLA_QK_W), lambda i, j: (i, tile_of(j), 1)),
            pl.BlockSpec((1, tc, GLA_V_W), lambda i, j: (i, tile_of(j), 1)),
            pl.BlockSpec((1, tc, GLA_V_W), lambda i, j: (i, tile_of(j), 2)),
            pl.BlockSpec((1, tc, LANE), lambda i, j: (i, tile_of(j), (2 * GLA_QK_W + 2 * GLA_V_W) // LANE)),
            pl.BlockSpec((1, LANE, GLA_QK_W), lambda i, j: (j // nt, 0, 0)),
            pl.BlockSpec((1, 1, GLA_QK_W), lambda i, j: (j // nt, 0, 0)),
            pl.BlockSpec((1, GLA_V_W), lambda i, j: (0, 0)),
            pl.BlockSpec((GLA_V_W, GLA_V_W), lambda i, j: (0, 0)),
        ],
        out_specs=pl.BlockSpec((1, tc, GLA_V_W), lambda i, j: (i, jnp.where(j < nt, nt - 1, 2 * nt - 1 - j), 0)),
        out_shape=jax.ShapeDtypeStruct((b, t, GLA_V_W), F32),
        scratch_shapes=[pltpu.VMEM((t, GLA_V_W), F32),
                        pltpu.VMEM((GLA_QK_W, GLA_V_W), F32)],
        compiler_params=_params(dimension_semantics=("arbitrary", "arbitrary")),
        name="gla",
    )(gla_g, gla_g, gla_g, gla_g, gla_g, w2p, gb, g, avg)


def _rope(y, cos, sin, half):
    lane = lax.broadcasted_iota(jnp.int32, (1, LANE), 1)
    lo = (lane % (2 * half)) < half
    rot = jnp.where(lo, pltpu.roll(y, LANE - half, 1), pltpu.roll(y, half, 1))
    return y * cos + rot * sin


ONES_LANE = LANE - 1
VT_ROWS = 80
SCORE_BOUND_LIMIT = 48.0


def _finish_keys(k):
    kf = k.astype(BF16).astype(F32)
    norm = jnp.sqrt(jnp.max(jnp.sum(kf * kf, axis=-1, keepdims=True), axis=0, keepdims=True))
    lane = lax.broadcasted_iota(jnp.int32, (1, LANE), 1)
    return jnp.where(lane == ONES_LANE, 1.0, k).astype(BF16), norm


def _update_kmax(kmax_ref, norms):
    @pl.when(pl.program_id(1) == 0)
    def _():
        kmax_ref[...] = jnp.zeros_like(kmax_ref)

    lane = lax.broadcasted_iota(jnp.int32, (1, LANE), 1)
    upd = jnp.zeros((1, LANE), F32)
    for g, norm in enumerate(norms):
        upd = jnp.where(lane == g, norm, upd)
    kmax_ref[0] = jnp.maximum(kmax_ref[0], upd)


def _vt_rows(v, heads, dv):
    tm = v.shape[0]
    vt = v.T.reshape(heads, dv, tm)
    row = lax.broadcasted_iota(jnp.int32, (heads, VT_ROWS - dv, tm), 1)
    return jnp.concatenate([vt, jnp.where(row == 0, 1.0, 0.0)], axis=1).astype(BF16)


def _attn_kernel(q_ref, k_ref, vt_ref, kmax_ref, o_ref, qa_ref, m_ref, acc_ref, *, n_q, rep, tk, dv):
    nk = k_ref.shape[1] // tk
    lane = lax.broadcasted_iota(jnp.int32, (1, LANE), 1)
    kmax = kmax_ref[0, 0:1, :]
    bound_max = None
    for h in range(n_q):
        qf = q_ref[0, :, h * LANE:(h + 1) * LANE].astype(F32)
        kg = jnp.max(jnp.where(lane == h // rep, kmax, 0.0), axis=-1, keepdims=True)
        bound = jnp.sqrt(jnp.sum(qf * qf, axis=-1, keepdims=True)) * kg
        qa_ref[h] = jnp.where(lane == ONES_LANE, -bound, qf).T.astype(BF16)
        top = jnp.max(bound)
        bound_max = top if bound_max is None else jnp.maximum(bound_max, top)
    acc_ref[...] = jnp.zeros(acc_ref.shape, F32)

    @pl.when(bound_max <= SCORE_BOUND_LIMIT)
    def _():
        def scores(i, h):
            ks = pl.multiple_of(i * tk, tk)
            g = h // rep
            return _dot(k_ref[0, pl.ds(ks, tk), g * LANE:(g + 1) * LANE], qa_ref[h])

        group = max(u for u in (1, 2, 4) if nk % u == 0)

        def body(ii, carry):
            steps = [(ii * group + u, h) for u in range(group) for h in range(n_q)]
            s_next = scores(*steps[0])
            for n, (i, h) in enumerate(steps):
                s = s_next
                if n + 1 < len(steps):
                    s_next = scores(*steps[n + 1])
                p = jnp.exp2(s)
                acc_ref[h, 0:dv, :] += _dot(vt_ref[0, h // rep, i, 0:dv, :], p.astype(BF16))
                acc_ref[h, dv:dv + 8, :] += jnp.sum(p.reshape(tk // 8, 8, p.shape[1]), axis=0)
            return carry

        lax.fori_loop(0, nk // group, body, 0)

    @pl.when(bound_max > SCORE_BOUND_LIMIT)
    def _():
        m_ref[...] = jnp.full(m_ref.shape, -jnp.inf, F32)

        def body(i, carry):
            ks = pl.multiple_of(i * tk, tk)
            for h in range(n_q):
                g = h // rep
                s = _dot_nt(k_ref[0, pl.ds(ks, tk), g * LANE:(g + 1) * LANE], q_ref[0, :, h * LANE:(h + 1) * LANE])
                m_old = m_ref[h]
                m_new = jnp.maximum(m_old, jnp.max(s, axis=0, keepdims=True))
                p = jnp.exp2(s - m_new)
                acc_ref[h] = acc_ref[h] * jnp.exp2(m_old - m_new) + _dot(vt_ref[0, g, i], p.astype(BF16))
                m_ref[h] = m_new
            return carry

        lax.fori_loop(0, nk, body, 0)

    o = jnp.concatenate([acc_ref[h, 0:dv, :] * (1.0 / jnp.sum(acc_ref[h, dv:dv + 8, :], axis=0, keepdims=True))
                         for h in range(n_q)], axis=0)
    o_ref[0] = o.T


def _attention(q, k, vt, kmax, rep, tq, tk, dv):
    b, t, qw = q.shape
    kw = k.shape[2]
    n_q = qw // LANE
    n_kv = n_q // rep
    ow = n_q * dv
    kern = functools.partial(_attn_kernel, n_q=n_q, rep=rep, tk=tk, dv=dv)
    return pl.pallas_call(
        kern,
        grid=(b, t // tq),
        in_specs=[
            pl.BlockSpec((1, tq, qw), lambda i, j: (i, j, 0)),
            pl.BlockSpec((1, t, kw), lambda i, j: (i, 0, 0)),
            pl.BlockSpec((1, n_kv, t // tk, VT_ROWS, tk), lambda i, j: (i, 0, 0, 0, 0)),
            pl.BlockSpec((1, 8, LANE), lambda i, j: (i, 0, 0)),
        ],
        out_specs=pl.BlockSpec((1, tq, ow), lambda i, j: (i, j, 0)),
        out_shape=jax.ShapeDtypeStruct((b, t, ow), F32),
        scratch_shapes=[pltpu.VMEM((n_q, LANE, tq), BF16), pltpu.VMEM((n_q, 1, tq), F32),
                        pltpu.VMEM((n_q, VT_ROWS, tq), F32)],
        compiler_params=_params(dimension_semantics=("arbitrary", "arbitrary")),
        name="attention",
    )(q, k, vt, kmax)


FF_CHUNK = 1024


def _outmlp_kernel(x_ref, m0_ref, m1_ref, m2_ref, m3_ref, wo_ref, g2_ref, w1_ref, w2_ref, gf_ref, o_ref, *, final):
    x1 = x_ref[...]
    for i, m_ref in enumerate((m0_ref, m1_ref, m2_ref, m3_ref)):
        x1 = x1 + _dot(m_ref[...].astype(BF16), wo_ref[i * GROUP_W:(i + 1) * GROUP_W, :])
    ms = jnp.mean(x1 * x1, axis=-1, keepdims=True)
    h = (x1 * lax.rsqrt(ms + EPS) * g2_ref[...]).astype(BF16)
    acc = None
    for c in range(D_FF // FF_CHUNK):
        sl = slice(c * FF_CHUNK, (c + 1) * FF_CHUNK)
        u = jnp.maximum(_dot(h, w1_ref[:, sl]), 0.0)
        part = _dot((u * u).astype(BF16), w2_ref[sl, :])
        acc = part if acc is None else acc + part
    y = x1 + acc
    if final:
        ms = jnp.mean(y * y, axis=-1, keepdims=True)
        y = y * lax.rsqrt(ms + EPS) * gf_ref[...]
    o_ref[...] = y


def _outmlp(x2d, mixes, wo, g2, w1, w2, gf, final, tm):
    n = x2d.shape[0]
    const = lambda i: (0, 0)
    single = pl.Buffered(1)
    kern = functools.partial(_outmlp_kernel, final=final)
    return pl.pallas_call(
        kern,
        grid=(n // tm,),
        in_specs=[pl.BlockSpec((tm, D_MODEL), lambda i: (i, 0))]
                 + [pl.BlockSpec((tm, GROUP_W), lambda i: (i, 0)) for _ in range(4)]
                 + [pl.BlockSpec((D_MODEL, D_MODEL), const, pipeline_mode=single),
                    pl.BlockSpec((1, D_MODEL), const),
                    pl.BlockSpec((D_MODEL, D_FF), const, pipeline_mode=single),
                    pl.BlockSpec((D_FF, D_MODEL), const, pipeline_mode=single),
                    pl.BlockSpec((1, D_MODEL), const)],
        out_specs=pl.BlockSpec((tm, D_MODEL), lambda i: (i, 0)),
        out_shape=jax.ShapeDtypeStruct((n, D_MODEL), F32),
        compiler_params=_params(dimension_semantics=("arbitrary",)),
        name="outmlp",
    )(x2d, *mixes, wo, g2.reshape(1, D_MODEL), w1, w2, gf.reshape(1, D_MODEL))


def _rope_tables(t):
    pos = jnp.arange(t, dtype=jnp.int32)
    row = (pos // GRID_W).astype(F32)
    col = (pos % GRID_W).astype(F32)

    def block(p, n):
        inv_freq = ROPE_THETA ** (-jnp.arange(n, dtype=F32) / n)
        ang = p[:, None] * inv_freq[None, :]
        c, s = jnp.cos(ang), jnp.sin(ang)
        return jnp.concatenate([c, c], axis=1), jnp.concatenate([-s, s], axis=1)

    def table(n, lead):
        cr, sr = block(row, n)
        cc, sc = block(col, n)
        tail = LANE - lead - 4 * n
        cos = jnp.concatenate([jnp.ones((t, lead), F32), cr, cc, jnp.ones((t, tail), F32)], axis=1)
        sin = jnp.concatenate([jnp.zeros((t, lead), F32), sr, sc, jnp.zeros((t, tail), F32)], axis=1)
        return cos, sin

    return table(GQA_HEAD_DIM // 4, 0), table(MLA_ROPE // 4, MLA_NOPE)


def _pad_lanes(v, width=LANE):
    return jnp.concatenate([v, jnp.zeros((width - v.shape[0],), v.dtype)]).reshape(1, width)


def _layer_params(i, p):
    (norm1_g, w_in, ssd_conv_w, ssd_conv_b, ssd_dt_bias, ssd_a_log, ssd_d, ssd_norm_g,
     gqa_q_norm_g, gqa_k_norm_g, gla_gate_w2, gla_gate_b, gla_norm_g, mla_q_norm_g, mla_w_uq,
     mla_kv_norm_g, mla_w_ukv, w_out, norm2_g, w_ff1, w_ff2) = [a[i] for a in p]
    out = {}
    out["norm1_g"] = norm1_g
    out["w_all"] = _gather_cols(w_in, _IN_COLS).astype(BF16)
    out["conv_w"] = jnp.concatenate([ssd_conv_w, jnp.zeros((HALO - SSD_CONV_W, SSD_XBC), F32)], axis=0)
    out["conv_b"] = ssd_conv_b.reshape(1, SSD_XBC)
    expand = lambda a: jnp.repeat(a, SSD_HEAD_DIM, axis=-1)
    out["dt_bias"] = _pad_lanes(ssd_dt_bias.reshape(2 * SSD_HEADS))
    spread = np.zeros((2, LANE, GROUP_W), np.float32)
    for d in range(2):
        for h in range(SSD_HEADS):
            spread[d, d * SSD_HEADS + h, h * SSD_HEAD_DIM:(h + 1) * SSD_HEAD_DIM] = 1.0
    out["dt_spread"] = jnp.asarray(spread, BF16)
    out["a_neg"] = expand(-jnp.exp(ssd_a_log) * LOG2E).reshape(2, 1, GROUP_W)
    out["ssd_d"] = expand(ssd_d).reshape(1, GROUP_W)
    out["ssd_norm_g"] = ssd_norm_g.reshape(1, GROUP_W)
    out["gq"] = _pad_lanes(gqa_q_norm_g)
    out["gk"] = _pad_lanes(gqa_k_norm_g)
    w2p = jnp.zeros((2, LANE, GLA_QK_W), F32)
    for d in range(2):
        w2p = w2p.at[d, d * GLA_LOWRANK:(d + 1) * GLA_LOWRANK, :].set(gla_gate_w2[d])
    out["gla_w2"] = w2p.astype(BF16)
    out["gla_b"] = gla_gate_b.reshape(2, 1, GLA_QK_W)
    out["gla_norm_g"] = jnp.tile(gla_norm_g, GLA_HEADS).reshape(1, GLA_V_W)
    out["mla_gq"] = mla_q_norm_g.reshape(1, MLA_Q_LORA)
    out["mla_gkv"] = mla_kv_norm_g.reshape(1, MLA_KV_LORA)
    uq_cols, uk_cols, uv_cols = [], [], []
    for h in range(MLA_HEADS):
        uq_cols += list(range(h * MLA_QK, (h + 1) * MLA_QK)) + [-1] * (LANE - MLA_QK)
        base = h * (MLA_NOPE + MLA_V)
        uk_cols += list(range(base, base + MLA_NOPE)) + [-1] * (LANE - MLA_NOPE)
        uv_cols += list(range(base + MLA_NOPE, base + MLA_NOPE + MLA_V))
    out["wuq"] = _gather_cols(mla_w_uq, np.asarray(uq_cols, np.int32)).astype(BF16)
    out["wuk"] = _gather_cols(mla_w_ukv, np.asarray(uk_cols, np.int32)).astype(BF16)
    out["wuv"] = _gather_cols(mla_w_ukv, np.asarray(uv_cols, np.int32)).astype(BF16)
    out["w_out"] = w_out.astype(BF16)
    out["norm2_g"] = norm2_g
    out["w_ff1"] = w_ff1.astype(BF16)
    out["w_ff2"] = w_ff2.astype(BF16)
    return out


TILE_PREF = dict(tc=1024, tp=512, tq_gqa=512, tq_mla=512, tmlp=512)


def _tiles(t):
    pick = lambda pref: max(c for c in (128, 256, 512, 1024) if c <= pref and t % c == 0)
    return {name: pick(pref) for name, pref in TILE_PREF.items()}


def _trunk(x, layers, final_norm_g):
    b, t, d = x.shape
    n = b * t
    ts = _tiles(t)
    tk = ts["tp"]
    tables = _rope_tables(t)
    avg = jnp.asarray(np.kron(np.eye(GLA_HEADS), np.full((GLA_DV, GLA_DV), 1.0 / GLA_DV)), BF16)
    x2d = x.reshape(n, d)
    for i, lp in enumerate(layers):
        ssd_g, gla_g, gq, gk, gvt, gkmax, mq, mk, mvt, mkmax = _inproj(x2d.reshape(b, t, d), lp, tables, tk)
        ssd_out = _ssd(ssd_g, lp["conv_w"], lp["conv_b"], lp["dt_bias"], lp["dt_spread"], lp["a_neg"],
                       lp["ssd_d"], lp["ssd_norm_g"], ts["tc"])
        gqa_out = _attention(gq, gk, gvt, gkmax, GQA_HEADS // GQA_KV_HEADS, ts["tq_gqa"], tk, GQA_HEAD_DIM)
        gla_out = _gla(gla_g, lp["gla_w2"], lp["gla_b"], lp["gla_norm_g"], avg, ts["tc"])
        mla_out = _attention(mq, mk, mvt, mkmax, 1, ts["tq_mla"], tk, MLA_V)
        mixes = [m.reshape(n, GROUP_W) for m in (ssd_out, gqa_out, gla_out, mla_out)]
        x2d = _outmlp(x2d, mixes, lp["w_out"], lp["norm2_g"], lp["w_ff1"], lp["w_ff2"], final_norm_g,
                      i == len(layers) - 1, ts["tmlp"])
    return x2d.reshape(b, t, d)


def kernel(x_prompt, x_sample, norm1_g, w_in, ssd_conv_w, ssd_conv_b, ssd_dt_bias, ssd_a_log, ssd_d, ssd_norm_g,
           gqa_q_norm_g, gqa_k_norm_g, gla_gate_w2, gla_gate_b, gla_norm_g, mla_q_norm_g, mla_w_uq,
           mla_kv_norm_g, mla_w_ukv, w_out, norm2_g, w_ff1, w_ff2, final_norm_g):
    stacked = (norm1_g, w_in, ssd_conv_w, ssd_conv_b, ssd_dt_bias, ssd_a_log, ssd_d, ssd_norm_g,
               gqa_q_norm_g, gqa_k_norm_g, gla_gate_w2, gla_gate_b, gla_norm_g, mla_q_norm_g, mla_w_uq,
               mla_kv_norm_g, mla_w_ukv, w_out, norm2_g, w_ff1, w_ff2)
    layers = [_layer_params(i, stacked) for i in range(norm1_g.shape[0])]
    return (_trunk(x_prompt, layers, final_norm_g), _trunk(x_sample, layers, final_norm_g))
```

```python
import functools

import numpy as np
import jax
import jax.numpy as jnp
from jax import lax
from jax.experimental import pallas as pl
from jax.experimental.pallas import tpu as pltpu

F32 = jnp.float32
BF16 = jnp.bfloat16

D_MODEL = 1024
DEPTH = 2
GRID_W = 64
CHUNK = 128
ROPE_THETA = 10000.0
EPS = 1e-6
GROUP_W = D_MODEL // 4
D_FF = 4 * D_MODEL

SSD_HEADS = 4
SSD_HEAD_DIM = GROUP_W // SSD_HEADS
SSD_GROUPS = 2
SSD_STATE = 128
SSD_CONV_W = 5
SSD_XBC = GROUP_W + 2 * SSD_GROUPS * SSD_STATE

GQA_HEADS = 4
GQA_KV_HEADS = 2
GQA_HEAD_DIM = GROUP_W // GQA_HEADS

GLA_HEADS = 4
GLA_DV = GROUP_W // GLA_HEADS
GLA_DK = GLA_DV // 2
GLA_LOWRANK = 16
GLA_TAU = 16.0

MLA_HEADS = 4
MLA_Q_LORA = 256
MLA_KV_LORA = 128
MLA_NOPE = 64
MLA_ROPE = 32
MLA_V = GROUP_W // MLA_HEADS
MLA_QK = MLA_NOPE + MLA_ROPE

IN_SIZES = (GROUP_W, SSD_XBC, 2 * SSD_HEADS,
            GQA_HEADS * GQA_HEAD_DIM, GQA_KV_HEADS * GQA_HEAD_DIM, GQA_KV_HEADS * GQA_HEAD_DIM,
            GLA_HEADS * GLA_DK, GLA_HEADS * GLA_DK, GLA_HEADS * GLA_DV, GLA_HEADS * GLA_DV, 2 * GLA_LOWRANK,
            MLA_Q_LORA, MLA_KV_LORA, MLA_ROPE)
IN_OFFS = tuple(int(v) for v in np.concatenate([[0], np.cumsum(IN_SIZES)]))

LOG2E = 1.4426950408889634
LANE = 128
HALO = 8
VMEM_LIMIT = 52 * 1024 * 1024

SSD_W = SSD_XBC + GROUP_W + LANE
GQA_W = GQA_HEADS * LANE + GQA_KV_HEADS * LANE + GQA_KV_HEADS * GQA_HEAD_DIM
GLA_W = 2 * GLA_HEADS * GLA_DK + 2 * GLA_HEADS * GLA_DV + LANE
MLA_W = MLA_Q_LORA + MLA_KV_LORA + LANE
PROJ_W = SSD_W + GQA_W + GLA_W + MLA_W


def _in_proj_columns():
    o = IN_OFFS
    cols = []
    cols += list(range(o[1], o[2]))
    cols += list(range(o[0], o[1]))
    cols += list(range(o[2], o[3])) + [-1] * (LANE - 2 * SSD_HEADS)
    for h in range(GQA_HEADS):
        cols += list(range(o[3] + h * GQA_HEAD_DIM, o[3] + (h + 1) * GQA_HEAD_DIM)) + [-1] * (LANE - GQA_HEAD_DIM)
    for h in range(GQA_KV_HEADS):
        cols += list(range(o[4] + h * GQA_HEAD_DIM, o[4] + (h + 1) * GQA_HEAD_DIM)) + [-1] * (LANE - GQA_HEAD_DIM)
    cols += list(range(o[5], o[6]))
    cols += list(range(o[6], o[10]))
    cols += list(range(o[10], o[11])) + [-1] * (LANE - 2 * GLA_LOWRANK)
    cols += list(range(o[11], o[13]))
    cols += [-1] * MLA_NOPE + list(range(o[13], o[14])) + [-1] * (LANE - MLA_QK)
    cols = np.asarray(cols, np.int32)
    assert cols.shape[0] == PROJ_W
    return cols


_IN_COLS = _in_proj_columns()


def _gather_cols(w, cols):
    picked = jnp.take(w, jnp.asarray(np.maximum(cols, 0)), axis=1)
    return jnp.where(jnp.asarray(cols >= 0)[None, :], picked, 0.0)


def _params(**kw):
    return pltpu.CompilerParams(vmem_limit_bytes=VMEM_LIMIT, **kw)


def _silu(x):
    return x * (1.0 / (1.0 + jnp.exp(-x)))


def _softplus(x):
    return jnp.maximum(x, 0.0) + jnp.log1p(jnp.exp(-jnp.abs(x)))


def _dot(a, b):
    return jnp.dot(a, b, preferred_element_type=F32)


def _dot_nt(a, b):
    return lax.dot_general(a, b, (((1,), (1,)), ((), ())), preferred_element_type=F32)


def _dot_tn(a, b):
    return lax.dot_general(a, b, (((0,), (0,)), ((), ())), preferred_element_type=F32)


def _split_dot(m_bf16, x):
    x1 = x.astype(BF16)
    r1 = x - x1.astype(F32)
    x2 = r1.astype(BF16)
    x3 = (r1 - x2.astype(F32)).astype(BF16)
    return _dot(m_bf16, x1) + _dot(m_bf16, x2) + _dot(m_bf16, x3)


def _tri(reverse):
    li = lax.broadcasted_iota(jnp.int32, (CHUNK, CHUNK), 0)
    si = lax.broadcasted_iota(jnp.int32, (CHUNK, CHUNK), 1)
    return (si >= li) if reverse else (si <= li)


def _inproj_kernel(x_ref, g_ref, w_ref, cosg_ref, sing_ref, cosm_ref, sinm_ref, gq_ref, gk_ref,
                   mgq_ref, mgkv_ref, wuq_ref, wuk_ref, wuv_ref,
                   ssd_ref, gla_ref, gq_out, gk_out, gvt_out, gkmax_out, mq_out, mk_out, mvt_out, mkmax_out):
    x = x_ref[0]
    ms = jnp.mean(x * x, axis=-1, keepdims=True)
    h = (x * lax.rsqrt(ms + EPS) * g_ref[...]).astype(BF16)
    c_gqa, c_gla, c_mla = SSD_W, SSD_W + GQA_W, SSD_W + GQA_W + GLA_W
    mla = _dot(h, w_ref[:, c_mla:PROJ_W])
    gqa = _dot(h, w_ref[:, c_gqa:c_gla])

    def rms(y, g):
        ms_y = jnp.mean(y * y, axis=-1, keepdims=True)
        return (y * lax.rsqrt(ms_y + EPS) * g).astype(BF16)

    cq = rms(mla[:, :MLA_Q_LORA], mgq_ref[...])
    ckv = rms(mla[:, MLA_Q_LORA:MLA_Q_LORA + MLA_KV_LORA], mgkv_ref[...])
    ssd_ref[0] = _dot(h, w_ref[:, 0:SSD_W])
    q = _dot(cq, wuq_ref[...])
    kn = _dot(ckv, wuk_ref[...])
    v = _dot(ckv, wuv_ref[...])

    cos, sin = cosg_ref[...], sing_ref[...]
    qw, kw = GQA_HEADS * LANE, GQA_KV_HEADS * LANE
    heads = [gqa[:, hd * LANE:(hd + 1) * LANE] for hd in range(GQA_HEADS + GQA_KV_HEADS)]
    gains = [gq_ref[...]] * GQA_HEADS + [gk_ref[...]] * GQA_KV_HEADS
    inv = [lax.rsqrt(jnp.sum(y * y, axis=-1, keepdims=True) * (1.0 / GQA_HEAD_DIM) + EPS) for y in heads]
    roped = [_rope(y * r * g, cos, sin, GQA_HEAD_DIM // 4) for y, r, g in zip(heads, inv, gains)]
    for hd in range(GQA_HEADS):
        gq_out[0, :, hd * LANE:(hd + 1) * LANE] = (roped[hd] * (GQA_HEAD_DIM ** -0.5 * LOG2E)).astype(BF16)
    norms = []
    for hd in range(GQA_KV_HEADS):
        gk_out[0, :, hd * LANE:(hd + 1) * LANE], norm = _finish_keys(roped[GQA_HEADS + hd])
        norms.append(norm)
    _update_kmax(gkmax_out, norms)
    gvt_out[0, :, 0] = _vt_rows(gqa[:, qw + kw:], GQA_KV_HEADS, GQA_HEAD_DIM)
    gla_ref[0] = _dot(h, w_ref[:, c_gla:c_mla])

    cos, sin = cosm_ref[...], sinm_ref[...]
    k_rope = _rope(mla[:, MLA_Q_LORA + MLA_KV_LORA:], cos, sin, MLA_ROPE // 4)
    q_roped = [_rope(q[:, hd * LANE:(hd + 1) * LANE], cos, sin, MLA_ROPE // 4) for hd in range(MLA_HEADS)]
    norms = []
    for hd in range(MLA_HEADS):
        sl = slice(hd * LANE, (hd + 1) * LANE)
        mq_out[0, :, sl] = (q_roped[hd] * (MLA_QK ** -0.5 * LOG2E)).astype(BF16)
        mk_out[0, :, sl], norm = _finish_keys(kn[:, sl] + k_rope)
        norms.append(norm)
    _update_kmax(mkmax_out, norms)
    mvt_out[0, :, 0] = _vt_rows(v, MLA_HEADS, MLA_V)


def _inproj(x, lp, tables, tm):
    b, t, _ = x.shape
    (cos_g, sin_g), (cos_m, sin_m) = tables
    const = lambda i, j: (0, 0)
    tok = lambda i, j: (i, j, 0)
    tab = pl.BlockSpec((tm, LANE), lambda i, j: (j, 0))
    qw, kw, mw = GQA_HEADS * LANE, GQA_KV_HEADS * LANE, MLA_HEADS * LANE

    def vt_spec(heads):
        return pl.BlockSpec((1, heads, 1, VT_ROWS, tm), lambda i, j: (i, 0, j, 0, 0))

    kmax_spec = pl.BlockSpec((1, 8, LANE), lambda i, j: (i, 0, 0))
    return pl.pallas_call(
        _inproj_kernel,
        grid=(b, t // tm),
        in_specs=[pl.BlockSpec((1, tm, D_MODEL), tok),
                  pl.BlockSpec((1, D_MODEL), const),
                  pl.BlockSpec((D_MODEL, PROJ_W), const, pipeline_mode=pl.Buffered(1)),
                  tab, tab, tab, tab,
                  pl.BlockSpec((1, LANE), const), pl.BlockSpec((1, LANE), const),
                  pl.BlockSpec((1, MLA_Q_LORA), const), pl.BlockSpec((1, MLA_KV_LORA), const),
                  pl.BlockSpec((MLA_Q_LORA, mw), const), pl.BlockSpec((MLA_KV_LORA, mw), const),
                  pl.BlockSpec((MLA_KV_LORA, MLA_HEADS * MLA_V), const)],
        out_specs=[pl.BlockSpec((1, tm, SSD_W), tok), pl.BlockSpec((1, tm, GLA_W), tok),
                   pl.BlockSpec((1, tm, qw), tok), pl.BlockSpec((1, tm, kw), tok), vt_spec(GQA_KV_HEADS), kmax_spec,
                   pl.BlockSpec((1, tm, mw), tok), pl.BlockSpec((1, tm, mw), tok), vt_spec(MLA_HEADS), kmax_spec],
        out_shape=[jax.ShapeDtypeStruct((b, t, SSD_W), F32), jax.ShapeDtypeStruct((b, t, GLA_W), F32),
                   jax.ShapeDtypeStruct((b, t, qw), BF16), jax.ShapeDtypeStruct((b, t, kw), BF16),
                   jax.ShapeDtypeStruct((b, GQA_KV_HEADS, t // tm, VT_ROWS, tm), BF16),
                   jax.ShapeDtypeStruct((b, 8, LANE), F32),
                   jax.ShapeDtypeStruct((b, t, mw), BF16), jax.ShapeDtypeStruct((b, t, mw), BF16),
                   jax.ShapeDtypeStruct((b, MLA_HEADS, t // tm, VT_ROWS, tm), BF16),
                   jax.ShapeDtypeStruct((b, 8, LANE), F32)],
        compiler_params=_params(dimension_semantics=("arbitrary", "arbitrary")),
        name="inproj",
    )(x, lp["norm1_g"].reshape(1, D_MODEL), lp["w_all"], cos_g, sin_g, cos_m, sin_m, lp["gq"], lp["gk"],
      lp["mla_gq"], lp["mla_gkv"], lp["wuq"], lp["wuk"], lp["wuv"])


def _ssd_kernel(xbc_ref, prev_ref, next_ref, z_ref, dtx_ref, cw_ref, cb_ref, dtb_ref, ex_ref, a_ref, d_ref, g_ref,
                out_ref, yf_ref, xs_ref, bc_ref, h_ref, xe_ref, *, nt, tc):
    j = pl.program_id(1)
    tile = jnp.where(j < nt, j, 2 * nt - 1 - j)
    base = pl.multiple_of(tile * tc, tc)
    nc = tc // CHUNK
    half = GROUP_W // SSD_GROUPS

    @pl.when((j == 0) | (j == nt))
    def _():
        h_ref[...] = jnp.zeros_like(h_ref)

    @pl.when(j < nt)
    def _():
        xe_ref[HALO:HALO + tc, :] = xbc_ref[0]
        xe_ref[0:HALO, :] = jnp.where(tile > 0, prev_ref[0], 0.0)
        xe_ref[HALO + tc:2 * HALO + tc, :] = jnp.where(tile < nt - 1, next_ref[0], 0.0)
        pad = SSD_CONV_W // 2
        acc = cb_ref[...] + cw_ref[0:1, :] * xe_ref[HALO - pad:HALO - pad + tc, :]
        for w in range(1, SSD_CONV_W):
            acc = acc + cw_ref[w:w + 1, :] * xe_ref[HALO - pad + w:HALO - pad + w + tc, :]
        xbc = _silu(acc)
        xs_ref[pl.ds(base, tc), :] = xbc[:, :GROUP_W]
        bc_ref[pl.ds(base, tc), :] = xbc[:, GROUP_W:].astype(BF16)

    xs = xs_ref[pl.ds(base, tc), :]
    dt_heads = _softplus(dtx_ref[0] + dtb_ref[...])
    dt = _split_dot_right(dt_heads, ex_ref[0])
    a_dt = dt * a_ref[0]
    xdt = xs * dt
    lane_head = lax.broadcasted_iota(jnp.int32, (1, GROUP_W), 1) // SSD_HEAD_DIM

    def scan(reverse):
        tri = _tri(reverse)
        tri_bf = tri.astype(BF16)
        chunks = range(nc)
        groups = range(SSD_GROUPS)
        per_group = SSD_HEADS // SSD_GROUPS
        sls = [slice(c * CHUNK, (c + 1) * CHUNK) for c in chunks]
        rows = [pl.ds(base + c * CHUNK, CHUNK) for c in chunks]
        acs = [_split_dot(tri_bf, a_dt[sl]) for sl in sls]
        b_c = [[bc_ref[rows[c], g * SSD_STATE:(g + 1) * SSD_STATE] for g in groups] for c in chunks]
        c_c = [[bc_ref[rows[c], (SSD_GROUPS + g) * SSD_STATE:(SSD_GROUPS + g + 1) * SSD_STATE] for g in groups]
               for c in chunks]
        scores = [[_dot_nt(c_c[c][g], b_c[c][g]) for g in groups] for c in chunks]
        tot = [a[0:1] if reverse else a[CHUNK - 1:CHUNK] for a in acs]
        acs_t = [a.T for a in acs]
        x_d = [(xdt[sls[c]] * jnp.exp2(tot[c] - acs[c])).astype(BF16) for c in chunks]
        st = [jnp.concatenate([_dot_tn(b_c[c][g], x_d[c][:, g * half:(g + 1) * half]) for g in groups], axis=1)
              for c in chunks]
        weights = []
        for c in chunks:
            w_c = []
            for hd in range(SSD_HEADS):
                col = jnp.broadcast_to(acs[c][:, hd * SSD_HEAD_DIM:hd * SSD_HEAD_DIM + 1], (CHUNK, CHUNK))
                row = jnp.broadcast_to(acs_t[c][hd * SSD_HEAD_DIM:hd * SSD_HEAD_DIM + 1, :], (CHUNK, CHUNK))
                decay = jnp.exp2(jnp.where(tri, col - row, -jnp.inf))
                w_c.append((scores[c][hd // per_group] * decay).astype(BF16))
            weights.append(w_c)
        y = []
        for c in chunks:
            acc = None
            for hd in range(SSD_HEADS):
                part = _dot(weights[c][hd], jnp.where(lane_head == hd, xdt[sls[c]], 0.0).astype(BF16))
                acc = part if acc is None else acc + part
            y.append(acc)
        h = h_ref[...]
        h_in = [None] * nc
        for c in (reversed(chunks) if reverse else chunks):
            h_in[c] = h
            h = h * jnp.exp2(tot[c]) + st[c]
        h_ref[...] = h
        y_off = [jnp.concatenate([_dot(c_c[c][g], h_in[c][:, g * half:(g + 1) * half].astype(BF16)) for g in groups],
                                 axis=1) for c in chunks]
        return [y[c] + y_off[c] * jnp.exp2(acs[c]) for c in chunks]

    @pl.when(j < nt)
    def _():
        for c, y in enumerate(scan(False)):
            yf_ref[pl.ds(base + c * CHUNK, CHUNK), :] = y

    @pl.when(j >= nt)
    def _():
        for c, yb in enumerate(scan(True)):
            sl = slice(c * CHUNK, (c + 1) * CHUNK)
            y = yf_ref[pl.ds(base + c * CHUNK, CHUNK), :] + yb + d_ref[...] * xs[sl]
            gated = y * _silu(z_ref[0, sl, :])
            ms = jnp.mean(gated * gated, axis=-1, keepdims=True)
            out_ref[0, sl, :] = gated * lax.rsqrt(ms + EPS) * g_ref[...]


def _ssd(ssd_g, cw, cb, dtb, expand, a_exp, d_exp, g, tc):
    b, t, _ = ssd_g.shape
    nt = t // tc
    hb = tc // HALO

    def tile_of(j):
        return jnp.where(j < nt, j, 2 * nt - 1 - j)

    def conv_tile(j):
        return jnp.minimum(j, nt - 1)

    kern = functools.partial(_ssd_kernel, nt=nt, tc=tc)
    return pl.pallas_call(
        kern,
        grid=(b, 2 * nt),
        in_specs=[
            pl.BlockSpec((1, tc, SSD_XBC), lambda i, j: (i, conv_tile(j), 0)),
            pl.BlockSpec((1, HALO, SSD_XBC), lambda i, j: (i, jnp.maximum(conv_tile(j) * hb - 1, 0), 0)),
            pl.BlockSpec((1, HALO, SSD_XBC), lambda i, j: (i, jnp.minimum((conv_tile(j) + 1) * hb, t // HALO - 1), 0)),
            pl.BlockSpec((1, tc, GROUP_W), lambda i, j: (i, tile_of(j), SSD_XBC // GROUP_W)),
            pl.BlockSpec((1, tc, LANE), lambda i, j: (i, tile_of(j), (SSD_XBC + GROUP_W) // LANE)),
            pl.BlockSpec((HALO, SSD_XBC), lambda i, j: (0, 0)),
            pl.BlockSpec((1, SSD_XBC), lambda i, j: (0, 0)),
            pl.BlockSpec((1, LANE), lambda i, j: (0, 0)),
            pl.BlockSpec((1, LANE, GROUP_W), lambda i, j: (j // nt, 0, 0)),
            pl.BlockSpec((1, 1, GROUP_W), lambda i, j: (j // nt, 0, 0)),
            pl.BlockSpec((1, GROUP_W), lambda i, j: (0, 0)),
            pl.BlockSpec((1, GROUP_W), lambda i, j: (0, 0)),
        ],
        out_specs=pl.BlockSpec((1, tc, GROUP_W), lambda i, j: (i, jnp.where(j < nt, nt - 1, 2 * nt - 1 - j), 0)),
        out_shape=jax.ShapeDtypeStruct((b, t, GROUP_W), F32),
        scratch_shapes=[pltpu.VMEM((t, GROUP_W), F32),
                        pltpu.VMEM((t, GROUP_W), F32),
                        pltpu.VMEM((t, 2 * SSD_GROUPS * SSD_STATE), BF16),
                        pltpu.VMEM((SSD_STATE, GROUP_W), F32),
                        pltpu.VMEM((tc + 2 * HALO, SSD_XBC), F32)],
        compiler_params=_params(dimension_semantics=("arbitrary", "arbitrary")),
        name="ssd",
    )(ssd_g, ssd_g, ssd_g, ssd_g, ssd_g, cw, cb, dtb, expand, a_exp, d_exp, g)


GLA_QK_W = GLA_HEADS * GLA_DK
GLA_V_W = GLA_HEADS * GLA_DV


def _gla_kernel(q_ref, k_ref, v_ref, r_ref, lr_ref, w2_ref, gb_ref, g_ref, avg_ref,
                out_ref, of_ref, s_ref, *, nt, tc):
    j = pl.program_id(1)
    tile = jnp.where(j < nt, j, 2 * nt - 1 - j)

    @pl.when((j == 0) | (j == nt))
    def _():
        s_ref[...] = jnp.zeros_like(s_ref)

    base = pl.multiple_of(tile * tc, tc)
    nc = tc // CHUNK
    logit = _dot(lr_ref[0].astype(BF16), w2_ref[0]) + gb_ref[0]
    logg = -_softplus(-logit) * (LOG2E / GLA_TAU)
    q = q_ref[0] * (GLA_DK ** -0.5)
    k = k_ref[0]
    v = v_ref[0]
    qk_head = lax.broadcasted_iota(jnp.int32, (1, GLA_QK_W), 1) // GLA_DK
    v_head = lax.broadcasted_iota(jnp.int32, (1, GLA_V_W), 1) // GLA_DV
    blockdiag = (lax.broadcasted_iota(jnp.int32, (GLA_QK_W, GLA_V_W), 0) // GLA_DK
                 == lax.broadcasted_iota(jnp.int32, (GLA_QK_W, GLA_V_W), 1) // GLA_DV)

    def scan(reverse):
        tri = _tri(reverse)
        tri_bf = tri.astype(BF16)
        last = 0 if reverse else CHUNK - 1
        chunks = range(nc)
        sls = [slice(c * CHUNK, (c + 1) * CHUNK) for c in chunks]
        gcs = [_split_dot(tri_bf, logg[sl]) for sl in sls]
        tot = [g[last:last + 1] for g in gcs]
        dec_col = [jnp.exp2(g.T[:, last:last + 1]) for g in gcs]
        qg = [q[sl] * jnp.exp2(g) for sl, g in zip(sls, gcs)]
        kg = [(k[sl] * jnp.exp2(-g)).astype(BF16) for sl, g in zip(sls, gcs)]
        kd = [(k[sl] * jnp.exp2(t - g)).astype(BF16) for sl, g, t in zip(sls, gcs, tot)]
        raw = [[_dot_nt(jnp.where(qk_head == hd, qg[c], 0.0).astype(BF16), kg[c]) for hd in range(GLA_HEADS)]
               for c in chunks]
        st = [jnp.where(blockdiag, _dot_tn(kd[c], v[sls[c]].astype(BF16)), 0.0) for c in chunks]
        att = [[jnp.where(tri, raw[c][hd], 0.0).astype(BF16) for hd in range(GLA_HEADS)] for c in chunks]
        o = []
        for c in chunks:
            acc = None
            for hd in range(GLA_HEADS):
                part = _dot(att[c][hd], jnp.where(v_head == hd, v[sls[c]], 0.0).astype(BF16))
                acc = part if acc is None else acc + part
            o.append(acc)
        s = s_ref[...]
        s_in = [None] * nc
        for c in (reversed(chunks) if reverse else chunks):
            s_in[c] = s
            s = s * dec_col[c] + st[c]
        s_ref[...] = s
        return [o[c] + _dot(qg[c].astype(BF16), s_in[c].astype(BF16)) for c in chunks]

    @pl.when(j < nt)
    def _():
        for c, o in enumerate(scan(False)):
            of_ref[pl.ds(base + c * CHUNK, CHUNK), :] = o

    @pl.when(j >= nt)
    def _():
        for c, ob in enumerate(scan(True)):
            sl = slice(c * CHUNK, (c + 1) * CHUNK)
            o = of_ref[pl.ds(base + c * CHUNK, CHUNK), :] + ob
            ms = _split_dot_right(o * o, avg_ref[...])
            out_ref[0, sl, :] = o * lax.rsqrt(ms + EPS) * g_ref[...] * _silu(r_ref[0, sl, :])


def _split_dot_right(x, m_bf16):
    x1 = x.astype(BF16)
    r1 = x - x1.astype(F32)
    x2 = r1.astype(BF16)
    x3 = (r1 - x2.astype(F32)).astype(BF16)
    return _dot(x1, m_bf16) + _dot(x2, m_bf16) + _dot(x3, m_bf16)


def _gla(gla_g, w2p, gb, g, avg, tc):
    b, t, _ = gla_g.shape
    nt = t // tc

    def tile_of(j):
        return jnp.where(j < nt, j, 2 * nt - 1 - j)

    kern = functools.partial(_gla_kernel, nt=nt, tc=tc)
    return pl.pallas_call(
        kern,
        grid=(b, 2 * nt),
        in_specs=[
            pl.BlockSpec((1, tc, GLA_QK_W), lambda i, j: (i, tile_of(j), 0)),
            pl.BlockSpec((1, tc, GLA_QK_W), lambda i, j: (i, tile_of(j), 1)),
            pl.BlockSpec((1, tc, GLA_V_W), lambda i, j: (i, tile_of(j), 1)),
            pl.BlockSpec((1, tc, GLA_V_W), lambda i, j: (i, tile_of(j), 2)),
            pl.BlockSpec((1, tc, LANE), lambda i, j: (i, tile_of(j), (2 * GLA_QK_W + 2 * GLA_V_W) // LANE)),
            pl.BlockSpec((1, LANE, GLA_QK_W), lambda i, j: (j // nt, 0, 0)),
            pl.BlockSpec((1, 1, GLA_QK_W), lambda i, j: (j // nt, 0, 0)),
            pl.BlockSpec((1, GLA_V_W), lambda i, j: (0, 0)),
            pl.BlockSpec((GLA_V_W, GLA_V_W), lambda i, j: (0, 0)),
        ],
        out_specs=pl.BlockSpec((1, tc, GLA_V_W), lambda i, j: (i, jnp.where(j < nt, nt - 1, 2 * nt - 1 - j), 0)),
        out_shape=jax.ShapeDtypeStruct((b, t, GLA_V_W), F32),
        scratch_shapes=[pltpu.VMEM((t, GLA_V_W), F32),
                        pltpu.VMEM((GLA_QK_W, GLA_V_W), F32)],
        compiler_params=_params(dimension_semantics=("arbitrary", "arbitrary")),
        name="gla",
    )(gla_g, gla_g, gla_g, gla_g, gla_g, w2p, gb, g, avg)


def _rope(y, cos, sin, half):
    lane = lax.broadcasted_iota(jnp.int32, (1, LANE), 1)
    lo = (lane % (2 * half)) < half
    rot = jnp.where(lo, pltpu.roll(y, LANE - half, 1), pltpu.roll(y, half, 1))
    return y * cos + rot * sin


ONES_LANE = LANE - 1
VT_ROWS = 80
SCORE_BOUND_LIMIT = 48.0

def _finish_keys(k):
    kf = k.astype(BF16).astype(F32)
    norm = jnp.sqrt(jnp.max(jnp.sum(kf * kf, axis=-1, keepdims=True), axis=0, keepdims=True))
    lane = lax.broadcasted_iota(jnp.int32, (1, LANE), 1)
    return jnp.where(lane == ONES_LANE, 1.0, k).astype(BF16), norm


def _update_kmax(kmax_ref, norms):
    @pl.when(pl.program_id(1) == 0)
    def _():
        kmax_ref[...] = jnp.zeros_like(kmax_ref)

    lane = lax.broadcasted_iota(jnp.int32, (1, LANE), 1)
    upd = jnp.zeros((1, LANE), F32)
    for g, norm in enumerate(norms):
        upd = jnp.where(lane == g, norm, upd)
    kmax_ref[0] = jnp.maximum(kmax_ref[0], upd)


def _vt_rows(v, heads, dv):
    tm = v.shape[0]
    vt = v.T.reshape(heads, dv, tm)
    row = lax.broadcasted_iota(jnp.int32, (heads, VT_ROWS - dv, tm), 1)
    return jnp.concatenate([vt, jnp.where(row == 0, 1.0, 0.0)], axis=1).astype(BF16)


def _attn_kernel(q_ref, k_ref, vt_ref, kmax_ref, o_ref, qa_ref, m_ref, acc_ref, *, n_q, rep, tk, dv):
    nk = k_ref.shape[1] // tk
    lane = lax.broadcasted_iota(jnp.int32, (1, LANE), 1)
    kmax = kmax_ref[0, 0:1, :]
    bound_max = None
    for h in range(n_q):
        qf = q_ref[0, :, h * LANE:(h + 1) * LANE].astype(F32)
        kg = jnp.max(jnp.where(lane == h // rep, kmax, 0.0), axis=-1, keepdims=True)
        bound = jnp.sqrt(jnp.sum(qf * qf, axis=-1, keepdims=True)) * kg
        qa_ref[h] = jnp.where(lane == ONES_LANE, -bound, qf).T.astype(BF16)
        top = jnp.max(bound)
        bound_max = top if bound_max is None else jnp.maximum(bound_max, top)
    acc_ref[...] = jnp.zeros(acc_ref.shape, F32)

    @pl.when(bound_max <= SCORE_BOUND_LIMIT)
    def _():
        def scores(i, h):
            ks = pl.multiple_of(i * tk, tk)
            g = h // rep
            return _dot(k_ref[0, pl.ds(ks, tk), g * LANE:(g + 1) * LANE], qa_ref[h])

        group = max(u for u in (1, 2, 4, 8) if nk % u == 0)

        def body(ii, carry):
            steps = [(ii * group + u, h) for u in range(group) for h in range(n_q)]
            s_next = scores(*steps[0])
            for n, (i, h) in enumerate(steps):
                s = s_next
                if n + 1 < len(steps):
                    s_next = scores(*steps[n + 1])
                p = jnp.exp2(s)
                acc_ref[h, 0:dv, :] += _dot(vt_ref[0, h // rep, i, 0:dv, :], p.astype(BF16))
                acc_ref[h, dv:dv + 8, :] += jnp.sum(p.reshape(tk // 8, 8, p.shape[1]), axis=0)
            return carry

        lax.fori_loop(0, nk // group, body, 0)

    @pl.when(bound_max > SCORE_BOUND_LIMIT)
    def _():
        m_ref[...] = jnp.full(m_ref.shape, -jnp.inf, F32)

        def body(i, carry):
            ks = pl.multiple_of(i * tk, tk)
            for h in range(n_q):
                g = h // rep
                s = _dot_nt(k_ref[0, pl.ds(ks, tk), g * LANE:(g + 1) * LANE], q_ref[0, :, h * LANE:(h + 1) * LANE])
                m_old = m_ref[h]
                m_new = jnp.maximum(m_old, jnp.max(s, axis=0, keepdims=True))
                p = jnp.exp2(s - m_new)
                acc_ref[h] = acc_ref[h] * jnp.exp2(m_old - m_new) + _dot(vt_ref[0, g, i], p.astype(BF16))
                m_ref[h] = m_new
            return carry

        lax.fori_loop(0, nk, body, 0)

    o = jnp.concatenate([acc_ref[h, 0:dv, :] * (1.0 / jnp.sum(acc_ref[h, dv:dv + 8, :], axis=0, keepdims=True))
                         for h in range(n_q)], axis=0)
    o_ref[0] = o.T


def _attention(q, k, vt, kmax, rep, tq, tk, dv):
    b, t, qw = q.shape
    kw = k.shape[2]
    n_q = qw // LANE
    n_kv = n_q // rep
    ow = n_q * dv
    kern = functools.partial(_attn_kernel, n_q=n_q, rep=rep, tk=tk, dv=dv)
    return pl.pallas_call(
        kern,
        grid=(b, t // tq),
        in_specs=[
            pl.BlockSpec((1, tq, qw), lambda i, j: (i, j, 0)),
            pl.BlockSpec((1, t, kw), lambda i, j: (i, 0, 0)),
            pl.BlockSpec((1, n_kv, t // tk, VT_ROWS, tk), lambda i, j: (i, 0, 0, 0, 0)),
            pl.BlockSpec((1, 8, LANE), lambda i, j: (i, 0, 0)),
        ],
        out_specs=pl.BlockSpec((1, tq, ow), lambda i, j: (i, j, 0)),
        out_shape=jax.ShapeDtypeStruct((b, t, ow), F32),
        scratch_shapes=[pltpu.VMEM((n_q, LANE, tq), BF16), pltpu.VMEM((n_q, 1, tq), F32),
                        pltpu.VMEM((n_q, VT_ROWS, tq), F32)],
        compiler_params=_params(dimension_semantics=("arbitrary", "arbitrary")),
        name="attention",
    )(q, k, vt, kmax)


FF_CHUNK = 1024


def _outmlp_kernel(x_ref, m0_ref, m1_ref, m2_ref, m3_ref, wo_ref, g2_ref, w1_ref, w2_ref, gf_ref, o_ref, *, final):
    x1 = x_ref[...]
    for i, m_ref in enumerate((m0_ref, m1_ref, m2_ref, m3_ref)):
        x1 = x1 + _dot(m_ref[...].astype(BF16), wo_ref[i * GROUP_W:(i + 1) * GROUP_W, :])
    ms = jnp.mean(x1 * x1, axis=-1, keepdims=True)
    h = (x1 * lax.rsqrt(ms + EPS) * g2_ref[...]).astype(BF16)
    acc = None
    for c in range(D_FF // FF_CHUNK):
        sl = slice(c * FF_CHUNK, (c + 1) * FF_CHUNK)
        u = jnp.maximum(_dot(h, w1_ref[:, sl]), 0.0)
        part = _dot((u * u).astype(BF16), w2_ref[sl, :])
        acc = part if acc is None else acc + part
    y = x1 + acc
    if final:
        ms = jnp.mean(y * y, axis=-1, keepdims=True)
        y = y * lax.rsqrt(ms + EPS) * gf_ref[...]
    o_ref[...] = y


def _outmlp(x2d, mixes, wo, g2, w1, w2, gf, final, tm):
    n = x2d.shape[0]
    const = lambda i: (0, 0)
    single = pl.Buffered(1)
    kern = functools.partial(_outmlp_kernel, final=final)
    return pl.pallas_call(
        kern,
        grid=(n // tm,),
        in_specs=[pl.BlockSpec((tm, D_MODEL), lambda i: (i, 0))]
                 + [pl.BlockSpec((tm, GROUP_W), lambda i: (i, 0)) for _ in range(4)]
                 + [pl.BlockSpec((D_MODEL, D_MODEL), const, pipeline_mode=single),
                    pl.BlockSpec((1, D_MODEL), const),
                    pl.BlockSpec((D_MODEL, D_FF), const, pipeline_mode=single),
                    pl.BlockSpec((D_FF, D_MODEL), const, pipeline_mode=single),
                    pl.BlockSpec((1, D_MODEL), const)],
        out_specs=pl.BlockSpec((tm, D_MODEL), lambda i: (i, 0)),
        out_shape=jax.ShapeDtypeStruct((n, D_MODEL), F32),
        compiler_params=_params(dimension_semantics=("arbitrary",)),
        name="outmlp",
    )(x2d, *mixes, wo, g2.reshape(1, D_MODEL), w1, w2, gf.reshape(1, D_MODEL))


def _rope_tables(t):
    pos = np.arange(t, dtype=np.int32)
    row = (pos // GRID_W).astype(np.float32)
    col = (pos % GRID_W).astype(np.float32)

    def block(p, n):
        inv_freq = np.float32(ROPE_THETA) ** (-np.arange(n, dtype=np.float32) / np.float32(n))
        ang = p[:, None] * inv_freq[None, :].astype(np.float32)
        c, s = np.cos(ang).astype(np.float32), np.sin(ang).astype(np.float32)
        return np.concatenate([c, c], axis=1), np.concatenate([-s, s], axis=1)

    def table(n, lead):
        cr, sr = block(row, n)
        cc, sc = block(col, n)
        tail = LANE - lead - 4 * n
        cos = np.concatenate([np.ones((t, lead), np.float32), cr, cc, np.ones((t, tail), np.float32)], axis=1)
        sin = np.concatenate([np.zeros((t, lead), np.float32), sr, sc, np.zeros((t, tail), np.float32)], axis=1)
        return jnp.asarray(cos), jnp.asarray(sin)

    return table(GQA_HEAD_DIM // 4, 0), table(MLA_ROPE // 4, MLA_NOPE)


def _pad_lanes(v, width=LANE):
    return jnp.concatenate([v, jnp.zeros((width - v.shape[0],), v.dtype)]).reshape(1, width)


def _layer_params(i, p):
    (norm1_g, w_in, ssd_conv_w, ssd_conv_b, ssd_dt_bias, ssd_a_log, ssd_d, ssd_norm_g,
     gqa_q_norm_g, gqa_k_norm_g, gla_gate_w2, gla_gate_b, gla_norm_g, mla_q_norm_g, mla_w_uq,
     mla_kv_norm_g, mla_w_ukv, w_out, norm2_g, w_ff1, w_ff2) = [a[i] for a in p]
    out = {}
    out["norm1_g"] = norm1_g
    out["w_all"] = _gather_cols(w_in, _IN_COLS).astype(BF16)
    out["conv_w"] = jnp.concatenate([ssd_conv_w, jnp.zeros((HALO - SSD_CONV_W, SSD_XBC), F32)], axis=0)
    out["conv_b"] = ssd_conv_b.reshape(1, SSD_XBC)
    expand = lambda a: jnp.repeat(a, SSD_HEAD_DIM, axis=-1)
    out["dt_bias"] = _pad_lanes(ssd_dt_bias.reshape(2 * SSD_HEADS))
    spread = np.zeros((2, LANE, GROUP_W), np.float32)
    for d in range(2):
        for h in range(SSD_HEADS):
            spread[d, d * SSD_HEADS + h, h * SSD_HEAD_DIM:(h + 1) * SSD_HEAD_DIM] = 1.0
    out["dt_spread"] = jnp.asarray(spread, BF16)
    out["a_neg"] = expand(-jnp.exp(ssd_a_log) * LOG2E).reshape(2, 1, GROUP_W)
    out["ssd_d"] = expand(ssd_d).reshape(1, GROUP_W)
    out["ssd_norm_g"] = ssd_norm_g.reshape(1, GROUP_W)
    out["gq"] = _pad_lanes(gqa_q_norm_g)
    out["gk"] = _pad_lanes(gqa_k_norm_g)
    w2p = jnp.zeros((2, LANE, GLA_QK_W), F32)
    for d in range(2):
        w2p = w2p.at[d, d * GLA_LOWRANK:(d + 1) * GLA_LOWRANK, :].set(gla_gate_w2[d])
    out["gla_w2"] = w2p.astype(BF16)
    out["gla_b"] = gla_gate_b.reshape(2, 1, GLA_QK_W)
    out["gla_norm_g"] = jnp.tile(gla_norm_g, GLA_HEADS).reshape(1, GLA_V_W)
    out["mla_gq"] = mla_q_norm_g.reshape(1, MLA_Q_LORA)
    out["mla_gkv"] = mla_kv_norm_g.reshape(1, MLA_KV_LORA)
    uq_cols, uk_cols, uv_cols = [], [], []
    for h in range(MLA_HEADS):
        uq_cols += list(range(h * MLA_QK, (h + 1) * MLA_QK)) + [-1] * (LANE - MLA_QK)
        base = h * (MLA_NOPE + MLA_V)
        uk_cols += list(range(base, base + MLA_NOPE)) + [-1] * (LANE - MLA_NOPE)
        uv_cols += list(range(base + MLA_NOPE, base + MLA_NOPE + MLA_V))
    out["wuq"] = _gather_cols(mla_w_uq, np.asarray(uq_cols, np.int32)).astype(BF16)
    out["wuk"] = _gather_cols(mla_w_ukv, np.asarray(uk_cols, np.int32)).astype(BF16)
    out["wuv"] = _gather_cols(mla_w_ukv, np.asarray(uv_cols, np.int32)).astype(BF16)
    out["w_out"] = w_out.astype(BF16)
    out["norm2_g"] = norm2_g
    out["w_ff1"] = w_ff1.astype(BF16)
    out["w_ff2"] = w_ff2.astype(BF16)
    return out


TILE_PREF = dict(tc=1024, tp=512, tq_gqa=512, tq_mla=512, tmlp=512)


def _tiles(t):
    pick = lambda pref: max(c for c in (128, 256, 512, 1024) if c <= pref and t % c == 0)
    return {name: pick(pref) for name, pref in TILE_PREF.items()}


def _trunk(x, layers, final_norm_g):
    b, t, d = x.shape
    n = b * t
    ts = _tiles(t)
    tk = ts["tp"]
    tables = _rope_tables(t)
    avg = jnp.asarray(np.kron(np.eye(GLA_HEADS), np.full((GLA_DV, GLA_DV), 1.0 / GLA_DV)), BF16)
    x2d = x.reshape(n, d)
    for i, lp in enumerate(layers):
        ssd_g, gla_g, gq, gk, gvt, gkmax, mq, mk, mvt, mkmax = _inproj(x2d.reshape(b, t, d), lp, tables, tk)
        ssd_out = _ssd(ssd_g, lp["conv_w"], lp["conv_b"], lp["dt_bias"], lp["dt_spread"], lp["a_neg"],
                       lp["ssd_d"], lp["ssd_norm_g"], ts["tc"])
        gqa_out = _attention(gq, gk, gvt, gkmax, GQA_HEADS // GQA_KV_HEADS, ts["tq_gqa"], tk, GQA_HEAD_DIM)
        gla_out = _gla(gla_g, lp["gla_w2"], lp["gla_b"], lp["gla_norm_g"], avg, ts["tc"])
        mla_out = _attention(mq, mk, mvt, mkmax, 1, ts["tq_mla"], tk, MLA_V)
        mixes = [m.reshape(n, GROUP_W) for m in (ssd_out, gqa_out, gla_out, mla_out)]
        x2d = _outmlp(x2d, mixes, lp["w_out"], lp["norm2_g"], lp["w_ff1"], lp["w_ff2"], final_norm_g,
                      i == len(layers) - 1, ts["tmlp"])
    return x2d.reshape(b, t, d)


def kernel(x_prompt, x_sample, norm1_g, w_in, ssd_conv_w, ssd_conv_b, ssd_dt_bias, ssd_a_log, ssd_d, ssd_norm_g,
           gqa_q_norm_g, gqa_k_norm_g, gla_gate_w2, gla_gate_b, gla_norm_g, mla_q_norm_g, mla_w_uq,
           mla_kv_norm_g, mla_w_ukv, w_out, norm2_g, w_ff1, w_ff2, final_norm_g):
    stacked = (norm1_g, w_in, ssd_conv_w, ssd_conv_b, ssd_dt_bias, ssd_a_log, ssd_d, ssd_norm_g,
               gqa_q_norm_g, gqa_k_norm_g, gla_gate_w2, gla_gate_b, gla_norm_g, mla_q_norm_g, mla_w_uq,
               mla_kv_norm_g, mla_w_ukv, w_out, norm2_g, w_ff1, w_ff2)
    layers = [_layer_params(i, stacked) for i in range(norm1_g.shape[0])]
    return (_trunk(x_prompt, layers, final_norm_g), _trunk(x_sample, layers, final_norm_g))
```

```python
import functools

import numpy as np
import jax
import jax.numpy as jnp
from jax import lax
from jax.experimental import pallas as pl
from jax.experimental.pallas import tpu as pltpu

F32 = jnp.float32
BF16 = jnp.bfloat16

D_MODEL = 1024
DEPTH = 2
GRID_W = 64
CHUNK = 128
ROPE_THETA = 10000.0
EPS = 1e-6
GROUP_W = D_MODEL // 4
D_FF = 4 * D_MODEL

SSD_HEADS = 4
SSD_HEAD_DIM = GROUP_W // SSD_HEADS
SSD_GROUPS = 2
SSD_STATE = 128
SSD_CONV_W = 5
SSD_XBC = GROUP_W + 2 * SSD_GROUPS * SSD_STATE

GQA_HEADS = 4
GQA_KV_HEADS = 2
GQA_HEAD_DIM = GROUP_W // GQA_HEADS

GLA_HEADS = 4
GLA_DV = GROUP_W // GLA_HEADS
GLA_DK = GLA_DV // 2
GLA_LOWRANK = 16
GLA_TAU = 16.0

MLA_HEADS = 4
MLA_Q_LORA = 256
MLA_KV_LORA = 128
MLA_NOPE = 64
MLA_ROPE = 32
MLA_V = GROUP_W // MLA_HEADS
MLA_QK = MLA_NOPE + MLA_ROPE

IN_SIZES = (GROUP_W, SSD_XBC, 2 * SSD_HEADS,
            GQA_HEADS * GQA_HEAD_DIM, GQA_KV_HEADS * GQA_HEAD_DIM, GQA_KV_HEADS * GQA_HEAD_DIM,
            GLA_HEADS * GLA_DK, GLA_HEADS * GLA_DK, GLA_HEADS * GLA_DV, GLA_HEADS * GLA_DV, 2 * GLA_LOWRANK,
            MLA_Q_LORA, MLA_KV_LORA, MLA_ROPE)
IN_OFFS = tuple(int(v) for v in np.concatenate([[0], np.cumsum(IN_SIZES)]))

LOG2E = 1.4426950408889634
LANE = 128
HALO = 8
VMEM_LIMIT = 52 * 1024 * 1024

SSD_W = SSD_XBC + GROUP_W + LANE
GQA_W = GQA_HEADS * LANE + GQA_KV_HEADS * LANE + GQA_KV_HEADS * GQA_HEAD_DIM
GLA_W = 2 * GLA_HEADS * GLA_DK + 2 * GLA_HEADS * GLA_DV + LANE
MLA_W = MLA_Q_LORA + MLA_KV_LORA + LANE
PROJ_W = SSD_W + GQA_W + GLA_W + MLA_W


def _in_proj_columns():
    o = IN_OFFS
    cols = []
    cols += list(range(o[1], o[2]))
    cols += list(range(o[0], o[1]))
    cols += list(range(o[2], o[3])) + [-1] * (LANE - 2 * SSD_HEADS)
    for h in range(GQA_HEADS):
        cols += list(range(o[3] + h * GQA_HEAD_DIM, o[3] + (h + 1) * GQA_HEAD_DIM)) + [-1] * (LANE - GQA_HEAD_DIM)
    for h in range(GQA_KV_HEADS):
        cols += list(range(o[4] + h * GQA_HEAD_DIM, o[4] + (h + 1) * GQA_HEAD_DIM)) + [-1] * (LANE - GQA_HEAD_DIM)
    cols += list(range(o[5], o[6]))
    cols += list(range(o[6], o[10]))
    cols += list(range(o[10], o[11])) + [-1] * (LANE - 2 * GLA_LOWRANK)
    cols += list(range(o[11], o[13]))
    cols += [-1] * MLA_NOPE + list(range(o[13], o[14])) + [-1] * (LANE - MLA_QK)
    cols = np.asarray(cols, np.int32)
    assert cols.shape[0] == PROJ_W
    return cols


_IN_COLS = _in_proj_columns()


def _gather_cols(w, cols):
    picked = jnp.take(w, jnp.asarray(np.maximum(cols, 0)), axis=1)
    return jnp.where(jnp.asarray(cols >= 0)[None, :], picked, 0.0)


def _params(**kw):
    return pltpu.CompilerParams(vmem_limit_bytes=VMEM_LIMIT, **kw)


def _silu(x):
    return x * (1.0 / (1.0 + jnp.exp(-x)))


def _softplus(x):
    return jnp.maximum(x, 0.0) + jnp.log1p(jnp.exp(-jnp.abs(x)))


def _dot(a, b):
    return jnp.dot(a, b, preferred_element_type=F32)


def _dot_nt(a, b):
    return lax.dot_general(a, b, (((1,), (1,)), ((), ())), preferred_element_type=F32)


def _dot_tn(a, b):
    return lax.dot_general(a, b, (((0,), (0,)), ((), ())), preferred_element_type=F32)


def _split_dot(m_bf16, x):
    x1 = x.astype(BF16)
    r1 = x - x1.astype(F32)
    x2 = r1.astype(BF16)
    x3 = (r1 - x2.astype(F32)).astype(BF16)
    return _dot(m_bf16, x1) + _dot(m_bf16, x2) + _dot(m_bf16, x3)


def _tri(reverse):
    li = lax.broadcasted_iota(jnp.int32, (CHUNK, CHUNK), 0)
    si = lax.broadcasted_iota(jnp.int32, (CHUNK, CHUNK), 1)
    return (si >= li) if reverse else (si <= li)


def _inproj_kernel(x_ref, g_ref, w_ref, cosg_ref, sing_ref, cosm_ref, sinm_ref, gq_ref, gk_ref,
                   mgq_ref, mgkv_ref, wuq_ref, wuk_ref, wuv_ref,
                   ssd_ref, gla_ref, gq_out, gk_out, gvt_out, gkmax_out, mq_out, mk_out, mvt_out, mkmax_out):
    x = x_ref[0]
    ms = jnp.mean(x * x, axis=-1, keepdims=True)
    h = (x * lax.rsqrt(ms + EPS) * g_ref[...]).astype(BF16)
    c_gqa, c_gla, c_mla = SSD_W, SSD_W + GQA_W, SSD_W + GQA_W + GLA_W
    mla = _dot(h, w_ref[:, c_mla:PROJ_W])
    gqa = _dot(h, w_ref[:, c_gqa:c_gla])

    def rms(y, g):
        ms_y = jnp.mean(y * y, axis=-1, keepdims=True)
        return (y * lax.rsqrt(ms_y + EPS) * g).astype(BF16)

    cq = rms(mla[:, :MLA_Q_LORA], mgq_ref[...])
    ckv = rms(mla[:, MLA_Q_LORA:MLA_Q_LORA + MLA_KV_LORA], mgkv_ref[...])
    ssd_ref[0] = _dot(h, w_ref[:, 0:SSD_W])
    q = _dot(cq, wuq_ref[...])
    kn = _dot(ckv, wuk_ref[...])
    v = _dot(ckv, wuv_ref[...])

    cos, sin = cosg_ref[...], sing_ref[...]
    qw, kw = GQA_HEADS * LANE, GQA_KV_HEADS * LANE
    heads = [gqa[:, hd * LANE:(hd + 1) * LANE] for hd in range(GQA_HEADS + GQA_KV_HEADS)]
    gains = [gq_ref[...]] * GQA_HEADS + [gk_ref[...]] * GQA_KV_HEADS
    inv = [lax.rsqrt(jnp.sum(y * y, axis=-1, keepdims=True) * (1.0 / GQA_HEAD_DIM) + EPS) for y in heads]
    roped = [_rope(y * r * g, cos, sin, GQA_HEAD_DIM // 4) for y, r, g in zip(heads, inv, gains)]
    for hd in range(GQA_HEADS):
        gq_out[0, :, hd * LANE:(hd + 1) * LANE] = (roped[hd] * (GQA_HEAD_DIM ** -0.5 * LOG2E)).astype(BF16)
    norms = []
    for hd in range(GQA_KV_HEADS):
        gk_out[0, :, hd * LANE:(hd + 1) * LANE], norm = _finish_keys(roped[GQA_HEADS + hd])
        norms.append(norm)
    _update_kmax(gkmax_out, norms)
    gvt_out[0, :, 0] = _vt_rows(gqa[:, qw + kw:], GQA_KV_HEADS, GQA_HEAD_DIM)
    gla_ref[0] = _dot(h, w_ref[:, c_gla:c_mla])

    cos, sin = cosm_ref[...], sinm_ref[...]
    k_rope = _rope(mla[:, MLA_Q_LORA + MLA_KV_LORA:], cos, sin, MLA_ROPE // 4)
    q_roped = [_rope(q[:, hd * LANE:(hd + 1) * LANE], cos, sin, MLA_ROPE // 4) for hd in range(MLA_HEADS)]
    norms = []
    for hd in range(MLA_HEADS):
        sl = slice(hd * LANE, (hd + 1) * LANE)
        mq_out[0, :, sl] = (q_roped[hd] * (MLA_QK ** -0.5 * LOG2E)).astype(BF16)
        mk_out[0, :, sl], norm = _finish_keys(kn[:, sl] + k_rope)
        norms.append(norm)
    _update_kmax(mkmax_out, norms)
    mvt_out[0, :, 0] = _vt_rows(v, MLA_HEADS, MLA_V)


def _inproj(x, lp, tables, tm):
    b, t, _ = x.shape
    (cos_g, sin_g), (cos_m, sin_m) = tables
    const = lambda i, j: (0, 0)
    tok = lambda i, j: (i, j, 0)
    tab = pl.BlockSpec((tm, LANE), lambda i, j: (j, 0))
    qw, kw, mw = GQA_HEADS * LANE, GQA_KV_HEADS * LANE, MLA_HEADS * LANE

    def vt_spec(heads):
        return pl.BlockSpec((1, heads, 1, VT_ROWS, tm), lambda i, j: (i, 0, j, 0, 0))

    kmax_spec = pl.BlockSpec((1, 8, LANE), lambda i, j: (i, 0, 0))
    return pl.pallas_call(
        _inproj_kernel,
        grid=(b, t // tm),
        in_specs=[pl.BlockSpec((1, tm, D_MODEL), tok),
                  pl.BlockSpec((1, D_MODEL), const),
                  pl.BlockSpec((D_MODEL, PROJ_W), const, pipeline_mode=pl.Buffered(1)),
                  tab, tab, tab, tab,
                  pl.BlockSpec((1, LANE), const), pl.BlockSpec((1, LANE), const),
                  pl.BlockSpec((1, MLA_Q_LORA), const), pl.BlockSpec((1, MLA_KV_LORA), const),
                  pl.BlockSpec((MLA_Q_LORA, mw), const), pl.BlockSpec((MLA_KV_LORA, mw), const),
                  pl.BlockSpec((MLA_KV_LORA, MLA_HEADS * MLA_V), const)],
        out_specs=[pl.BlockSpec((1, tm, SSD_W), tok), pl.BlockSpec((1, tm, GLA_W), tok),
                   pl.BlockSpec((1, tm, qw), tok), pl.BlockSpec((1, tm, kw), tok), vt_spec(GQA_KV_HEADS), kmax_spec,
                   pl.BlockSpec((1, tm, mw), tok), pl.BlockSpec((1, tm, mw), tok), vt_spec(MLA_HEADS), kmax_spec],
        out_shape=[jax.ShapeDtypeStruct((b, t, SSD_W), F32), jax.ShapeDtypeStruct((b, t, GLA_W), F32),
                   jax.ShapeDtypeStruct((b, t, qw), BF16), jax.ShapeDtypeStruct((b, t, kw), BF16),
                   jax.ShapeDtypeStruct((b, GQA_KV_HEADS, t // tm, VT_ROWS, tm), BF16),
                   jax.ShapeDtypeStruct((b, 8, LANE), F32),
                   jax.ShapeDtypeStruct((b, t, mw), BF16), jax.ShapeDtypeStruct((b, t, mw), BF16),
                   jax.ShapeDtypeStruct((b, MLA_HEADS, t // tm, VT_ROWS, tm), BF16),
                   jax.ShapeDtypeStruct((b, 8, LANE), F32)],
        compiler_params=_params(dimension_semantics=("arbitrary", "arbitrary")),
        name="inproj",
    )(x, lp["norm1_g"].reshape(1, D_MODEL), lp["w_all"], cos_g, sin_g, cos_m, sin_m, lp["gq"], lp["gk"],
      lp["mla_gq"], lp["mla_gkv"], lp["wuq"], lp["wuk"], lp["wuv"])


def _ssd_kernel(xbc_ref, prev_ref, next_ref, z_ref, dtx_ref, cw_ref, cb_ref, dtb_ref, ex_ref, a_ref, d_ref, g_ref,
                out_ref, yf_ref, xs_ref, bc_ref, h_ref, xe_ref, *, nt, tc):
    j = pl.program_id(1)
    tile = jnp.where(j < nt, j, 2 * nt - 1 - j)
    base = pl.multiple_of(tile * tc, tc)
    nc = tc // CHUNK
    half = GROUP_W // SSD_GROUPS

    @pl.when((j == 0) | (j == nt))
    def _():
        h_ref[...] = jnp.zeros_like(h_ref)

    @pl.when(j < nt)
    def _():
        xe_ref[HALO:HALO + tc, :] = xbc_ref[0]
        xe_ref[0:HALO, :] = jnp.where(tile > 0, prev_ref[0], 0.0)
        xe_ref[HALO + tc:2 * HALO + tc, :] = jnp.where(tile < nt - 1, next_ref[0], 0.0)
        pad = SSD_CONV_W // 2
        window = xe_ref[...]
        acc = cb_ref[...] + cw_ref[pad:pad + 1, :] * window[HALO:HALO + tc]
        for w in range(SSD_CONV_W):
            if w != pad:
                shifted = pltpu.roll(window, (pad - w) % (tc + 2 * HALO), 0)
                acc = acc + cw_ref[w:w + 1, :] * shifted[HALO:HALO + tc]
        xbc = _silu(acc)
        xs_ref[pl.ds(base, tc), :] = xbc[:, :GROUP_W]
        bc_ref[pl.ds(base, tc), :] = xbc[:, GROUP_W:].astype(BF16)

    xs = xs_ref[pl.ds(base, tc), :]
    dt_heads = _softplus(dtx_ref[0] + dtb_ref[...])
    dt = _split_dot_right(dt_heads, ex_ref[0])
    a_dt = dt * a_ref[0]
    xdt = xs * dt
    lane_head = lax.broadcasted_iota(jnp.int32, (1, GROUP_W), 1) // SSD_HEAD_DIM

    def scan(reverse):
        tri = _tri(reverse)
        tri_bf = tri.astype(BF16)
        chunks = range(nc)
        groups = range(SSD_GROUPS)
        per_group = SSD_HEADS // SSD_GROUPS
        sls = [slice(c * CHUNK, (c + 1) * CHUNK) for c in chunks]
        rows = [pl.ds(base + c * CHUNK, CHUNK) for c in chunks]
        acs = [_split_dot(tri_bf, a_dt[sl]) for sl in sls]
        b_c = [[bc_ref[rows[c], g * SSD_STATE:(g + 1) * SSD_STATE] for g in groups] for c in chunks]
        c_c = [[bc_ref[rows[c], (SSD_GROUPS + g) * SSD_STATE:(SSD_GROUPS + g + 1) * SSD_STATE] for g in groups]
               for c in chunks]
        scores = [[_dot_nt(c_c[c][g], b_c[c][g]) for g in groups] for c in chunks]
        tot = [a[0:1] if reverse else a[CHUNK - 1:CHUNK] for a in acs]
        acs_t = [a.T for a in acs]
        x_d = [(xdt[sls[c]] * jnp.exp2(tot[c] - acs[c])).astype(BF16) for c in chunks]
        st = [jnp.concatenate([_dot_tn(b_c[c][g], x_d[c][:, g * half:(g + 1) * half]) for g in groups], axis=1)
              for c in chunks]
        weights = []
        for c in chunks:
            w_c = []
            for hd in range(SSD_HEADS):
                col = jnp.broadcast_to(acs[c][:, hd * SSD_HEAD_DIM:hd * SSD_HEAD_DIM + 1], (CHUNK, CHUNK))
                row = jnp.broadcast_to(acs_t[c][hd * SSD_HEAD_DIM:hd * SSD_HEAD_DIM + 1, :], (CHUNK, CHUNK))
                decay = jnp.exp2(jnp.where(tri, col - row, -jnp.inf))
                w_c.append((scores[c][hd // per_group] * decay).astype(BF16))
            weights.append(w_c)
        y = []
        for c in chunks:
            parts = []
            for g in groups:
                x_g = xdt[sls[c], g * half:(g + 1) * half]
                own = lane_head[:, g * half:(g + 1) * half]
                w_g = jnp.concatenate([weights[c][g * per_group + r] for r in range(per_group)], axis=1)
                x_st = jnp.concatenate([jnp.where(own == g * per_group + r, x_g, 0.0) for r in range(per_group)], axis=0)
                parts.append(_dot(w_g, x_st.astype(BF16)))
            y.append(jnp.concatenate(parts, axis=1))
        h = h_ref[...]
        h_in = [None] * nc
        for c in (reversed(chunks) if reverse else chunks):
            h_in[c] = h
            h = h * jnp.exp2(tot[c]) + st[c]
        h_ref[...] = h
        y_off = [jnp.concatenate([_dot(c_c[c][g], h_in[c][:, g * half:(g + 1) * half].astype(BF16)) for g in groups],
                                 axis=1) for c in chunks]
        return [y[c] + y_off[c] * jnp.exp2(acs[c]) for c in chunks]

    @pl.when(j < nt)
    def _():
        for c, y in enumerate(scan(False)):
            yf_ref[pl.ds(base + c * CHUNK, CHUNK), :] = y

    @pl.when(j >= nt)
    def _():
        for c, yb in enumerate(scan(True)):
            sl = slice(c * CHUNK, (c + 1) * CHUNK)
            y = yf_ref[pl.ds(base + c * CHUNK, CHUNK), :] + yb + d_ref[...] * xs[sl]
            gated = y * _silu(z_ref[0, sl, :])
            ms = jnp.mean(gated * gated, axis=-1, keepdims=True)
            out_ref[0, sl, :] = gated * lax.rsqrt(ms + EPS) * g_ref[...]


def _ssd(ssd_g, cw, cb, dtb, expand, a_exp, d_exp, g, tc):
    b, t, _ = ssd_g.shape
    nt = t // tc
    hb = tc // HALO

    def tile_of(j):
        return jnp.where(j < nt, j, 2 * nt - 1 - j)

    def conv_tile(j):
        return jnp.minimum(j, nt - 1)

    kern = functools.partial(_ssd_kernel, nt=nt, tc=tc)
    return pl.pallas_call(
        kern,
        grid=(b, 2 * nt),
        in_specs=[
            pl.BlockSpec((1, tc, SSD_XBC), lambda i, j: (i, conv_tile(j), 0)),
            pl.BlockSpec((1, HALO, SSD_XBC), lambda i, j: (i, jnp.maximum(conv_tile(j) * hb - 1, 0), 0)),
            pl.BlockSpec((1, HALO, SSD_XBC), lambda i, j: (i, jnp.minimum((conv_tile(j) + 1) * hb, t // HALO - 1), 0)),
            pl.BlockSpec((1, tc, GROUP_W), lambda i, j: (i, tile_of(j), SSD_XBC // GROUP_W)),
            pl.BlockSpec((1, tc, LANE), lambda i, j: (i, tile_of(j), (SSD_XBC + GROUP_W) // LANE)),
            pl.BlockSpec((HALO, SSD_XBC), lambda i, j: (0, 0)),
            pl.BlockSpec((1, SSD_XBC), lambda i, j: (0, 0)),
            pl.BlockSpec((1, LANE), lambda i, j: (0, 0)),
            pl.BlockSpec((1, LANE, GROUP_W), lambda i, j: (j // nt, 0, 0)),
            pl.BlockSpec((1, 1, GROUP_W), lambda i, j: (j // nt, 0, 0)),
            pl.BlockSpec((1, GROUP_W), lambda i, j: (0, 0)),
            pl.BlockSpec((1, GROUP_W), lambda i, j: (0, 0)),
        ],
        out_specs=pl.BlockSpec((1, tc, GROUP_W), lambda i, j: (i, jnp.where(j < nt, nt - 1, 2 * nt - 1 - j), 0)),
        out_shape=jax.ShapeDtypeStruct((b, t, GROUP_W), F32),
        scratch_shapes=[pltpu.VMEM((t, GROUP_W), F32),
                        pltpu.VMEM((t, GROUP_W), F32),
                        pltpu.VMEM((t, 2 * SSD_GROUPS * SSD_STATE), BF16),
                        pltpu.VMEM((SSD_STATE, GROUP_W), F32),
                        pltpu.VMEM((tc + 2 * HALO, SSD_XBC), F32)],
        compiler_params=_params(dimension_semantics=("arbitrary", "arbitrary")),
        name="ssd",
    )(ssd_g, ssd_g, ssd_g, ssd_g, ssd_g, cw, cb, dtb, expand, a_exp, d_exp, g)


GLA_QK_W = GLA_HEADS * GLA_DK
GLA_V_W = GLA_HEADS * GLA_DV


def _gla_kernel(q_ref, k_ref, v_ref, r_ref, lr_ref, w2_ref, gb_ref, g_ref, avg_ref,
                out_ref, of_ref, s_ref, *, nt, tc):
    j = pl.program_id(1)
    tile = jnp.where(j < nt, j, 2 * nt - 1 - j)

    @pl.when((j == 0) | (j == nt))
    def _():
        s_ref[...] = jnp.zeros_like(s_ref)

    base = pl.multiple_of(tile * tc, tc)
    nc = tc // CHUNK
    logit = _dot(lr_ref[0].astype(BF16), w2_ref[0]) + gb_ref[0]
    logg = -_softplus(-logit) * (LOG2E / GLA_TAU)
    q = q_ref[0] * (GLA_DK ** -0.5)
    k = k_ref[0]
    v = v_ref[0]
    qk_head = lax.broadcasted_iota(jnp.int32, (1, GLA_QK_W), 1) // GLA_DK
    v_head = lax.broadcasted_iota(jnp.int32, (1, GLA_V_W), 1) // GLA_DV
    blockdiag = (lax.broadcasted_iota(jnp.int32, (GLA_QK_W, GLA_V_W), 0) // GLA_DK
                 == lax.broadcasted_iota(jnp.int32, (GLA_QK_W, GLA_V_W), 1) // GLA_DV)

    def scan(reverse):
        tri = _tri(reverse)
        tri_bf = tri.astype(BF16)
        last = 0 if reverse else CHUNK - 1
        chunks = range(nc)
        sls = [slice(c * CHUNK, (c + 1) * CHUNK) for c in chunks]
        gcs = [_split_dot(tri_bf, logg[sl]) for sl in sls]
        tot = [g[last:last + 1] for g in gcs]
        dec_col = [jnp.exp2(g.T[:, last:last + 1]) for g in gcs]
        qg = [q[sl] * jnp.exp2(g) for sl, g in zip(sls, gcs)]
        kg = [(k[sl] * jnp.exp2(-g)).astype(BF16) for sl, g in zip(sls, gcs)]
        kd = [(k[sl] * jnp.exp2(t - g)).astype(BF16) for sl, g, t in zip(sls, gcs, tot)]
        q_heads = [jnp.concatenate([jnp.where(qk_head == hd, qg[c], 0.0) for hd in range(GLA_HEADS)], axis=0)
                   for c in chunks]
        raw_all = [_dot_nt(q_heads[c].astype(BF16), kg[c]) for c in chunks]
        raw = [[raw_all[c][hd * CHUNK:(hd + 1) * CHUNK] for hd in range(GLA_HEADS)] for c in chunks]
        st = [jnp.where(blockdiag, _dot_tn(kd[c], v[sls[c]].astype(BF16)), 0.0) for c in chunks]
        att = [[jnp.where(tri, raw[c][hd], 0.0).astype(BF16) for hd in range(GLA_HEADS)] for c in chunks]
        o = []
        for c in chunks:
            acc = None
            for pair in range(GLA_HEADS // 2):
                a_st = jnp.concatenate([att[c][2 * pair], att[c][2 * pair + 1]], axis=1)
                v_st = jnp.concatenate([jnp.where(v_head == 2 * pair + r, v[sls[c]], 0.0) for r in range(2)], axis=0)
                part = _dot(a_st, v_st.astype(BF16))
                acc = part if acc is None else acc + part
            o.append(acc)
        s = s_ref[...]
        s_in = [None] * nc
        for c in (reversed(chunks) if reverse else chunks):
            s_in[c] = s
            s = s * dec_col[c] + st[c]
        s_ref[...] = s
        return [o[c] + _dot(qg[c].astype(BF16), s_in[c].astype(BF16)) for c in chunks]

    @pl.when(j < nt)
    def _():
        for c, o in enumerate(scan(False)):
            of_ref[pl.ds(base + c * CHUNK, CHUNK), :] = o

    @pl.when(j >= nt)
    def _():
        for c, ob in enumerate(scan(True)):
            sl = slice(c * CHUNK, (c + 1) * CHUNK)
            o = of_ref[pl.ds(base + c * CHUNK, CHUNK), :] + ob
            ms = _split_dot_right(o * o, avg_ref[...])
            out_ref[0, sl, :] = o * lax.rsqrt(ms + EPS) * g_ref[...] * _silu(r_ref[0, sl, :])


def _split_dot_right(x, m_bf16):
    x1 = x.astype(BF16)
    r1 = x - x1.astype(F32)
    x2 = r1.astype(BF16)
    x3 = (r1 - x2.astype(F32)).astype(BF16)
    return _dot(x1, m_bf16) + _dot(x2, m_bf16) + _dot(x3, m_bf16)


def _gla(gla_g, w2p, gb, g, avg, tc):
    b, t, _ = gla_g.shape
    nt = t // tc

    def tile_of(j):
        return jnp.where(j < nt, j, 2 * nt - 1 - j)

    kern = functools.partial(_gla_kernel, nt=nt, tc=tc)
    return pl.pallas_call(
        kern,
        grid=(b, 2 * nt),
        in_specs=[
            pl.BlockSpec((1, tc, GLA_QK_W), lambda i, j: (i, tile_of(j), 0)),
            pl.BlockSpec((1, tc, GLA_QK_W), lambda i, j: (i, tile_of(j), 1)),
            pl.BlockSpec((1, tc, GLA_V_W), lambda i, j: (i, tile_of(j), 1)),
            pl.BlockSpec((1, tc, GLA_V_W), lambda i, j: (i, tile_of(j), 2)),
            pl.BlockSpec((1, tc, LANE), lambda i, j: (i, tile_of(j), (2 * GLA_QK_W + 2 * GLA_V_W) // LANE)),
            pl.BlockSpec((1, LANE, GLA_QK_W), lambda i, j: (j // nt, 0, 0)),
            pl.BlockSpec((1, 1, GLA_QK_W), lambda i, j: (j // nt, 0, 0)),
            pl.BlockSpec((1, GLA_V_W), lambda i, j: (0, 0)),
            pl.BlockSpec((GLA_V_W, GLA_V_W), lambda i, j: (0, 0)),
        ],
        out_specs=pl.BlockSpec((1, tc, GLA_V_W), lambda i, j: (i, jnp.where(j < nt, nt - 1, 2 * nt - 1 - j), 0)),
        out_shape=jax.ShapeDtypeStruct((b, t, GLA_V_W), F32),
        scratch_shapes=[pltpu.VMEM((t, GLA_V_W), F32),
                        pltpu.VMEM((GLA_QK_W, GLA_V_W), F32)],
        compiler_params=_params(dimension_semantics=("arbitrary", "arbitrary")),
        name="gla",
    )(gla_g, gla_g, gla_g, gla_g, gla_g, w2p, gb, g, avg)


def _rope(y, cos, sin, half):
    lane = lax.broadcasted_iota(jnp.int32, (1, LANE), 1)
    lo = (lane % (2 * half)) < half
    rot = jnp.where(lo, pltpu.roll(y, LANE - half, 1), pltpu.roll(y, half, 1))
    return y * cos + rot * sin


ONES_LANE = LANE - 1
VT_ROWS = 80
SCORE_BOUND_LIMIT = 48.0

def _finish_keys(k):
    kf = k.astype(BF16).astype(F32)
    norm = jnp.sqrt(jnp.max(jnp.sum(kf * kf, axis=-1, keepdims=True), axis=0, keepdims=True))
    lane = lax.broadcasted_iota(jnp.int32, (1, LANE), 1)
    return jnp.where(lane == ONES_LANE, 1.0, k).astype(BF16), norm


def _update_kmax(kmax_ref, norms):
    @pl.when(pl.program_id(1) == 0)
    def _():
        kmax_ref[...] = jnp.zeros_like(kmax_ref)

    lane = lax.broadcasted_iota(jnp.int32, (1, LANE), 1)
    upd = jnp.zeros((1, LANE), F32)
    for g, norm in enumerate(norms):
        upd = jnp.where(lane == g, norm, upd)
    kmax_ref[0] = jnp.maximum(kmax_ref[0], upd)


def _vt_rows(v, heads, dv):
    tm = v.shape[0]
    vt = v.T.reshape(heads, dv, tm)
    row = lax.broadcasted_iota(jnp.int32, (heads, VT_ROWS - dv, tm), 1)
    return jnp.concatenate([vt, jnp.where(row == 0, 1.0, 0.0)], axis=1).astype(BF16)


def _attn_kernel(q_ref, k_ref, vt_ref, kmax_ref, o_ref, qa_ref, m_ref, acc_ref, *, n_q, rep, tk, dv):
    nk = k_ref.shape[1] // tk
    lane = lax.broadcasted_iota(jnp.int32, (1, LANE), 1)
    kmax = kmax_ref[0, 0:1, :]
    bound_max = None
    for h in range(n_q):
        qf = q_ref[0, :, h * LANE:(h + 1) * LANE].astype(F32)
        kg = jnp.max(jnp.where(lane == h // rep, kmax, 0.0), axis=-1, keepdims=True)
        bound = jnp.sqrt(jnp.sum(qf * qf, axis=-1, keepdims=True)) * kg
        qa_ref[h] = jnp.where(lane == ONES_LANE, -bound, qf).T.astype(BF16)
        top = jnp.max(bound)
        bound_max = top if bound_max is None else jnp.maximum(bound_max, top)
    acc_ref[...] = jnp.zeros(acc_ref.shape, F32)

    @pl.when(bound_max <= SCORE_BOUND_LIMIT)
    def _():
        def scores(i, h):
            ks = pl.multiple_of(i * tk, tk)
            g = h // rep
            return _dot(k_ref[0, pl.ds(ks, tk), g * LANE:(g + 1) * LANE], qa_ref[h])

        group = max(u for u in (1, 2, 4, 8) if nk % u == 0)

        def body(ii, carry):
            steps = [(ii * group + u, h) for u in range(group) for h in range(n_q)]
            s_next = scores(*steps[0])
            for n, (i, h) in enumerate(steps):
                s = s_next
                if n + 1 < len(steps):
                    s_next = scores(*steps[n + 1])
                p = jnp.exp2(s)
                acc_ref[h, 0:dv, :] += _dot(vt_ref[0, h // rep, i, 0:dv, :], p.astype(BF16))
                acc_ref[h, dv:dv + 8, :] += jnp.sum(p.reshape(tk // 8, 8, p.shape[1]), axis=0)
            return carry

        lax.fori_loop(0, nk // group, body, 0)

    @pl.when(bound_max > SCORE_BOUND_LIMIT)
    def _():
        m_ref[...] = jnp.full(m_ref.shape, -jnp.inf, F32)

        def body(i, carry):
            ks = pl.multiple_of(i * tk, tk)
            for h in range(n_q):
                g = h // rep
                s = _dot_nt(k_ref[0, pl.ds(ks, tk), g * LANE:(g + 1) * LANE], q_ref[0, :, h * LANE:(h + 1) * LANE])
                m_old = m_ref[h]
                m_new = jnp.maximum(m_old, jnp.max(s, axis=0, keepdims=True))
                p = jnp.exp2(s - m_new)
                acc_ref[h] = acc_ref[h] * jnp.exp2(m_old - m_new) + _dot(vt_ref[0, g, i], p.astype(BF16))
                m_ref[h] = m_new
            return carry

        lax.fori_loop(0, nk, body, 0)

    o = jnp.concatenate([acc_ref[h, 0:dv, :] * (1.0 / jnp.sum(acc_ref[h, dv:dv + 8, :], axis=0, keepdims=True))
                         for h in range(n_q)], axis=0)
    o_ref[0] = o.T


def _attention(q, k, vt, kmax, rep, tq, tk, dv):
    b, t, qw = q.shape
    kw = k.shape[2]
    n_q = qw // LANE
    n_kv = n_q // rep
    ow = n_q * dv
    kern = functools.partial(_attn_kernel, n_q=n_q, rep=rep, tk=tk, dv=dv)
    return pl.pallas_call(
        kern,
        grid=(b, t // tq),
        in_specs=[
            pl.BlockSpec((1, tq, qw), lambda i, j: (i, j, 0)),
            pl.BlockSpec((1, t, kw), lambda i, j: (i, 0, 0)),
            pl.BlockSpec((1, n_kv, t // tk, VT_ROWS, tk), lambda i, j: (i, 0, 0, 0, 0)),
            pl.BlockSpec((1, 8, LANE), lambda i, j: (i, 0, 0)),
        ],
        out_specs=pl.BlockSpec((1, tq, ow), lambda i, j: (i, j, 0)),
        out_shape=jax.ShapeDtypeStruct((b, t, ow), F32),
        scratch_shapes=[pltpu.VMEM((n_q, LANE, tq), BF16), pltpu.VMEM((n_q, 1, tq), F32),
                        pltpu.VMEM((n_q, VT_ROWS, tq), F32)],
        compiler_params=_params(dimension_semantics=("arbitrary", "arbitrary")),
        name="attention",
    )(q, k, vt, kmax)


FF_CHUNK = 1024


def _outmlp_kernel(x_ref, m0_ref, m1_ref, m2_ref, m3_ref, wo_ref, g2_ref, w1_ref, w2_ref, gf_ref, o_ref, *, final):
    x1 = x_ref[...]
    for i, m_ref in enumerate((m0_ref, m1_ref, m2_ref, m3_ref)):
        x1 = x1 + _dot(m_ref[...].astype(BF16), wo_ref[i * GROUP_W:(i + 1) * GROUP_W, :])
    ms = jnp.mean(x1 * x1, axis=-1, keepdims=True)
    h = (x1 * lax.rsqrt(ms + EPS) * g2_ref[...]).astype(BF16)
    acc = None
    for c in range(D_FF // FF_CHUNK):
        sl = slice(c * FF_CHUNK, (c + 1) * FF_CHUNK)
        u = jnp.maximum(_dot(h, w1_ref[:, sl]), 0.0)
        part = _dot((u * u).astype(BF16), w2_ref[sl, :])
        acc = part if acc is None else acc + part
    y = x1 + acc
    if final:
        ms = jnp.mean(y * y, axis=-1, keepdims=True)
        y = y * lax.rsqrt(ms + EPS) * gf_ref[...]
    o_ref[...] = y


def _outmlp(x2d, mixes, wo, g2, w1, w2, gf, final, tm):
    n = x2d.shape[0]
    const = lambda i: (0, 0)
    single = pl.Buffered(1)
    kern = functools.partial(_outmlp_kernel, final=final)
    return pl.pallas_call(
        kern,
        grid=(n // tm,),
        in_specs=[pl.BlockSpec((tm, D_MODEL), lambda i: (i, 0))]
                 + [pl.BlockSpec((tm, GROUP_W), lambda i: (i, 0)) for _ in range(4)]
                 + [pl.BlockSpec((D_MODEL, D_MODEL), const, pipeline_mode=single),
                    pl.BlockSpec((1, D_MODEL), const),
                    pl.BlockSpec((D_MODEL, D_FF), const, pipeline_mode=single),
                    pl.BlockSpec((D_FF, D_MODEL), const, pipeline_mode=single),
                    pl.BlockSpec((1, D_MODEL), const)],
        out_specs=pl.BlockSpec((tm, D_MODEL), lambda i: (i, 0)),
        out_shape=jax.ShapeDtypeStruct((n, D_MODEL), F32),
        compiler_params=_params(dimension_semantics=("arbitrary",)),
        name="outmlp",
    )(x2d, *mixes, wo, g2.reshape(1, D_MODEL), w1, w2, gf.reshape(1, D_MODEL))


def _rope_tables(t):
    pos = np.arange(t, dtype=np.int32)
    row = (pos // GRID_W).astype(np.float32)
    col = (pos % GRID_W).astype(np.float32)

    def block(p, n):
        inv_freq = np.float32(ROPE_THETA) ** (-np.arange(n, dtype=np.float32) / np.float32(n))
        ang = p[:, None] * inv_freq[None, :].astype(np.float32)
        c, s = np.cos(ang).astype(np.float32), np.sin(ang).astype(np.float32)
        return np.concatenate([c, c], axis=1), np.concatenate([-s, s], axis=1)

    def table(n, lead):
        cr, sr = block(row, n)
        cc, sc = block(col, n)
        tail = LANE - lead - 4 * n
        cos = np.concatenate([np.ones((t, lead), np.float32), cr, cc, np.ones((t, tail), np.float32)], axis=1)
        sin = np.concatenate([np.zeros((t, lead), np.float32), sr, sc, np.zeros((t, tail), np.float32)], axis=1)
        return jnp.asarray(cos), jnp.asarray(sin)

    return table(GQA_HEAD_DIM // 4, 0), table(MLA_ROPE // 4, MLA_NOPE)


def _pad_lanes(v, width=LANE):
    return jnp.concatenate([v, jnp.zeros((width - v.shape[0],), v.dtype)]).reshape(1, width)


def _layer_params(i, p):
    (norm1_g, w_in, ssd_conv_w, ssd_conv_b, ssd_dt_bias, ssd_a_log, ssd_d, ssd_norm_g,
     gqa_q_norm_g, gqa_k_norm_g, gla_gate_w2, gla_gate_b, gla_norm_g, mla_q_norm_g, mla_w_uq,
     mla_kv_norm_g, mla_w_ukv, w_out, norm2_g, w_ff1, w_ff2) = [a[i] for a in p]
    out = {}
    out["norm1_g"] = norm1_g
    out["w_all"] = _gather_cols(w_in, _IN_COLS).astype(BF16)
    out["conv_w"] = jnp.concatenate([ssd_conv_w, jnp.zeros((HALO - SSD_CONV_W, SSD_XBC), F32)], axis=0)
    out["conv_b"] = ssd_conv_b.reshape(1, SSD_XBC)
    expand = lambda a: jnp.repeat(a, SSD_HEAD_DIM, axis=-1)
    out["dt_bias"] = _pad_lanes(ssd_dt_bias.reshape(2 * SSD_HEADS))
    spread = np.zeros((2, LANE, GROUP_W), np.float32)
    for d in range(2):
        for h in range(SSD_HEADS):
            spread[d, d * SSD_HEADS + h, h * SSD_HEAD_DIM:(h + 1) * SSD_HEAD_DIM] = 1.0
    out["dt_spread"] = jnp.asarray(spread, BF16)
    out["a_neg"] = expand(-jnp.exp(ssd_a_log) * LOG2E).reshape(2, 1, GROUP_W)
    out["ssd_d"] = expand(ssd_d).reshape(1, GROUP_W)
    out["ssd_norm_g"] = ssd_norm_g.reshape(1, GROUP_W)
    out["gq"] = _pad_lanes(gqa_q_norm_g)
    out["gk"] = _pad_lanes(gqa_k_norm_g)
    w2p = jnp.zeros((2, LANE, GLA_QK_W), F32)
    for d in range(2):
        w2p = w2p.at[d, d * GLA_LOWRANK:(d + 1) * GLA_LOWRANK, :].set(gla_gate_w2[d])
    out["gla_w2"] = w2p.astype(BF16)
    out["gla_b"] = gla_gate_b.reshape(2, 1, GLA_QK_W)
    out["gla_norm_g"] = jnp.tile(gla_norm_g, GLA_HEADS).reshape(1, GLA_V_W)
    out["mla_gq"] = mla_q_norm_g.reshape(1, MLA_Q_LORA)
    out["mla_gkv"] = mla_kv_norm_g.reshape(1, MLA_KV_LORA)
    uq_cols, uk_cols, uv_cols = [], [], []
    for h in range(MLA_HEADS):
        uq_cols += list(range(h * MLA_QK, (h + 1) * MLA_QK)) + [-1] * (LANE - MLA_QK)
        base = h * (MLA_NOPE + MLA_V)
        uk_cols += list(range(base, base + MLA_NOPE)) + [-1] * (LANE - MLA_NOPE)
        uv_cols += list(range(base + MLA_NOPE, base + MLA_NOPE + MLA_V))
    out["wuq"] = _gather_cols(mla_w_uq, np.asarray(uq_cols, np.int32)).astype(BF16)
    out["wuk"] = _gather_cols(mla_w_ukv, np.asarray(uk_cols, np.int32)).astype(BF16)
    out["wuv"] = _gather_cols(mla_w_ukv, np.asarray(uv_cols, np.int32)).astype(BF16)
    out["w_out"] = w_out.astype(BF16)
    out["norm2_g"] = norm2_g
    out["w_ff1"] = w_ff1.astype(BF16)
    out["w_ff2"] = w_ff2.astype(BF16)
    return out


TILE_PREF = dict(tc=1024, tp=512, tq_gqa=512, tq_mla=512, tmlp=512)


def _tiles(t):
    pick = lambda pref: max(c for c in (128, 256, 512, 1024) if c <= pref and t % c == 0)
    return {name: pick(pref) for name, pref in TILE_PREF.items()}


def _trunk(x, layers, final_norm_g):
    b, t, d = x.shape
    n = b * t
    ts = _tiles(t)
    tk = ts["tp"]
    tables = _rope_tables(t)
    avg = jnp.asarray(np.kron(np.eye(GLA_HEADS), np.full((GLA_DV, GLA_DV), 1.0 / GLA_DV)), BF16)
    x2d = x.reshape(n, d)
    for i, lp in enumerate(layers):
        ssd_g, gla_g, gq, gk, gvt, gkmax, mq, mk, mvt, mkmax = _inproj(x2d.reshape(b, t, d), lp, tables, tk)
        ssd_out = _ssd(ssd_g, lp["conv_w"], lp["conv_b"], lp["dt_bias"], lp["dt_spread"], lp["a_neg"],
                       lp["ssd_d"], lp["ssd_norm_g"], ts["tc"])
        gqa_out = _attention(gq, gk, gvt, gkmax, GQA_HEADS // GQA_KV_HEADS, ts["tq_gqa"], tk, GQA_HEAD_DIM)
        gla_out = _gla(gla_g, lp["gla_w2"], lp["gla_b"], lp["gla_norm_g"], avg, ts["tc"])
        mla_out = _attention(mq, mk, mvt, mkmax, 1, ts["tq_mla"], tk, MLA_V)
        mixes = [m.reshape(n, GROUP_W) for m in (ssd_out, gqa_out, gla_out, mla_out)]
        x2d = _outmlp(x2d, mixes, lp["w_out"], lp["norm2_g"], lp["w_ff1"], lp["w_ff2"], final_norm_g,
                      i == len(layers) - 1, ts["tmlp"])
    return x2d.reshape(b, t, d)


def kernel(x_prompt, x_sample, norm1_g, w_in, ssd_conv_w, ssd_conv_b, ssd_dt_bias, ssd_a_log, ssd_d, ssd_norm_g,
           gqa_q_norm_g, gqa_k_norm_g, gla_gate_w2, gla_gate_b, gla_norm_g, mla_q_norm_g, mla_w_uq,
           mla_kv_norm_g, mla_w_ukv, w_out, norm2_g, w_ff1, w_ff2, final_norm_g):
    stacked = (norm1_g, w_in, ssd_conv_w, ssd_conv_b, ssd_dt_bias, ssd_a_log, ssd_d, ssd_norm_g,
               gqa_q_norm_g, gqa_k_norm_g, gla_gate_w2, gla_gate_b, gla_norm_g, mla_q_norm_g, mla_w_uq,
               mla_kv_norm_g, mla_w_ukv, w_out, norm2_g, w_ff1, w_ff2)
    layers = [_layer_params(i, stacked) for i in range(norm1_g.shape[0])]
    return (_trunk(x_prompt, layers, final_norm_g), _trunk(x_sample, layers, final_norm_g))
```

```python
import functools

import numpy as np
import jax
import jax.numpy as jnp
from jax import lax
from jax.experimental import pallas as pl
from jax.experimental.pallas import tpu as pltpu

F32 = jnp.float32
BF16 = jnp.bfloat16

D_MODEL = 1024
DEPTH = 2
GRID_W = 64
CHUNK = 128
ROPE_THETA = 10000.0
EPS = 1e-6
GROUP_W = D_MODEL // 4
D_FF = 4 * D_MODEL

SSD_HEADS = 4
SSD_HEAD_DIM = GROUP_W // SSD_HEADS
SSD_GROUPS = 2
SSD_STATE = 128
SSD_CONV_W = 5
SSD_XBC = GROUP_W + 2 * SSD_GROUPS * SSD_STATE

GQA_HEADS = 4
GQA_KV_HEADS = 2
GQA_HEAD_DIM = GROUP_W // GQA_HEADS

GLA_HEADS = 4
GLA_DV = GROUP_W // GLA_HEADS
GLA_DK = GLA_DV // 2
GLA_LOWRANK = 16
GLA_TAU = 16.0

MLA_HEADS = 4
MLA_Q_LORA = 256
MLA_KV_LORA = 128
MLA_NOPE = 64
MLA_ROPE = 32
MLA_V = GROUP_W // MLA_HEADS
MLA_QK = MLA_NOPE + MLA_ROPE

IN_SIZES = (GROUP_W, SSD_XBC, 2 * SSD_HEADS,
            GQA_HEADS * GQA_HEAD_DIM, GQA_KV_HEADS * GQA_HEAD_DIM, GQA_KV_HEADS * GQA_HEAD_DIM,
            GLA_HEADS * GLA_DK, GLA_HEADS * GLA_DK, GLA_HEADS * GLA_DV, GLA_HEADS * GLA_DV, 2 * GLA_LOWRANK,
            MLA_Q_LORA, MLA_KV_LORA, MLA_ROPE)
IN_OFFS = tuple(int(v) for v in np.concatenate([[0], np.cumsum(IN_SIZES)]))

LOG2E = 1.4426950408889634
LANE = 128
HALO = 8
VMEM_LIMIT = 52 * 1024 * 1024

SSD_W = SSD_XBC + GROUP_W + LANE
GQA_W = GQA_HEADS * LANE + GQA_KV_HEADS * LANE + GQA_KV_HEADS * GQA_HEAD_DIM
GLA_W = 2 * GLA_HEADS * GLA_DK + 2 * GLA_HEADS * GLA_DV + LANE
MLA_W = MLA_Q_LORA + MLA_KV_LORA + LANE
PROJ_W = SSD_W + GQA_W + GLA_W + MLA_W


def _in_proj_columns():
    o = IN_OFFS
    cols = []
    cols += list(range(o[1], o[2]))
    cols += list(range(o[0], o[1]))
    cols += list(range(o[2], o[3])) + [-1] * (LANE - 2 * SSD_HEADS)
    for h in range(GQA_HEADS):
        cols += list(range(o[3] + h * GQA_HEAD_DIM, o[3] + (h + 1) * GQA_HEAD_DIM)) + [-1] * (LANE - GQA_HEAD_DIM)
    for h in range(GQA_KV_HEADS):
        cols += list(range(o[4] + h * GQA_HEAD_DIM, o[4] + (h + 1) * GQA_HEAD_DIM)) + [-1] * (LANE - GQA_HEAD_DIM)
    cols += list(range(o[5], o[6]))
    cols += list(range(o[6], o[10]))
    cols += list(range(o[10], o[11])) + [-1] * (LANE - 2 * GLA_LOWRANK)
    cols += list(range(o[11], o[13]))
    cols += [-1] * MLA_NOPE + list(range(o[13], o[14])) + [-1] * (LANE - MLA_QK)
    cols = np.asarray(cols, np.int32)
    assert cols.shape[0] == PROJ_W
    return cols


_IN_COLS = _in_proj_columns()


def _gather_cols(w, cols):
    picked = jnp.take(w, jnp.asarray(np.maximum(cols, 0)), axis=1)
    return jnp.where(jnp.asarray(cols >= 0)[None, :], picked, 0.0)


def _params(**kw):
    return pltpu.CompilerParams(vmem_limit_bytes=VMEM_LIMIT, **kw)


def _silu(x):
    return x * (1.0 / (1.0 + jnp.exp(-x)))


def _softplus(x):
    return jnp.maximum(x, 0.0) + jnp.log1p(jnp.exp(-jnp.abs(x)))


def _dot(a, b):
    return jnp.dot(a, b, preferred_element_type=F32)


def _dot_nt(a, b):
    return lax.dot_general(a, b, (((1,), (1,)), ((), ())), preferred_element_type=F32)


def _dot_tn(a, b):
    return lax.dot_general(a, b, (((0,), (0,)), ((), ())), preferred_element_type=F32)


def _split_dot(m_bf16, x):
    x1 = x.astype(BF16)
    r1 = x - x1.astype(F32)
    x2 = r1.astype(BF16)
    x3 = (r1 - x2.astype(F32)).astype(BF16)
    return _dot(m_bf16, x1) + _dot(m_bf16, x2) + _dot(m_bf16, x3)


def _tri(reverse):
    li = lax.broadcasted_iota(jnp.int32, (CHUNK, CHUNK), 0)
    si = lax.broadcasted_iota(jnp.int32, (CHUNK, CHUNK), 1)
    return (si >= li) if reverse else (si <= li)


def _inproj_kernel(x_ref, g_ref, w_ref, cosg_ref, sing_ref, cosm_ref, sinm_ref, gq_ref, gk_ref,
                   mgq_ref, mgkv_ref, wuq_ref, wuk_ref, wuv_ref,
                   ssd_ref, gla_ref, gq_out, gk_out, gvt_out, gkmax_out, mq_out, mk_out, mvt_out, mkmax_out):
    x = x_ref[0]
    ms = jnp.mean(x * x, axis=-1, keepdims=True)
    h = (x * lax.rsqrt(ms + EPS) * g_ref[...]).astype(BF16)
    c_gqa, c_gla, c_mla = SSD_W, SSD_W + GQA_W, SSD_W + GQA_W + GLA_W
    mla = _dot(h, w_ref[:, c_mla:PROJ_W])
    gqa = _dot(h, w_ref[:, c_gqa:c_gla])

    def rms(y, g):
        ms_y = jnp.mean(y * y, axis=-1, keepdims=True)
        return (y * lax.rsqrt(ms_y + EPS) * g).astype(BF16)

    cq = rms(mla[:, :MLA_Q_LORA], mgq_ref[...])
    ckv = rms(mla[:, MLA_Q_LORA:MLA_Q_LORA + MLA_KV_LORA], mgkv_ref[...])
    ssd_ref[0] = _dot(h, w_ref[:, 0:SSD_W])
    q = _dot(cq, wuq_ref[...])
    kn = _dot(ckv, wuk_ref[...])
    v = _dot(ckv, wuv_ref[...])

    cos, sin = cosg_ref[...], sing_ref[...]
    qw, kw = GQA_HEADS * LANE, GQA_KV_HEADS * LANE
    heads = [gqa[:, hd * LANE:(hd + 1) * LANE] for hd in range(GQA_HEADS + GQA_KV_HEADS)]
    gains = [gq_ref[...]] * GQA_HEADS + [gk_ref[...]] * GQA_KV_HEADS
    inv = [lax.rsqrt(jnp.sum(y * y, axis=-1, keepdims=True) * (1.0 / GQA_HEAD_DIM) + EPS) for y in heads]
    roped = [_rope(y * r * g, cos, sin, GQA_HEAD_DIM // 4) for y, r, g in zip(heads, inv, gains)]
    for hd in range(GQA_HEADS):
        gq_out[0, :, hd * LANE:(hd + 1) * LANE] = (roped[hd] * (GQA_HEAD_DIM ** -0.5 * LOG2E)).astype(BF16)
    norms = []
    for hd in range(GQA_KV_HEADS):
        gk_out[0, :, hd * LANE:(hd + 1) * LANE], norm = _finish_keys(roped[GQA_HEADS + hd])
        norms.append(norm)
    _update_kmax(gkmax_out, norms)
    gvt_out[0, :, 0] = _vt_rows(gqa[:, qw + kw:], GQA_KV_HEADS, GQA_HEAD_DIM)
    gla_ref[0] = _dot(h, w_ref[:, c_gla:c_mla])

    cos, sin = cosm_ref[...], sinm_ref[...]
    k_rope = _rope(mla[:, MLA_Q_LORA + MLA_KV_LORA:], cos, sin, MLA_ROPE // 4)
    q_roped = [_rope(q[:, hd * LANE:(hd + 1) * LANE], cos, sin, MLA_ROPE // 4) for hd in range(MLA_HEADS)]
    norms = []
    for hd in range(MLA_HEADS):
        sl = slice(hd * LANE, (hd + 1) * LANE)
        mq_out[0, :, sl] = (q_roped[hd] * (MLA_QK ** -0.5 * LOG2E)).astype(BF16)
        mk_out[0, :, sl], norm = _finish_keys(kn[:, sl] + k_rope)
        norms.append(norm)
    _update_kmax(mkmax_out, norms)
    mvt_out[0, :, 0] = _vt_rows(v, MLA_HEADS, MLA_V)


def _inproj(x, lp, tables, tm):
    b, t, _ = x.shape
    (cos_g, sin_g), (cos_m, sin_m) = tables
    const = lambda i, j: (0, 0)
    tok = lambda i, j: (i, j, 0)
    tab = pl.BlockSpec((tm, LANE), lambda i, j: (j, 0))
    qw, kw, mw = GQA_HEADS * LANE, GQA_KV_HEADS * LANE, MLA_HEADS * LANE

    def vt_spec(heads):
        return pl.BlockSpec((1, heads, 1, VT_ROWS, tm), lambda i, j: (i, 0, j, 0, 0))

    kmax_spec = pl.BlockSpec((1, 8, LANE), lambda i, j: (i, 0, 0))
    return pl.pallas_call(
        _inproj_kernel,
        grid=(b, t // tm),
        in_specs=[pl.BlockSpec((1, tm, D_MODEL), tok),
                  pl.BlockSpec((1, D_MODEL), const),
                  pl.BlockSpec((D_MODEL, PROJ_W), const, pipeline_mode=pl.Buffered(1)),
                  tab, tab, tab, tab,
                  pl.BlockSpec((1, LANE), const), pl.BlockSpec((1, LANE), const),
                  pl.BlockSpec((1, MLA_Q_LORA), const), pl.BlockSpec((1, MLA_KV_LORA), const),
                  pl.BlockSpec((MLA_Q_LORA, mw), const), pl.BlockSpec((MLA_KV_LORA, mw), const),
                  pl.BlockSpec((MLA_KV_LORA, MLA_HEADS * MLA_V), const)],
        out_specs=[pl.BlockSpec((1, tm, SSD_W), tok), pl.BlockSpec((1, tm, GLA_W), tok),
                   pl.BlockSpec((1, tm, qw), tok), pl.BlockSpec((1, tm, kw), tok), vt_spec(GQA_KV_HEADS), kmax_spec,
                   pl.BlockSpec((1, tm, mw), tok), pl.BlockSpec((1, tm, mw), tok), vt_spec(MLA_HEADS), kmax_spec],
        out_shape=[jax.ShapeDtypeStruct((b, t, SSD_W), F32), jax.ShapeDtypeStruct((b, t, GLA_W), F32),
                   jax.ShapeDtypeStruct((b, t, qw), BF16), jax.ShapeDtypeStruct((b, t, kw), BF16),
                   jax.ShapeDtypeStruct((b, GQA_KV_HEADS, t // tm, VT_ROWS, tm), BF16),
                   jax.ShapeDtypeStruct((b, 8, LANE), F32),
                   jax.ShapeDtypeStruct((b, t, mw), BF16), jax.ShapeDtypeStruct((b, t, mw), BF16),
                   jax.ShapeDtypeStruct((b, MLA_HEADS, t // tm, VT_ROWS, tm), BF16),
                   jax.ShapeDtypeStruct((b, 8, LANE), F32)],
        compiler_params=_params(dimension_semantics=("arbitrary", "arbitrary")),
        name="inproj",
    )(x, lp["norm1_g"].reshape(1, D_MODEL), lp["w_all"], cos_g, sin_g, cos_m, sin_m, lp["gq"], lp["gk"],
      lp["mla_gq"], lp["mla_gkv"], lp["wuq"], lp["wuk"], lp["wuv"])


def _ssd_kernel(xbc_ref, prev_ref, next_ref, z_ref, dtx_ref, cw_ref, cb_ref, dtb_ref, ex_ref, a_ref, d_ref, g_ref,
                out_ref, yf_ref, xs_ref, bc_ref, h_ref, xe_ref, *, nt, tc):
    j = pl.program_id(1)
    tile = jnp.where(j < nt, j, 2 * nt - 1 - j)
    base = pl.multiple_of(tile * tc, tc)
    nc = tc // CHUNK
    half = GROUP_W // SSD_GROUPS

    @pl.when((j == 0) | (j == nt))
    def _():
        h_ref[...] = jnp.zeros_like(h_ref)

    @pl.when(j < nt)
    def _():
        xe_ref[HALO:HALO + tc, :] = xbc_ref[0]
        xe_ref[0:HALO, :] = jnp.where(tile > 0, prev_ref[0], 0.0)
        xe_ref[HALO + tc:2 * HALO + tc, :] = jnp.where(tile < nt - 1, next_ref[0], 0.0)
        pad = SSD_CONV_W // 2
        window = xe_ref[...]
        acc = cb_ref[...] + cw_ref[pad:pad + 1, :] * window[HALO:HALO + tc]
        for w in range(SSD_CONV_W):
            if w != pad:
                shifted = pltpu.roll(window, (pad - w) % (tc + 2 * HALO), 0)
                acc = acc + cw_ref[w:w + 1, :] * shifted[HALO:HALO + tc]
        xbc = _silu(acc)
        xs_ref[pl.ds(base, tc), :] = xbc[:, :GROUP_W]
        bc_ref[pl.ds(base, tc), :] = xbc[:, GROUP_W:].astype(BF16)

    xs = xs_ref[pl.ds(base, tc), :]
    dt_heads = _softplus(dtx_ref[0] + dtb_ref[...])
    dt = _split_dot_right(dt_heads, ex_ref[0])
    a_dt = dt * a_ref[0]
    xdt = xs * dt
    lane_head = lax.broadcasted_iota(jnp.int32, (1, GROUP_W), 1) // SSD_HEAD_DIM

    def scan(reverse):
        tri = _tri(reverse)
        tri_bf = tri.astype(BF16)
        chunks = range(nc)
        groups = range(SSD_GROUPS)
        per_group = SSD_HEADS // SSD_GROUPS
        sls = [slice(c * CHUNK, (c + 1) * CHUNK) for c in chunks]
        rows = [pl.ds(base + c * CHUNK, CHUNK) for c in chunks]
        acs = [_split_dot(tri_bf, a_dt[sl]) for sl in sls]
        b_c = [[bc_ref[rows[c], g * SSD_STATE:(g + 1) * SSD_STATE] for g in groups] for c in chunks]
        c_c = [[bc_ref[rows[c], (SSD_GROUPS + g) * SSD_STATE:(SSD_GROUPS + g + 1) * SSD_STATE] for g in groups]
               for c in chunks]
        scores = [[_dot_nt(c_c[c][g], b_c[c][g]) for g in groups] for c in chunks]
        tot = [a[0:1] if reverse else a[CHUNK - 1:CHUNK] for a in acs]
        acs_t = [a.T for a in acs]
        x_d = [(xdt[sls[c]] * jnp.exp2(tot[c] - acs[c])).astype(BF16) for c in chunks]
        st = [jnp.concatenate([_dot_tn(b_c[c][g], x_d[c][:, g * half:(g + 1) * half]) for g in groups], axis=1)
              for c in chunks]
        weights = []
        for c in chunks:
            w_c = []
            for hd in range(SSD_HEADS):
                col = jnp.broadcast_to(acs[c][:, hd * SSD_HEAD_DIM:hd * SSD_HEAD_DIM + 1], (CHUNK, CHUNK))
                row = jnp.broadcast_to(acs_t[c][hd * SSD_HEAD_DIM:hd * SSD_HEAD_DIM + 1, :], (CHUNK, CHUNK))
                decay = jnp.exp2(jnp.where(tri, col - row, -jnp.inf))
                w_c.append((scores[c][hd // per_group] * decay).astype(BF16))
            weights.append(w_c)
        y = []
        for c in chunks:
            parts = []
            for g in groups:
                x_g = xdt[sls[c], g * half:(g + 1) * half]
                own = lane_head[:, g * half:(g + 1) * half]
                w_g = jnp.concatenate([weights[c][g * per_group + r] for r in range(per_group)], axis=1)
                x_st = jnp.concatenate([jnp.where(own == g * per_group + r, x_g, 0.0) for r in range(per_group)], axis=0)
                parts.append(_dot(w_g, x_st.astype(BF16)))
            y.append(jnp.concatenate(parts, axis=1))
        h = h_ref[...]
        h_in = [None] * nc
        for c in (reversed(chunks) if reverse else chunks):
            h_in[c] = h
            h = h * jnp.exp2(tot[c]) + st[c]
        h_ref[...] = h
        y_off = [jnp.concatenate([_dot(c_c[c][g], h_in[c][:, g * half:(g + 1) * half].astype(BF16)) for g in groups],
                                 axis=1) for c in chunks]
        return [y[c] + y_off[c] * jnp.exp2(acs[c]) for c in chunks]

    @pl.when(j < nt)
    def _():
        for c, y in enumerate(scan(False)):
            yf_ref[pl.ds(base + c * CHUNK, CHUNK), :] = y

    @pl.when(j >= nt)
    def _():
        for c, yb in enumerate(scan(True)):
            sl = slice(c * CHUNK, (c + 1) * CHUNK)
            y = yf_ref[pl.ds(base + c * CHUNK, CHUNK), :] + yb + d_ref[...] * xs[sl]
            gated = y * _silu(z_ref[0, sl, :])
            ms = jnp.mean(gated * gated, axis=-1, keepdims=True)
            out_ref[0, sl, :] = gated * lax.rsqrt(ms + EPS) * g_ref[...]


def _ssd(ssd_g, cw, cb, dtb, expand, a_exp, d_exp, g, tc):
    b, t, _ = ssd_g.shape
    nt = t // tc
    hb = tc // HALO

    def tile_of(j):
        return jnp.where(j < nt, j, 2 * nt - 1 - j)

    def conv_tile(j):
        return jnp.minimum(j, nt - 1)

    kern = functools.partial(_ssd_kernel, nt=nt, tc=tc)
    return pl.pallas_call(
        kern,
        grid=(b, 2 * nt),
        in_specs=[
            pl.BlockSpec((1, tc, SSD_XBC), lambda i, j: (i, conv_tile(j), 0)),
            pl.BlockSpec((1, HALO, SSD_XBC), lambda i, j: (i, jnp.maximum(conv_tile(j) * hb - 1, 0), 0)),
            pl.BlockSpec((1, HALO, SSD_XBC), lambda i, j: (i, jnp.minimum((conv_tile(j) + 1) * hb, t // HALO - 1), 0)),
            pl.BlockSpec((1, tc, GROUP_W), lambda i, j: (i, tile_of(j), SSD_XBC // GROUP_W)),
            pl.BlockSpec((1, tc, LANE), lambda i, j: (i, tile_of(j), (SSD_XBC + GROUP_W) // LANE)),
            pl.BlockSpec((HALO, SSD_XBC), lambda i, j: (0, 0)),
            pl.BlockSpec((1, SSD_XBC), lambda i, j: (0, 0)),
            pl.BlockSpec((1, LANE), lambda i, j: (0, 0)),
            pl.BlockSpec((1, LANE, GROUP_W), lambda i, j: (j // nt, 0, 0)),
            pl.BlockSpec((1, 1, GROUP_W), lambda i, j: (j // nt, 0, 0)),
            pl.BlockSpec((1, GROUP_W), lambda i, j: (0, 0)),
            pl.BlockSpec((1, GROUP_W), lambda i, j: (0, 0)),
        ],
        out_specs=pl.BlockSpec((1, tc, GROUP_W), lambda i, j: (i, jnp.where(j < nt, nt - 1, 2 * nt - 1 - j), 0)),
        out_shape=jax.ShapeDtypeStruct((b, t, GROUP_W), F32),
        scratch_shapes=[pltpu.VMEM((t, GROUP_W), F32),
                        pltpu.VMEM((t, GROUP_W), F32),
                        pltpu.VMEM((t, 2 * SSD_GROUPS * SSD_STATE), BF16),
                        pltpu.VMEM((SSD_STATE, GROUP_W), F32),
                        pltpu.VMEM((tc + 2 * HALO, SSD_XBC), F32)],
        compiler_params=_params(dimension_semantics=("arbitrary", "arbitrary")),
        name="ssd",
    )(ssd_g, ssd_g, ssd_g, ssd_g, ssd_g, cw, cb, dtb, expand, a_exp, d_exp, g)


GLA_QK_W = GLA_HEADS * GLA_DK
GLA_V_W = GLA_HEADS * GLA_DV


def _gla_kernel(q_ref, k_ref, v_ref, r_ref, lr_ref, w2_ref, gb_ref, g_ref, avg_ref,
                out_ref, of_ref, s_ref, *, nt, tc):
    j = pl.program_id(1)
    tile = jnp.where(j < nt, j, 2 * nt - 1 - j)

    @pl.when((j == 0) | (j == nt))
    def _():
        s_ref[...] = jnp.zeros_like(s_ref)

    base = pl.multiple_of(tile * tc, tc)
    nc = tc // CHUNK
    logit = _dot(lr_ref[0].astype(BF16), w2_ref[0]) + gb_ref[0]
    logg = -_softplus(-logit) * (LOG2E / GLA_TAU)
    q = q_ref[0] * (GLA_DK ** -0.5)
    k = k_ref[0]
    v = v_ref[0]
    qk_head = lax.broadcasted_iota(jnp.int32, (1, GLA_QK_W), 1) // GLA_DK
    v_head = lax.broadcasted_iota(jnp.int32, (1, GLA_V_W), 1) // GLA_DV
    blockdiag = (lax.broadcasted_iota(jnp.int32, (GLA_QK_W, GLA_V_W), 0) // GLA_DK
                 == lax.broadcasted_iota(jnp.int32, (GLA_QK_W, GLA_V_W), 1) // GLA_DV)

    def scan(reverse):
        tri = _tri(reverse)
        tri_bf = tri.astype(BF16)
        last = 0 if reverse else CHUNK - 1
        chunks = range(nc)
        sls = [slice(c * CHUNK, (c + 1) * CHUNK) for c in chunks]
        gcs = [_split_dot(tri_bf, logg[sl]) for sl in sls]
        tot = [g[last:last + 1] for g in gcs]
        dec_col = [jnp.exp2(g.T[:, last:last + 1]) for g in gcs]
        qg = [q[sl] * jnp.exp2(g) for sl, g in zip(sls, gcs)]
        kg = [(k[sl] * jnp.exp2(-g)).astype(BF16) for sl, g in zip(sls, gcs)]
        kd = [(k[sl] * jnp.exp2(t - g)).astype(BF16) for sl, g, t in zip(sls, gcs, tot)]
        q_heads = [jnp.concatenate([jnp.where(qk_head == hd, qg[c], 0.0) for hd in range(GLA_HEADS)], axis=0)
                   for c in chunks]
        raw_all = [_dot_nt(q_heads[c].astype(BF16), kg[c]) for c in chunks]
        raw = [[raw_all[c][hd * CHUNK:(hd + 1) * CHUNK] for hd in range(GLA_HEADS)] for c in chunks]
        st = [jnp.where(blockdiag, _dot_tn(kd[c], v[sls[c]].astype(BF16)), 0.0) for c in chunks]
        att = [[jnp.where(tri, raw[c][hd], 0.0).astype(BF16) for hd in range(GLA_HEADS)] for c in chunks]
        o = []
        for c in chunks:
            acc = None
            for pair in range(GLA_HEADS // 2):
                a_st = jnp.concatenate([att[c][2 * pair], att[c][2 * pair + 1]], axis=1)
                v_st = jnp.concatenate([jnp.where(v_head == 2 * pair + r, v[sls[c]], 0.0) for r in range(2)], axis=0)
                part = _dot(a_st, v_st.astype(BF16))
                acc = part if acc is None else acc + part
            o.append(acc)
        s = s_ref[...]
        s_in = [None] * nc
        for c in (reversed(chunks) if reverse else chunks):
            s_in[c] = s
            s = s * dec_col[c] + st[c]
        s_ref[...] = s
        return [o[c] + _dot(qg[c].astype(BF16), s_in[c].astype(BF16)) for c in chunks]

    @pl.when(j < nt)
    def _():
        for c, o in enumerate(scan(False)):
            of_ref[pl.ds(base + c * CHUNK, CHUNK), :] = o

    @pl.when(j >= nt)
    def _():
        for c, ob in enumerate(scan(True)):
            sl = slice(c * CHUNK, (c + 1) * CHUNK)
            o = of_ref[pl.ds(base + c * CHUNK, CHUNK), :] + ob
            ms = _split_dot_right(o * o, avg_ref[...])
            out_ref[0, sl, :] = o * lax.rsqrt(ms + EPS) * g_ref[...] * _silu(r_ref[0, sl, :])


def _split_dot_right(x, m_bf16):
    x1 = x.astype(BF16)
    r1 = x - x1.astype(F32)
    x2 = r1.astype(BF16)
    x3 = (r1 - x2.astype(F32)).astype(BF16)
    return _dot(x1, m_bf16) + _dot(x2, m_bf16) + _dot(x3, m_bf16)


def _gla(gla_g, w2p, gb, g, avg, tc):
    b, t, _ = gla_g.shape
    nt = t // tc

    def tile_of(j):
        return jnp.where(j < nt, j, 2 * nt - 1 - j)

    kern = functools.partial(_gla_kernel, nt=nt, tc=tc)
    return pl.pallas_call(
        kern,
        grid=(b, 2 * nt),
        in_specs=[
            pl.BlockSpec((1, tc, GLA_QK_W), lambda i, j: (i, tile_of(j), 0)),
            pl.BlockSpec((1, tc, GLA_QK_W), lambda i, j: (i, tile_of(j), 1)),
            pl.BlockSpec((1, tc, GLA_V_W), lambda i, j: (i, tile_of(j), 1)),
            pl.BlockSpec((1, tc, GLA_V_W), lambda i, j: (i, tile_of(j), 2)),
            pl.BlockSpec((1, tc, LANE), lambda i, j: (i, tile_of(j), (2 * GLA_QK_W + 2 * GLA_V_W) // LANE)),
            pl.BlockSpec((1, LANE, GLA_QK_W), lambda i, j: (j // nt, 0, 0)),
            pl.BlockSpec((1, 1, GLA_QK_W), lambda i, j: (j // nt, 0, 0)),
            pl.BlockSpec((1, GLA_V_W), lambda i, j: (0, 0)),
            pl.BlockSpec((GLA_V_W, GLA_V_W), lambda i, j: (0, 0)),
        ],
        out_specs=pl.BlockSpec((1, tc, GLA_V_W), lambda i, j: (i, jnp.where(j < nt, nt - 1, 2 * nt - 1 - j), 0)),
        out_shape=jax.ShapeDtypeStruct((b, t, GLA_V_W), F32),
        scratch_shapes=[pltpu.VMEM((t, GLA_V_W), F32),
                        pltpu.VMEM((GLA_QK_W, GLA_V_W), F32)],
        compiler_params=_params(dimension_semantics=("arbitrary", "arbitrary")),
        name="gla",
    )(gla_g, gla_g, gla_g, gla_g, gla_g, w2p, gb, g, avg)


def _rope(y, cos, sin, half):
    lane = lax.broadcasted_iota(jnp.int32, (1, LANE), 1)
    lo = (lane % (2 * half)) < half
    rot = jnp.where(lo, pltpu.roll(y, LANE - half, 1), pltpu.roll(y, half, 1))
    return y * cos + rot * sin


ONES_LANE = LANE - 1
VT_ROWS = 80
SCORE_BOUND_LIMIT = 48.0

def _finish_keys(k):
    kf = k.astype(BF16).astype(F32)
    norm = jnp.sqrt(jnp.max(jnp.sum(kf * kf, axis=-1, keepdims=True), axis=0, keepdims=True))
    lane = lax.broadcasted_iota(jnp.int32, (1, LANE), 1)
    return jnp.where(lane == ONES_LANE, 1.0, k).astype(BF16), norm


def _update_kmax(kmax_ref, norms):
    @pl.when(pl.program_id(1) == 0)
    def _():
        kmax_ref[...] = jnp.zeros_like(kmax_ref)

    lane = lax.broadcasted_iota(jnp.int32, (1, LANE), 1)
    upd = jnp.zeros((1, LANE), F32)
    for g, norm in enumerate(norms):
        upd = jnp.where(lane == g, norm, upd)
    kmax_ref[0] = jnp.maximum(kmax_ref[0], upd)


def _vt_rows(v, heads, dv):
    tm = v.shape[0]
    vt = v.T.reshape(heads, dv, tm)
    row = lax.broadcasted_iota(jnp.int32, (heads, VT_ROWS - dv, tm), 1)
    return jnp.concatenate([vt, jnp.where(row == 0, 1.0, 0.0)], axis=1).astype(BF16)


def _attn_kernel(q_ref, k_ref, vt_ref, kmax_ref, o_ref, qa_ref, m_ref, acc_ref, *, n_q, rep, tk, dv):
    nk = k_ref.shape[1] // tk
    lane = lax.broadcasted_iota(jnp.int32, (1, LANE), 1)
    kmax = kmax_ref[0, 0:1, :]
    bound_max = None
    for h in range(n_q):
        qf = q_ref[0, :, h * LANE:(h + 1) * LANE].astype(F32)
        kg = jnp.max(jnp.where(lane == h // rep, kmax, 0.0), axis=-1, keepdims=True)
        bound = jnp.sqrt(jnp.sum(qf * qf, axis=-1, keepdims=True)) * kg
        qa_ref[h] = jnp.where(lane == ONES_LANE, -bound, qf).T.astype(BF16)
        top = jnp.max(bound)
        bound_max = top if bound_max is None else jnp.maximum(bound_max, top)
    acc_ref[...] = jnp.zeros(acc_ref.shape, F32)

    @pl.when(bound_max <= SCORE_BOUND_LIMIT)
    def _():
        def scores(i, h):
            ks = pl.multiple_of(i * tk, tk)
            g = h // rep
            return _dot(k_ref[0, pl.ds(ks, tk), g * LANE:(g + 1) * LANE], qa_ref[h])

        group = max(u for u in (1, 2, 4, 8, 16) if nk % u == 0)

        def body(ii, carry):
            steps = [(ii * group + u, h) for u in range(group) for h in range(n_q)]
            s_next = scores(*steps[0])
            for n, (i, h) in enumerate(steps):
                s = s_next
                if n + 1 < len(steps):
                    s_next = scores(*steps[n + 1])
                p = jnp.exp2(s)
                acc_ref[h, 0:dv, :] += _dot(vt_ref[0, h // rep, i, 0:dv, :], p.astype(BF16))
                acc_ref[h, dv:dv + 8, :] += jnp.sum(p.reshape(tk // 8, 8, p.shape[1]), axis=0)
            return carry

        lax.fori_loop(0, nk // group, body, 0)

    @pl.when(bound_max > SCORE_BOUND_LIMIT)
    def _():
        m_ref[...] = jnp.full(m_ref.shape, -jnp.inf, F32)

        def body(i, carry):
            ks = pl.multiple_of(i * tk, tk)
            for h in range(n_q):
                g = h // rep
                s = _dot_nt(k_ref[0, pl.ds(ks, tk), g * LANE:(g + 1) * LANE], q_ref[0, :, h * LANE:(h + 1) * LANE])
                m_old = m_ref[h]
                m_new = jnp.maximum(m_old, jnp.max(s, axis=0, keepdims=True))
                p = jnp.exp2(s - m_new)
                acc_ref[h] = acc_ref[h] * jnp.exp2(m_old - m_new) + _dot(vt_ref[0, g, i], p.astype(BF16))
                m_ref[h] = m_new
            return carry

        lax.fori_loop(0, nk, body, 0)

    o = jnp.concatenate([acc_ref[h, 0:dv, :] * (1.0 / jnp.sum(acc_ref[h, dv:dv + 8, :], axis=0, keepdims=True))
                         for h in range(n_q)], axis=0)
    o_ref[0] = o.T


def _attention(q, k, vt, kmax, rep, tq, tk, dv):
    b, t, qw = q.shape
    kw = k.shape[2]
    n_q = qw // LANE
    n_kv = n_q // rep
    ow = n_q * dv
    kern = functools.partial(_attn_kernel, n_q=n_q, rep=rep, tk=tk, dv=dv)
    return pl.pallas_call(
        kern,
        grid=(b, t // tq),
        in_specs=[
            pl.BlockSpec((1, tq, qw), lambda i, j: (i, j, 0)),
            pl.BlockSpec((1, t, kw), lambda i, j: (i, 0, 0)),
            pl.BlockSpec((1, n_kv, t // tk, VT_ROWS, tk), lambda i, j: (i, 0, 0, 0, 0)),
            pl.BlockSpec((1, 8, LANE), lambda i, j: (i, 0, 0)),
        ],
        out_specs=pl.BlockSpec((1, tq, ow), lambda i, j: (i, j, 0)),
        out_shape=jax.ShapeDtypeStruct((b, t, ow), F32),
        scratch_shapes=[pltpu.VMEM((n_q, LANE, tq), BF16), pltpu.VMEM((n_q, 1, tq), F32),
                        pltpu.VMEM((n_q, VT_ROWS, tq), F32)],
        compiler_params=_params(dimension_semantics=("arbitrary", "arbitrary")),
        name="attention",
    )(q, k, vt, kmax)


FF_CHUNK = 1024


def _outmlp_kernel(x_ref, m0_ref, m1_ref, m2_ref, m3_ref, wo_ref, g2_ref, w1_ref, w2_ref, gf_ref, o_ref, *, final):
    x1 = x_ref[...]
    for i, m_ref in enumerate((m0_ref, m1_ref, m2_ref, m3_ref)):
        x1 = x1 + _dot(m_ref[...].astype(BF16), wo_ref[i * GROUP_W:(i + 1) * GROUP_W, :])
    ms = jnp.mean(x1 * x1, axis=-1, keepdims=True)
    h = (x1 * lax.rsqrt(ms + EPS) * g2_ref[...]).astype(BF16)
    acc = None
    for c in range(D_FF // FF_CHUNK):
        sl = slice(c * FF_CHUNK, (c + 1) * FF_CHUNK)
        u = jnp.maximum(_dot(h, w1_ref[:, sl]), 0.0)
        part = _dot((u * u).astype(BF16), w2_ref[sl, :])
        acc = part if acc is None else acc + part
    y = x1 + acc
    if final:
        ms = jnp.mean(y * y, axis=-1, keepdims=True)
        y = y * lax.rsqrt(ms + EPS) * gf_ref[...]
    o_ref[...] = y


def _outmlp(x2d, mixes, wo, g2, w1, w2, gf, final, tm):
    n = x2d.shape[0]
    const = lambda i: (0, 0)
    single = pl.Buffered(1)
    kern = functools.partial(_outmlp_kernel, final=final)
    return pl.pallas_call(
        kern,
        grid=(n // tm,),
        in_specs=[pl.BlockSpec((tm, D_MODEL), lambda i: (i, 0))]
                 + [pl.BlockSpec((tm, GROUP_W), lambda i: (i, 0)) for _ in range(4)]
                 + [pl.BlockSpec((D_MODEL, D_MODEL), const, pipeline_mode=single),
                    pl.BlockSpec((1, D_MODEL), const),
                    pl.BlockSpec((D_MODEL, D_FF), const, pipeline_mode=single),
                    pl.BlockSpec((D_FF, D_MODEL), const, pipeline_mode=single),
                    pl.BlockSpec((1, D_MODEL), const)],
        out_specs=pl.BlockSpec((tm, D_MODEL), lambda i: (i, 0)),
        out_shape=jax.ShapeDtypeStruct((n, D_MODEL), F32),
        compiler_params=_params(dimension_semantics=("arbitrary",)),
        name="outmlp",
    )(x2d, *mixes, wo, g2.reshape(1, D_MODEL), w1, w2, gf.reshape(1, D_MODEL))


def _rope_tables(t):
    pos = np.arange(t, dtype=np.int32)
    row = (pos // GRID_W).astype(np.float32)
    col = (pos % GRID_W).astype(np.float32)

    def block(p, n):
        inv_freq = np.float32(ROPE_THETA) ** (-np.arange(n, dtype=np.float32) / np.float32(n))
        ang = p[:, None] * inv_freq[None, :].astype(np.float32)
        c, s = np.cos(ang).astype(np.float32), np.sin(ang).astype(np.float32)
        return np.concatenate([c, c], axis=1), np.concatenate([-s, s], axis=1)

    def table(n, lead):
        cr, sr = block(row, n)
        cc, sc = block(col, n)
        tail = LANE - lead - 4 * n
        cos = np.concatenate([np.ones((t, lead), np.float32), cr, cc, np.ones((t, tail), np.float32)], axis=1)
        sin = np.concatenate([np.zeros((t, lead), np.float32), sr, sc, np.zeros((t, tail), np.float32)], axis=1)
        return jnp.asarray(cos), jnp.asarray(sin)

    return table(GQA_HEAD_DIM // 4, 0), table(MLA_ROPE // 4, MLA_NOPE)


def _pad_lanes(v, width=LANE):
    return jnp.concatenate([v, jnp.zeros((width - v.shape[0],), v.dtype)]).reshape(1, width)


def _layer_params(i, p):
    (norm1_g, w_in, ssd_conv_w, ssd_conv_b, ssd_dt_bias, ssd_a_log, ssd_d, ssd_norm_g,
     gqa_q_norm_g, gqa_k_norm_g, gla_gate_w2, gla_gate_b, gla_norm_g, mla_q_norm_g, mla_w_uq,
     mla_kv_norm_g, mla_w_ukv, w_out, norm2_g, w_ff1, w_ff2) = [a[i] for a in p]
    out = {}
    out["norm1_g"] = norm1_g
    out["w_all"] = _gather_cols(w_in, _IN_COLS).astype(BF16)
    out["conv_w"] = jnp.concatenate([ssd_conv_w, jnp.zeros((HALO - SSD_CONV_W, SSD_XBC), F32)], axis=0)
    out["conv_b"] = ssd_conv_b.reshape(1, SSD_XBC)
    expand = lambda a: jnp.repeat(a, SSD_HEAD_DIM, axis=-1)
    out["dt_bias"] = _pad_lanes(ssd_dt_bias.reshape(2 * SSD_HEADS))
    spread = np.zeros((2, LANE, GROUP_W), np.float32)
    for d in range(2):
        for h in range(SSD_HEADS):
            spread[d, d * SSD_HEADS + h, h * SSD_HEAD_DIM:(h + 1) * SSD_HEAD_DIM] = 1.0
    out["dt_spread"] = jnp.asarray(spread, BF16)
    out["a_neg"] = expand(-jnp.exp(ssd_a_log) * LOG2E).reshape(2, 1, GROUP_W)
    out["ssd_d"] = expand(ssd_d).reshape(1, GROUP_W)
    out["ssd_norm_g"] = ssd_norm_g.reshape(1, GROUP_W)
    out["gq"] = _pad_lanes(gqa_q_norm_g)
    out["gk"] = _pad_lanes(gqa_k_norm_g)
    w2p = jnp.zeros((2, LANE, GLA_QK_W), F32)
    for d in range(2):
        w2p = w2p.at[d, d * GLA_LOWRANK:(d + 1) * GLA_LOWRANK, :].set(gla_gate_w2[d])
    out["gla_w2"] = w2p.astype(BF16)
    out["gla_b"] = gla_gate_b.reshape(2, 1, GLA_QK_W)
    out["gla_norm_g"] = jnp.tile(gla_norm_g, GLA_HEADS).reshape(1, GLA_V_W)
    out["mla_gq"] = mla_q_norm_g.reshape(1, MLA_Q_LORA)
    out["mla_gkv"] = mla_kv_norm_g.reshape(1, MLA_KV_LORA)
    uq_cols, uk_cols, uv_cols = [], [], []
    for h in range(MLA_HEADS):
        uq_cols += list(range(h * MLA_QK, (h + 1) * MLA_QK)) + [-1] * (LANE - MLA_QK)
        base = h * (MLA_NOPE + MLA_V)
        uk_cols += list(range(base, base + MLA_NOPE)) + [-1] * (LANE - MLA_NOPE)
        uv_cols += list(range(base + MLA_NOPE, base + MLA_NOPE + MLA_V))
    out["wuq"] = _gather_cols(mla_w_uq, np.asarray(uq_cols, np.int32)).astype(BF16)
    out["wuk"] = _gather_cols(mla_w_ukv, np.asarray(uk_cols, np.int32)).astype(BF16)
    out["wuv"] = _gather_cols(mla_w_ukv, np.asarray(uv_cols, np.int32)).astype(BF16)
    out["w_out"] = w_out.astype(BF16)
    out["norm2_g"] = norm2_g
    out["w_ff1"] = w_ff1.astype(BF16)
    out["w_ff2"] = w_ff2.astype(BF16)
    return out


TILE_PREF = dict(tc=1024, tp=512, tq_gqa=512, tq_mla=512, tmlp=512)


def _tiles(t):
    pick = lambda pref: max(c for c in (128, 256, 512, 1024) if c <= pref and t % c == 0)
    return {name: pick(pref) for name, pref in TILE_PREF.items()}


def _trunk(x, layers, final_norm_g):
    b, t, d = x.shape
    n = b * t
    ts = _tiles(t)
    tk = ts["tp"]
    tables = _rope_tables(t)
    avg = jnp.asarray(np.kron(np.eye(GLA_HEADS), np.full((GLA_DV, GLA_DV), 1.0 / GLA_DV)), BF16)
    x2d = x.reshape(n, d)
    for i, lp in enumerate(layers):
        ssd_g, gla_g, gq, gk, gvt, gkmax, mq, mk, mvt, mkmax = _inproj(x2d.reshape(b, t, d), lp, tables, tk)
        ssd_out = _ssd(ssd_g, lp["conv_w"], lp["conv_b"], lp["dt_bias"], lp["dt_spread"], lp["a_neg"],
                       lp["ssd_d"], lp["ssd_norm_g"], ts["tc"])
        gqa_out = _attention(gq, gk, gvt, gkmax, GQA_HEADS // GQA_KV_HEADS, ts["tq_gqa"], tk, GQA_HEAD_DIM)
        gla_out = _gla(gla_g, lp["gla_w2"], lp["gla_b"], lp["gla_norm_g"], avg, ts["tc"])
        mla_out = _attention(mq, mk, mvt, mkmax, 1, ts["tq_mla"], tk, MLA_V)
        mixes = [m.reshape(n, GROUP_W) for m in (ssd_out, gqa_out, gla_out, mla_out)]
        x2d = _outmlp(x2d, mixes, lp["w_out"], lp["norm2_g"], lp["w_ff1"], lp["w_ff2"], final_norm_g,
                      i == len(layers) - 1, ts["tmlp"])
    return x2d.reshape(b, t, d)


def kernel(x_prompt, x_sample, norm1_g, w_in, ssd_conv_w, ssd_conv_b, ssd_dt_bias, ssd_a_log, ssd_d, ssd_norm_g,
           gqa_q_norm_g, gqa_k_norm_g, gla_gate_w2, gla_gate_b, gla_norm_g, mla_q_norm_g, mla_w_uq,
           mla_kv_norm_g, mla_w_ukv, w_out, norm2_g, w_ff1, w_ff2, final_norm_g):
    stacked = (norm1_g, w_in, ssd_conv_w, ssd_conv_b, ssd_dt_bias, ssd_a_log, ssd_d, ssd_norm_g,
               gqa_q_norm_g, gqa_k_norm_g, gla_gate_w2, gla_gate_b, gla_norm_g, mla_q_norm_g, mla_w_uq,
               mla_kv_norm_g, mla_w_ukv, w_out, norm2_g, w_ff1, w_ff2)
    layers = [_layer_params(i, stacked) for i in range(norm1_g.shape[0])]
    return (_trunk(x_prompt, layers, final_norm_g), _trunk(x_sample, layers, final_norm_g))
```

```python
import functools

import numpy as np
import jax
import jax.numpy as jnp
from jax import lax
from jax.experimental import pallas as pl
from jax.experimental.pallas import tpu as pltpu

F32 = jnp.float32
BF16 = jnp.bfloat16

D_MODEL = 1024
DEPTH = 2
GRID_W = 64
CHUNK = 128
ROPE_THETA = 10000.0
EPS = 1e-6
GROUP_W = D_MODEL // 4
D_FF = 4 * D_MODEL

SSD_HEADS = 4
SSD_HEAD_DIM = GROUP_W // SSD_HEADS
SSD_GROUPS = 2
SSD_STATE = 128
SSD_CONV_W = 5
SSD_XBC = GROUP_W + 2 * SSD_GROUPS * SSD_STATE

GQA_HEADS = 4
GQA_KV_HEADS = 2
GQA_HEAD_DIM = GROUP_W // GQA_HEADS

GLA_HEADS = 4
GLA_DV = GROUP_W // GLA_HEADS
GLA_DK = GLA_DV // 2
GLA_LOWRANK = 16
GLA_TAU = 16.0

MLA_HEADS = 4
MLA_Q_LORA = 256
MLA_KV_LORA = 128
MLA_NOPE = 64
MLA_ROPE = 32
MLA_V = GROUP_W // MLA_HEADS
MLA_QK = MLA_NOPE + MLA_ROPE

IN_SIZES = (GROUP_W, SSD_XBC, 2 * SSD_HEADS,
            GQA_HEADS * GQA_HEAD_DIM, GQA_KV_HEADS * GQA_HEAD_DIM, GQA_KV_HEADS * GQA_HEAD_DIM,
            GLA_HEADS * GLA_DK, GLA_HEADS * GLA_DK, GLA_HEADS * GLA_DV, GLA_HEADS * GLA_DV, 2 * GLA_LOWRANK,
            MLA_Q_LORA, MLA_KV_LORA, MLA_ROPE)
IN_OFFS = tuple(int(v) for v in np.concatenate([[0], np.cumsum(IN_SIZES)]))

LOG2E = 1.4426950408889634
LANE = 128
HALO = 8
VMEM_LIMIT = 52 * 1024 * 1024

SSD_W = SSD_XBC + GROUP_W + LANE
GQA_W = GQA_HEADS * LANE + GQA_KV_HEADS * LANE + GQA_KV_HEADS * GQA_HEAD_DIM
GLA_W = 2 * GLA_HEADS * GLA_DK + 2 * GLA_HEADS * GLA_DV + LANE
MLA_W = MLA_Q_LORA + MLA_KV_LORA + LANE
PROJ_W = SSD_W + GQA_W + GLA_W + MLA_W


def _in_proj_columns():
    o = IN_OFFS
    cols = []
    cols += list(range(o[1], o[2]))
    cols += list(range(o[0], o[1]))
    cols += list(range(o[2], o[3])) + [-1] * (LANE - 2 * SSD_HEADS)
    for h in range(GQA_HEADS):
        cols += list(range(o[3] + h * GQA_HEAD_DIM, o[3] + (h + 1) * GQA_HEAD_DIM)) + [-1] * (LANE - GQA_HEAD_DIM)
    for h in range(GQA_KV_HEADS):
        cols += list(range(o[4] + h * GQA_HEAD_DIM, o[4] + (h + 1) * GQA_HEAD_DIM)) + [-1] * (LANE - GQA_HEAD_DIM)
    cols += list(range(o[5], o[6]))
    cols += list(range(o[6], o[10]))
    cols += list(range(o[10], o[11])) + [-1] * (LANE - 2 * GLA_LOWRANK)
    cols += list(range(o[11], o[13]))
    cols += [-1] * MLA_NOPE + list(range(o[13], o[14])) + [-1] * (LANE - MLA_QK)
    cols = np.asarray(cols, np.int32)
    assert cols.shape[0] == PROJ_W
    return cols


_IN_COLS = _in_proj_columns()


def _gather_cols(w, cols):
    picked = jnp.take(w, jnp.asarray(np.maximum(cols, 0)), axis=1)
    return jnp.where(jnp.asarray(cols >= 0)[None, :], picked, 0.0)


def _params(**kw):
    return pltpu.CompilerParams(vmem_limit_bytes=VMEM_LIMIT, **kw)


def _silu(x):
    return x * (1.0 / (1.0 + jnp.exp(-x)))


def _softplus(x):
    return jnp.maximum(x, 0.0) + jnp.log1p(jnp.exp(-jnp.abs(x)))


def _dot(a, b):
    return jnp.dot(a, b, preferred_element_type=F32)


def _dot_nt(a, b):
    return lax.dot_general(a, b, (((1,), (1,)), ((), ())), preferred_element_type=F32)


def _dot_tn(a, b):
    return lax.dot_general(a, b, (((0,), (0,)), ((), ())), preferred_element_type=F32)


def _split_dot(m_bf16, x):
    x1 = x.astype(BF16)
    r1 = x - x1.astype(F32)
    x2 = r1.astype(BF16)
    x3 = (r1 - x2.astype(F32)).astype(BF16)
    return _dot(m_bf16, x1) + _dot(m_bf16, x2) + _dot(m_bf16, x3)


def _tri(reverse):
    li = lax.broadcasted_iota(jnp.int32, (CHUNK, CHUNK), 0)
    si = lax.broadcasted_iota(jnp.int32, (CHUNK, CHUNK), 1)
    return (si >= li) if reverse else (si <= li)


def _inproj_kernel(x_ref, g_ref, w_ref, cosg_ref, sing_ref, cosm_ref, sinm_ref, gq_ref, gk_ref,
                   mgq_ref, mgkv_ref, wuq_ref, wuk_ref, wuv_ref,
                   ssd_ref, gla_ref, gq_out, gk_out, gvt_out, gkmax_out, mq_out, mk_out, mvt_out, mkmax_out):
    x = x_ref[0]
    ms = jnp.mean(x * x, axis=-1, keepdims=True)
    h = (x * lax.rsqrt(ms + EPS) * g_ref[...]).astype(BF16)
    c_gqa, c_gla, c_mla = SSD_W, SSD_W + GQA_W, SSD_W + GQA_W + GLA_W
    mla = _dot(h, w_ref[:, c_mla:PROJ_W])
    gqa = _dot(h, w_ref[:, c_gqa:c_gla])

    def rms(y, g):
        ms_y = jnp.mean(y * y, axis=-1, keepdims=True)
        return (y * lax.rsqrt(ms_y + EPS) * g).astype(BF16)

    cq = rms(mla[:, :MLA_Q_LORA], mgq_ref[...])
    ckv = rms(mla[:, MLA_Q_LORA:MLA_Q_LORA + MLA_KV_LORA], mgkv_ref[...])
    ssd_ref[0] = _dot(h, w_ref[:, 0:SSD_W])
    q = _dot(cq, wuq_ref[...])
    kn = _dot(ckv, wuk_ref[...])
    v = _dot(ckv, wuv_ref[...])

    cos, sin = cosg_ref[...], sing_ref[...]
    qw, kw = GQA_HEADS * LANE, GQA_KV_HEADS * LANE
    heads = [gqa[:, hd * LANE:(hd + 1) * LANE] for hd in range(GQA_HEADS + GQA_KV_HEADS)]
    gains = [gq_ref[...]] * GQA_HEADS + [gk_ref[...]] * GQA_KV_HEADS
    inv = [lax.rsqrt(jnp.sum(y * y, axis=-1, keepdims=True) * (1.0 / GQA_HEAD_DIM) + EPS) for y in heads]
    roped = [_rope(y * r * g, cos, sin, GQA_HEAD_DIM // 4) for y, r, g in zip(heads, inv, gains)]
    for hd in range(GQA_HEADS):
        gq_out[0, :, hd * LANE:(hd + 1) * LANE] = (roped[hd] * (GQA_HEAD_DIM ** -0.5 * LOG2E)).astype(BF16)
    norms = []
    for hd in range(GQA_KV_HEADS):
        gk_out[0, :, hd * LANE:(hd + 1) * LANE], norm = _finish_keys(roped[GQA_HEADS + hd])
        norms.append(norm)
    _update_kmax(gkmax_out, norms)
    gvt_out[0, :, 0] = _vt_rows(gqa[:, qw + kw:], GQA_KV_HEADS, GQA_HEAD_DIM)
    gla_ref[0] = _dot(h, w_ref[:, c_gla:c_mla])

    cos, sin = cosm_ref[...], sinm_ref[...]
    k_rope = _rope(mla[:, MLA_Q_LORA + MLA_KV_LORA:], cos, sin, MLA_ROPE // 4)
    q_roped = [_rope(q[:, hd * LANE:(hd + 1) * LANE], cos, sin, MLA_ROPE // 4) for hd in range(MLA_HEADS)]
    norms = []
    for hd in range(MLA_HEADS):
        sl = slice(hd * LANE, (hd + 1) * LANE)
        mq_out[0, :, sl] = (q_roped[hd] * (MLA_QK ** -0.5 * LOG2E)).astype(BF16)
        mk_out[0, :, sl], norm = _finish_keys(kn[:, sl] + k_rope)
        norms.append(norm)
    _update_kmax(mkmax_out, norms)
    mvt_out[0, :, 0] = _vt_rows(v, MLA_HEADS, MLA_V)


def _inproj(x, lp, tables, tm):
    b, t, _ = x.shape
    (cos_g, sin_g), (cos_m, sin_m) = tables
    const = lambda i, j: (0, 0)
    tok = lambda i, j: (i, j, 0)
    tab = pl.BlockSpec((tm, LANE), lambda i, j: (j, 0))
    qw, kw, mw = GQA_HEADS * LANE, GQA_KV_HEADS * LANE, MLA_HEADS * LANE

    def vt_spec(heads):
        return pl.BlockSpec((1, heads, 1, VT_ROWS, tm), lambda i, j: (i, 0, j, 0, 0))

    kmax_spec = pl.BlockSpec((1, 8, LANE), lambda i, j: (i, 0, 0))
    return pl.pallas_call(
        _inproj_kernel,
        grid=(b, t // tm),
        in_specs=[pl.BlockSpec((1, tm, D_MODEL), tok),
                  pl.BlockSpec((1, D_MODEL), const),
                  pl.BlockSpec((D_MODEL, PROJ_W), const, pipeline_mode=pl.Buffered(1)),
                  tab, tab, tab, tab,
                  pl.BlockSpec((1, LANE), const), pl.BlockSpec((1, LANE), const),
                  pl.BlockSpec((1, MLA_Q_LORA), const), pl.BlockSpec((1, MLA_KV_LORA), const),
                  pl.BlockSpec((MLA_Q_LORA, mw), const), pl.BlockSpec((MLA_KV_LORA, mw), const),
                  pl.BlockSpec((MLA_KV_LORA, MLA_HEADS * MLA_V), const)],
        out_specs=[pl.BlockSpec((1, tm, SSD_W), tok), pl.BlockSpec((1, tm, GLA_W), tok),
                   pl.BlockSpec((1, tm, qw), tok), pl.BlockSpec((1, tm, kw), tok), vt_spec(GQA_KV_HEADS), kmax_spec,
                   pl.BlockSpec((1, tm, mw), tok), pl.BlockSpec((1, tm, mw), tok), vt_spec(MLA_HEADS), kmax_spec],
        out_shape=[jax.ShapeDtypeStruct((b, t, SSD_W), F32), jax.ShapeDtypeStruct((b, t, GLA_W), F32),
                   jax.ShapeDtypeStruct((b, t, qw), BF16), jax.ShapeDtypeStruct((b, t, kw), BF16),
                   jax.ShapeDtypeStruct((b, GQA_KV_HEADS, t // tm, VT_ROWS, tm), BF16),
                   jax.ShapeDtypeStruct((b, 8, LANE), F32),
                   jax.ShapeDtypeStruct((b, t, mw), BF16), jax.ShapeDtypeStruct((b, t, mw), BF16),
                   jax.ShapeDtypeStruct((b, MLA_HEADS, t // tm, VT_ROWS, tm), BF16),
                   jax.ShapeDtypeStruct((b, 8, LANE), F32)],
        compiler_params=_params(dimension_semantics=("arbitrary", "arbitrary")),
        name="inproj",
    )(x, lp["norm1_g"].reshape(1, D_MODEL), lp["w_all"], cos_g, sin_g, cos_m, sin_m, lp["gq"], lp["gk"],
      lp["mla_gq"], lp["mla_gkv"], lp["wuq"], lp["wuk"], lp["wuv"])


def _ssd_kernel(xbc_ref, prev_ref, next_ref, z_ref, dtx_ref, cw_ref, cb_ref, dtb_ref, ex_ref, a_ref, d_ref, g_ref,
                out_ref, yf_ref, xs_ref, bc_ref, h_ref, xe_ref, *, nt, tc):
    j = pl.program_id(1)
    tile = jnp.where(j < nt, j, 2 * nt - 1 - j)
    base = pl.multiple_of(tile * tc, tc)
    nc = tc // CHUNK
    half = GROUP_W // SSD_GROUPS

    @pl.when((j == 0) | (j == nt))
    def _():
        h_ref[...] = jnp.zeros_like(h_ref)

    @pl.when(j < nt)
    def _():
        xe_ref[HALO:HALO + tc, :] = xbc_ref[0]
        xe_ref[0:HALO, :] = jnp.where(tile > 0, prev_ref[0], 0.0)
        xe_ref[HALO + tc:2 * HALO + tc, :] = jnp.where(tile < nt - 1, next_ref[0], 0.0)
        pad = SSD_CONV_W // 2
        window = xe_ref[...]
        acc = cb_ref[...] + cw_ref[pad:pad + 1, :] * window[HALO:HALO + tc]
        for w in range(SSD_CONV_W):
            if w != pad:
                shifted = pltpu.roll(window, (pad - w) % (tc + 2 * HALO), 0)
                acc = acc + cw_ref[w:w + 1, :] * shifted[HALO:HALO + tc]
        xbc = _silu(acc)
        xs_ref[pl.ds(base, tc), :] = xbc[:, :GROUP_W]
        bc_ref[pl.ds(base, tc), :] = xbc[:, GROUP_W:].astype(BF16)

    xs = xs_ref[pl.ds(base, tc), :]
    dt_heads = _softplus(dtx_ref[0] + dtb_ref[...])
    dt = _split_dot_right(dt_heads, ex_ref[0])
    a_dt = dt * a_ref[0]
    xdt = xs * dt
    lane_head = lax.broadcasted_iota(jnp.int32, (1, GROUP_W), 1) // SSD_HEAD_DIM

    def scan(reverse):
        tri = _tri(reverse)
        tri_bf = tri.astype(BF16)
        chunks = range(nc)
        groups = range(SSD_GROUPS)
        per_group = SSD_HEADS // SSD_GROUPS
        sls = [slice(c * CHUNK, (c + 1) * CHUNK) for c in chunks]
        rows = [pl.ds(base + c * CHUNK, CHUNK) for c in chunks]
        acs = [_split_dot(tri_bf, a_dt[sl]) for sl in sls]
        b_c = [[bc_ref[rows[c], g * SSD_STATE:(g + 1) * SSD_STATE] for g in groups] for c in chunks]
        c_c = [[bc_ref[rows[c], (SSD_GROUPS + g) * SSD_STATE:(SSD_GROUPS + g + 1) * SSD_STATE] for g in groups]
               for c in chunks]
        scores = [[_dot_nt(c_c[c][g], b_c[c][g]) for g in groups] for c in chunks]
        tot = [a[0:1] if reverse else a[CHUNK - 1:CHUNK] for a in acs]
        acs_t = [a.T for a in acs]
        x_d = [(xdt[sls[c]] * jnp.exp2(tot[c] - acs[c])).astype(BF16) for c in chunks]
        st = [jnp.concatenate([_dot_tn(b_c[c][g], x_d[c][:, g * half:(g + 1) * half]) for g in groups], axis=1)
              for c in chunks]
        weights = []
        for c in chunks:
            w_c = []
            for hd in range(SSD_HEADS):
                col = jnp.broadcast_to(acs[c][:, hd * SSD_HEAD_DIM:hd * SSD_HEAD_DIM + 1], (CHUNK, CHUNK))
                row = jnp.broadcast_to(acs_t[c][hd * SSD_HEAD_DIM:hd * SSD_HEAD_DIM + 1, :], (CHUNK, CHUNK))
                decay = jnp.exp2(jnp.where(tri, col - row, -jnp.inf))
                w_c.append((scores[c][hd // per_group] * decay).astype(BF16))
            weights.append(w_c)
        y = []
        for c in chunks:
            parts = []
            for g in groups:
                x_g = xdt[sls[c], g * half:(g + 1) * half]
                own = lane_head[:, g * half:(g + 1) * half]
                w_g = jnp.concatenate([weights[c][g * per_group + r] for r in range(per_group)], axis=1)
                x_st = jnp.concatenate([jnp.where(own == g * per_group + r, x_g, 0.0) for r in range(per_group)], axis=0)
                parts.append(_dot(w_g, x_st.astype(BF16)))
            y.append(jnp.concatenate(parts, axis=1))
        h = h_ref[...]
        h_in = [None] * nc
        for c in (reversed(chunks) if reverse else chunks):
            h_in[c] = h
            h = h * jnp.exp2(tot[c]) + st[c]
        h_ref[...] = h
        y_off = [jnp.concatenate([_dot(c_c[c][g], h_in[c][:, g * half:(g + 1) * half].astype(BF16)) for g in groups],
                                 axis=1) for c in chunks]
        return [y[c] + y_off[c] * jnp.exp2(acs[c]) for c in chunks]

    @pl.when(j < nt)
    def _():
        for c, y in enumerate(scan(False)):
            yf_ref[pl.ds(base + c * CHUNK, CHUNK), :] = y

    @pl.when(j >= nt)
    def _():
        for c, yb in enumerate(scan(True)):
            sl = slice(c * CHUNK, (c + 1) * CHUNK)
            y = yf_ref[pl.ds(base + c * CHUNK, CHUNK), :] + yb + d_ref[...] * xs[sl]
            gated = y * _silu(z_ref[0, sl, :])
            ms = jnp.mean(gated * gated, axis=-1, keepdims=True)
            out_ref[0, sl, :] = gated * lax.rsqrt(ms + EPS) * g_ref[...]


def _ssd(ssd_g, cw, cb, dtb, expand, a_exp, d_exp, g, tc):
    b, t, _ = ssd_g.shape
    nt = t // tc
    hb = tc // HALO

    def tile_of(j):
        return jnp.where(j < nt, j, 2 * nt - 1 - j)

    def conv_tile(j):
        return jnp.minimum(j, nt - 1)

    kern = functools.partial(_ssd_kernel, nt=nt, tc=tc)
    return pl.pallas_call(
        kern,
        grid=(b, 2 * nt),
        in_specs=[
            pl.BlockSpec((1, tc, SSD_XBC), lambda i, j: (i, conv_tile(j), 0)),
            pl.BlockSpec((1, HALO, SSD_XBC), lambda i, j: (i, jnp.maximum(conv_tile(j) * hb - 1, 0), 0)),
            pl.BlockSpec((1, HALO, SSD_XBC), lambda i, j: (i, jnp.minimum((conv_tile(j) + 1) * hb, t // HALO - 1), 0)),
            pl.BlockSpec((1, tc, GROUP_W), lambda i, j: (i, tile_of(j), SSD_XBC // GROUP_W)),
            pl.BlockSpec((1, tc, LANE), lambda i, j: (i, tile_of(j), (SSD_XBC + GROUP_W) // LANE)),
            pl.BlockSpec((HALO, SSD_XBC), lambda i, j: (0, 0)),
            pl.BlockSpec((1, SSD_XBC), lambda i, j: (0, 0)),
            pl.BlockSpec((1, LANE), lambda i, j: (0, 0)),
            pl.BlockSpec((1, LANE, GROUP_W), lambda i, j: (j // nt, 0, 0)),
            pl.BlockSpec((1, 1, GROUP_W), lambda i, j: (j // nt, 0, 0)),
            pl.BlockSpec((1, GROUP_W), lambda i, j: (0, 0)),
            pl.BlockSpec((1, GROUP_W), lambda i, j: (0, 0)),
        ],
        out_specs=pl.BlockSpec((1, tc, GROUP_W), lambda i, j: (i, jnp.where(j < nt, nt - 1, 2 * nt - 1 - j), 0)),
        out_shape=jax.ShapeDtypeStruct((b, t, GROUP_W), F32),
        scratch_shapes=[pltpu.VMEM((t, GROUP_W), F32),
                        pltpu.VMEM((t, GROUP_W), F32),
                        pltpu.VMEM((t, 2 * SSD_GROUPS * SSD_STATE), BF16),
                        pltpu.VMEM((SSD_STATE, GROUP_W), F32),
                        pltpu.VMEM((tc + 2 * HALO, SSD_XBC), F32)],
        compiler_params=_params(dimension_semantics=("arbitrary", "arbitrary")),
        name="ssd",
    )(ssd_g, ssd_g, ssd_g, ssd_g, ssd_g, cw, cb, dtb, expand, a_exp, d_exp, g)


GLA_QK_W = GLA_HEADS * GLA_DK
GLA_V_W = GLA_HEADS * GLA_DV


def _gla_kernel(q_ref, k_ref, v_ref, r_ref, lr_ref, w2_ref, gb_ref, g_ref, avg_ref,
                out_ref, of_ref, s_ref, *, nt, tc):
    j = pl.program_id(1)
    tile = jnp.where(j < nt, j, 2 * nt - 1 - j)

    @pl.when((j == 0) | (j == nt))
    def _():
        s_ref[...] = jnp.zeros_like(s_ref)

    base = pl.multiple_of(tile * tc, tc)
    nc = tc // CHUNK
    logit = _dot(lr_ref[0].astype(BF16), w2_ref[0]) + gb_ref[0]
    logg = -_softplus(-logit) * (LOG2E / GLA_TAU)
    q = q_ref[0] * (GLA_DK ** -0.5)
    k = k_ref[0]
    v = v_ref[0]
    qk_head = lax.broadcasted_iota(jnp.int32, (1, GLA_QK_W), 1) // GLA_DK
    v_head = lax.broadcasted_iota(jnp.int32, (1, GLA_V_W), 1) // GLA_DV
    blockdiag = (lax.broadcasted_iota(jnp.int32, (GLA_QK_W, GLA_V_W), 0) // GLA_DK
                 == lax.broadcasted_iota(jnp.int32, (GLA_QK_W, GLA_V_W), 1) // GLA_DV)

    def scan(reverse):
        tri = _tri(reverse)
        tri_bf = tri.astype(BF16)
        last = 0 if reverse else CHUNK - 1
        chunks = range(nc)
        sls = [slice(c * CHUNK, (c + 1) * CHUNK) for c in chunks]
        gcs = [_split_dot(tri_bf, logg[sl]) for sl in sls]
        tot = [g[last:last + 1] for g in gcs]
        dec_col = [jnp.exp2(g.T[:, last:last + 1]) for g in gcs]
        qg = [q[sl] * jnp.exp2(g) for sl, g in zip(sls, gcs)]
        kg = [(k[sl] * jnp.exp2(-g)).astype(BF16) for sl, g in zip(sls, gcs)]
        kd = [(k[sl] * jnp.exp2(t - g)).astype(BF16) for sl, g, t in zip(sls, gcs, tot)]
        q_heads = [jnp.concatenate([jnp.where(qk_head == hd, qg[c], 0.0) for hd in range(GLA_HEADS)], axis=0)
                   for c in chunks]
        raw_all = [_dot_nt(q_heads[c].astype(BF16), kg[c]) for c in chunks]
        raw = [[raw_all[c][hd * CHUNK:(hd + 1) * CHUNK] for hd in range(GLA_HEADS)] for c in chunks]
        st = [jnp.where(blockdiag, _dot_tn(kd[c], v[sls[c]].astype(BF16)), 0.0) for c in chunks]
        att = [[jnp.where(tri, raw[c][hd], 0.0).astype(BF16) for hd in range(GLA_HEADS)] for c in chunks]
        o = []
        for c in chunks:
            acc = None
            for pair in range(GLA_HEADS // 2):
                a_st = jnp.concatenate([att[c][2 * pair], att[c][2 * pair + 1]], axis=1)
                v_st = jnp.concatenate([jnp.where(v_head == 2 * pair + r, v[sls[c]], 0.0) for r in range(2)], axis=0)
                part = _dot(a_st, v_st.astype(BF16))
                acc = part if acc is None else acc + part
            o.append(acc)
        s = s_ref[...]
        s_in = [None] * nc
        for c in (reversed(chunks) if reverse else chunks):
            s_in[c] = s
            s = s * dec_col[c] + st[c]
        s_ref[...] = s
        return [o[c] + _dot(qg[c].astype(BF16), s_in[c].astype(BF16)) for c in chunks]

    @pl.when(j < nt)
    def _():
        for c, o in enumerate(scan(False)):
            of_ref[pl.ds(base + c * CHUNK, CHUNK), :] = o

    @pl.when(j >= nt)
    def _():
        for c, ob in enumerate(scan(True)):
            sl = slice(c * CHUNK, (c + 1) * CHUNK)
            o = of_ref[pl.ds(base + c * CHUNK, CHUNK), :] + ob
            ms = _split_dot_right(o * o, avg_ref[...])
            out_ref[0, sl, :] = o * lax.rsqrt(ms + EPS) * g_ref[...] * _silu(r_ref[0, sl, :])


def _split_dot_right(x, m_bf16):
    x1 = x.astype(BF16)
    r1 = x - x1.astype(F32)
    x2 = r1.astype(BF16)
    x3 = (r1 - x2.astype(F32)).astype(BF16)
    return _dot(x1, m_bf16) + _dot(x2, m_bf16) + _dot(x3, m_bf16)


def _gla(gla_g, w2p, gb, g, avg, tc):
    b, t, _ = gla_g.shape
    nt = t // tc

    def tile_of(j):
        return jnp.where(j < nt, j, 2 * nt - 1 - j)

    kern = functools.partial(_gla_kernel, nt=nt, tc=tc)
    return pl.pallas_call(
        kern,
        grid=(b, 2 * nt),
        in_specs=[
            pl.BlockSpec((1, tc, GLA_QK_W), lambda i, j: (i, tile_of(j), 0)),
            pl.BlockSpec((1, tc, GLA_QK_W), lambda i, j: (i, tile_of(j), 1)),
            pl.BlockSpec((1, tc, GLA_V_W), lambda i, j: (i, tile_of(j), 1)),
            pl.BlockSpec((1, tc, GLA_V_W), lambda i, j: (i, tile_of(j), 2)),
            pl.BlockSpec((1, tc, LANE), lambda i, j: (i, tile_of(j), (2 * GLA_QK_W + 2 * GLA_V_W) // LANE)),
            pl.BlockSpec((1, LANE, GLA_QK_W), lambda i, j: (j // nt, 0, 0)),
            pl.BlockSpec((1, 1, GLA_QK_W), lambda i, j: (j // nt, 0, 0)),
            pl.BlockSpec((1, GLA_V_W), lambda i, j: (0, 0)),
            pl.BlockSpec((GLA_V_W, GLA_V_W), lambda i, j: (0, 0)),
        ],
        out_specs=pl.BlockSpec((1, tc, GLA_V_W), lambda i, j: (i, jnp.where(j < nt, nt - 1, 2 * nt - 1 - j), 0)),
        out_shape=jax.ShapeDtypeStruct((b, t, GLA_V_W), F32),
        scratch_shapes=[pltpu.VMEM((t, GLA_V_W), F32),
                        pltpu.VMEM((GLA_QK_W, GLA_V_W), F32)],
        compiler_params=_params(dimension_semantics=("arbitrary", "arbitrary")),
        name="gla",
    )(gla_g, gla_g, gla_g, gla_g, gla_g, w2p, gb, g, avg)


def _rope(y, cos, sin, half):
    lane = lax.broadcasted_iota(jnp.int32, (1, LANE), 1)
    lo = (lane % (2 * half)) < half
    rot = jnp.where(lo, pltpu.roll(y, LANE - half, 1), pltpu.roll(y, half, 1))
    return y * cos + rot * sin


ONES_LANE = LANE - 1
VT_ROWS = 80
SCORE_BOUND_LIMIT = 48.0

def _finish_keys(k):
    kf = k.astype(BF16).astype(F32)
    norm = jnp.sqrt(jnp.max(jnp.sum(kf * kf, axis=-1, keepdims=True), axis=0, keepdims=True))
    lane = lax.broadcasted_iota(jnp.int32, (1, LANE), 1)
    return jnp.where(lane == ONES_LANE, 1.0, k).astype(BF16), norm


def _update_kmax(kmax_ref, norms):
    @pl.when(pl.program_id(1) == 0)
    def _():
        kmax_ref[...] = jnp.zeros_like(kmax_ref)

    lane = lax.broadcasted_iota(jnp.int32, (1, LANE), 1)
    upd = jnp.zeros((1, LANE), F32)
    for g, norm in enumerate(norms):
        upd = jnp.where(lane == g, norm, upd)
    kmax_ref[0] = jnp.maximum(kmax_ref[0], upd)


def _vt_rows(v, heads, dv):
    tm = v.shape[0]
    vt = v.T.reshape(heads, dv, tm)
    row = lax.broadcasted_iota(jnp.int32, (heads, VT_ROWS - dv, tm), 1)
    return jnp.concatenate([vt, jnp.where(row == 0, 1.0, 0.0)], axis=1).astype(BF16)


def _attn_kernel(q_ref, k_ref, vt_ref, kmax_ref, o_ref, qa_ref, m_ref, acc_ref, *, n_q, rep, tk, dv):
    nk = k_ref.shape[1] // tk
    lane = lax.broadcasted_iota(jnp.int32, (1, LANE), 1)
    kmax = kmax_ref[0, 0:1, :]
    bound_max = None
    for h in range(n_q):
        qf = q_ref[0, :, h * LANE:(h + 1) * LANE].astype(F32)
        kg = jnp.max(jnp.where(lane == h // rep, kmax, 0.0), axis=-1, keepdims=True)
        bound = jnp.sqrt(jnp.sum(qf * qf, axis=-1, keepdims=True)) * kg
        qa_ref[h] = jnp.where(lane == ONES_LANE, -bound, qf).T.astype(BF16)
        top = jnp.max(bound)
        bound_max = top if bound_max is None else jnp.maximum(bound_max, top)
    acc_ref[...] = jnp.zeros(acc_ref.shape, F32)

    @pl.when(bound_max <= SCORE_BOUND_LIMIT)
    def _():
        def scores(i, h):
            ks = pl.multiple_of(i * tk, tk)
            g = h // rep
            return _dot(k_ref[0, pl.ds(ks, tk), g * LANE:(g + 1) * LANE], qa_ref[h])

        group = max(u for u in (1, 2, 4, 8, 16) if nk % u == 0)

        def body(ii, carry):
            steps = [(ii * group + u, h) for u in range(group) for h in range(n_q)]
            s_next = scores(*steps[0])
            for n, (i, h) in enumerate(steps):
                s = s_next
                if n + 1 < len(steps):
                    s_next = scores(*steps[n + 1])
                p = jnp.exp2(s)
                acc_ref[h, 0:dv, :] += _dot(vt_ref[0, h // rep, i, 0:dv, :], p.astype(BF16))
                acc_ref[h, dv:dv + 8, :] += jnp.sum(p.reshape(tk // 8, 8, p.shape[1]), axis=0)
            return carry

        lax.fori_loop(0, nk // group, body, 0)

    @pl.when(bound_max > SCORE_BOUND_LIMIT)
    def _():
        m_ref[...] = jnp.full(m_ref.shape, -jnp.inf, F32)

        def body(i, carry):
            ks = pl.multiple_of(i * tk, tk)
            for h in range(n_q):
                g = h // rep
                s = _dot_nt(k_ref[0, pl.ds(ks, tk), g * LANE:(g + 1) * LANE], q_ref[0, :, h * LANE:(h + 1) * LANE])
                m_old = m_ref[h]
                m_new = jnp.maximum(m_old, jnp.max(s, axis=0, keepdims=True))
                p = jnp.exp2(s - m_new)
                acc_ref[h] = acc_ref[h] * jnp.exp2(m_old - m_new) + _dot(vt_ref[0, g, i], p.astype(BF16))
                m_ref[h] = m_new
            return carry

        lax.fori_loop(0, nk, body, 0)

    o = jnp.concatenate([acc_ref[h, 0:dv, :] * (1.0 / jnp.sum(acc_ref[h, dv:dv + 8, :], axis=0, keepdims=True))
                         for h in range(n_q)], axis=0)
    o_ref[0] = o.T


def _attention(q, k, vt, kmax, rep, tq, tk, dv):
    b, t, qw = q.shape
    kw = k.shape[2]
    n_q = qw // LANE
    n_kv = n_q // rep
    ow = n_q * dv
    kern = functools.partial(_attn_kernel, n_q=n_q, rep=rep, tk=tk, dv=dv)
    return pl.pallas_call(
        kern,
        grid=(b, t // tq),
        in_specs=[
            pl.BlockSpec((1, tq, qw), lambda i, j: (i, j, 0)),
            pl.BlockSpec((1, t, kw), lambda i, j: (i, 0, 0)),
            pl.BlockSpec((1, n_kv, t // tk, VT_ROWS, tk), lambda i, j: (i, 0, 0, 0, 0)),
            pl.BlockSpec((1, 8, LANE), lambda i, j: (i, 0, 0)),
        ],
        out_specs=pl.BlockSpec((1, tq, ow), lambda i, j: (i, j, 0)),
        out_shape=jax.ShapeDtypeStruct((b, t, ow), F32),
        scratch_shapes=[pltpu.VMEM((n_q, LANE, tq), BF16), pltpu.VMEM((n_q, 1, tq), F32),
                        pltpu.VMEM((n_q, VT_ROWS, tq), F32)],
        compiler_params=_params(dimension_semantics=("arbitrary", "arbitrary")),
        name="attention",
    )(q, k, vt, kmax)


FF_CHUNK = 1024


def _outmlp_kernel(x_ref, m0_ref, m1_ref, m2_ref, m3_ref, wo_ref, g2_ref, w1_ref, w2_ref, gf_ref, o_ref, *, final):
    x1 = x_ref[...]
    for i, m_ref in enumerate((m0_ref, m1_ref, m2_ref, m3_ref)):
        x1 = x1 + _dot(m_ref[...].astype(BF16), wo_ref[i * GROUP_W:(i + 1) * GROUP_W, :])
    ms = jnp.mean(x1 * x1, axis=-1, keepdims=True)
    h = (x1 * lax.rsqrt(ms + EPS) * g2_ref[...]).astype(BF16)
    acc = None
    for c in range(D_FF // FF_CHUNK):
        sl = slice(c * FF_CHUNK, (c + 1) * FF_CHUNK)
        u = jnp.maximum(_dot(h, w1_ref[:, sl]), 0.0)
        part = _dot((u * u).astype(BF16), w2_ref[sl, :])
        acc = part if acc is None else acc + part
    y = x1 + acc
    if final:
        ms = jnp.mean(y * y, axis=-1, keepdims=True)
        y = y * lax.rsqrt(ms + EPS) * gf_ref[...]
    o_ref[...] = y


def _outmlp(x2d, mixes, wo, g2, w1, w2, gf, final, tm):
    n = x2d.shape[0]
    const = lambda i: (0, 0)
    single = pl.Buffered(1)
    kern = functools.partial(_outmlp_kernel, final=final)
    return pl.pallas_call(
        kern,
        grid=(n // tm,),
        in_specs=[pl.BlockSpec((tm, D_MODEL), lambda i: (i, 0))]
                 + [pl.BlockSpec((tm, GROUP_W), lambda i: (i, 0)) for _ in range(4)]
                 + [pl.BlockSpec((D_MODEL, D_MODEL), const, pipeline_mode=single),
                    pl.BlockSpec((1, D_MODEL), const),
                    pl.BlockSpec((D_MODEL, D_FF), const, pipeline_mode=single),
                    pl.BlockSpec((D_FF, D_MODEL), const, pipeline_mode=single),
                    pl.BlockSpec((1, D_MODEL), const)],
        out_specs=pl.BlockSpec((tm, D_MODEL), lambda i: (i, 0)),
        out_shape=jax.ShapeDtypeStruct((n, D_MODEL), F32),
        compiler_params=_params(dimension_semantics=("arbitrary",)),
        name="outmlp",
    )(x2d, *mixes, wo, g2.reshape(1, D_MODEL), w1, w2, gf.reshape(1, D_MODEL))


def _rope_tables(t):
    pos = np.arange(t, dtype=np.int32)
    row = (pos // GRID_W).astype(np.float32)
    col = (pos % GRID_W).astype(np.float32)

    def block(p, n):
        inv_freq = np.float32(ROPE_THETA) ** (-np.arange(n, dtype=np.float32) / np.float32(n))
        ang = p[:, None] * inv_freq[None, :].astype(np.float32)
        c, s = np.cos(ang).astype(np.float32), np.sin(ang).astype(np.float32)
        return np.concatenate([c, c], axis=1), np.concatenate([-s, s], axis=1)

    def table(n, lead):
        cr, sr = block(row, n)
        cc, sc = block(col, n)
        tail = LANE - lead - 4 * n
        cos = np.concatenate([np.ones((t, lead), np.float32), cr, cc, np.ones((t, tail), np.float32)], axis=1)
        sin = np.concatenate([np.zeros((t, lead), np.float32), sr, sc, np.zeros((t, tail), np.float32)], axis=1)
        return jnp.asarray(cos), jnp.asarray(sin)

    return table(GQA_HEAD_DIM // 4, 0), table(MLA_ROPE // 4, MLA_NOPE)


def _pad_lanes(v, width=LANE):
    return jnp.concatenate([v, jnp.zeros((width - v.shape[0],), v.dtype)]).reshape(1, width)


def _layer_params(i, p):
    (norm1_g, w_in, ssd_conv_w, ssd_conv_b, ssd_dt_bias, ssd_a_log, ssd_d, ssd_norm_g,
     gqa_q_norm_g, gqa_k_norm_g, gla_gate_w2, gla_gate_b, gla_norm_g, mla_q_norm_g, mla_w_uq,
     mla_kv_norm_g, mla_w_ukv, w_out, norm2_g, w_ff1, w_ff2) = [a[i] for a in p]
    out = {}
    out["norm1_g"] = norm1_g
    out["w_all"] = _gather_cols(w_in, _IN_COLS).astype(BF16)
    out["conv_w"] = jnp.concatenate([ssd_conv_w, jnp.zeros((HALO - SSD_CONV_W, SSD_XBC), F32)], axis=0)
    out["conv_b"] = ssd_conv_b.reshape(1, SSD_XBC)
    expand = lambda a: jnp.repeat(a, SSD_HEAD_DIM, axis=-1)
    out["dt_bias"] = _pad_lanes(ssd_dt_bias.reshape(2 * SSD_HEADS))
    spread = np.zeros((2, LANE, GROUP_W), np.float32)
    for d in range(2):
        for h in range(SSD_HEADS):
            spread[d, d * SSD_HEADS + h, h * SSD_HEAD_DIM:(h + 1) * SSD_HEAD_DIM] = 1.0
    out["dt_spread"] = jnp.asarray(spread, BF16)
    out["a_neg"] = expand(-jnp.exp(ssd_a_log) * LOG2E).reshape(2, 1, GROUP_W)
    out["ssd_d"] = expand(ssd_d).reshape(1, GROUP_W)
    out["ssd_norm_g"] = ssd_norm_g.reshape(1, GROUP_W)
    out["gq"] = _pad_lanes(gqa_q_norm_g)
    out["gk"] = _pad_lanes(gqa_k_norm_g)
    w2p = jnp.zeros((2, LANE, GLA_QK_W), F32)
    for d in range(2):
        w2p = w2p.at[d, d * GLA_LOWRANK:(d + 1) * GLA_LOWRANK, :].set(gla_gate_w2[d])
    out["gla_w2"] = w2p.astype(BF16)
    out["gla_b"] = gla_gate_b.reshape(2, 1, GLA_QK_W)
    out["gla_norm_g"] = jnp.tile(gla_norm_g, GLA_HEADS).reshape(1, GLA_V_W)
    out["mla_gq"] = mla_q_norm_g.reshape(1, MLA_Q_LORA)
    out["mla_gkv"] = mla_kv_norm_g.reshape(1, MLA_KV_LORA)
    uq_cols, uk_cols, uv_cols = [], [], []
    for h in range(MLA_HEADS):
        uq_cols += list(range(h * MLA_QK, (h + 1) * MLA_QK)) + [-1] * (LANE - MLA_QK)
        base = h * (MLA_NOPE + MLA_V)
        uk_cols += list(range(base, base + MLA_NOPE)) + [-1] * (LANE - MLA_NOPE)
        uv_cols += list(range(base + MLA_NOPE, base + MLA_NOPE + MLA_V))
    out["wuq"] = _gather_cols(mla_w_uq, np.asarray(uq_cols, np.int32)).astype(BF16)
    out["wuk"] = _gather_cols(mla_w_ukv, np.asarray(uk_cols, np.int32)).astype(BF16)
    out["wuv"] = _gather_cols(mla_w_ukv, np.asarray(uv_cols, np.int32)).astype(BF16)
    out["w_out"] = w_out.astype(BF16)
    out["norm2_g"] = norm2_g
    out["w_ff1"] = w_ff1.astype(BF16)
    out["w_ff2"] = w_ff2.astype(BF16)
    return out


TILE_PREF = dict(tc=1024, tc_gla=2048, tp=512, tq_gqa=512, tq_mla=512, tmlp=512)


def _tiles(t):
    pick = lambda pref: max(c for c in (128, 256, 512, 1024, 2048) if c <= pref and t % c == 0)
    return {name: pick(pref) for name, pref in TILE_PREF.items()}


def _trunk(x, layers, final_norm_g):
    b, t, d = x.shape
    n = b * t
    ts = _tiles(t)
    tk = ts["tp"]
    tables = _rope_tables(t)
    avg = jnp.asarray(np.kron(np.eye(GLA_HEADS), np.full((GLA_DV, GLA_DV), 1.0 / GLA_DV)), BF16)
    x2d = x.reshape(n, d)
    for i, lp in enumerate(layers):
        ssd_g, gla_g, gq, gk, gvt, gkmax, mq, mk, mvt, mkmax = _inproj(x2d.reshape(b, t, d), lp, tables, tk)
        ssd_out = _ssd(ssd_g, lp["conv_w"], lp["conv_b"], lp["dt_bias"], lp["dt_spread"], lp["a_neg"],
                       lp["ssd_d"], lp["ssd_norm_g"], ts["tc"])
        gqa_out = _attention(gq, gk, gvt, gkmax, GQA_HEADS // GQA_KV_HEADS, ts["tq_gqa"], tk, GQA_HEAD_DIM)
        gla_out = _gla(gla_g, lp["gla_w2"], lp["gla_b"], lp["gla_norm_g"], avg, ts["tc_gla"])
        mla_out = _attention(mq, mk, mvt, mkmax, 1, ts["tq_mla"], tk, MLA_V)
        mixes = [m.reshape(n, GROUP_W) for m in (ssd_out, gqa_out, gla_out, mla_out)]
        x2d = _outmlp(x2d, mixes, lp["w_out"], lp["norm2_g"], lp["w_ff1"], lp["w_ff2"], final_norm_g,
                      i == len(layers) - 1, ts["tmlp"])
    return x2d.reshape(b, t, d)


def kernel(x_prompt, x_sample, norm1_g, w_in, ssd_conv_w, ssd_conv_b, ssd_dt_bias, ssd_a_log, ssd_d, ssd_norm_g,
           gqa_q_norm_g, gqa_k_norm_g, gla_gate_w2, gla_gate_b, gla_norm_g, mla_q_norm_g, mla_w_uq,
           mla_kv_norm_g, mla_w_ukv, w_out, norm2_g, w_ff1, w_ff2, final_norm_g):
    stacked = (norm1_g, w_in, ssd_conv_w, ssd_conv_b, ssd_dt_bias, ssd_a_log, ssd_d, ssd_norm_g,
               gqa_q_norm_g, gqa_k_norm_g, gla_gate_w2, gla_gate_b, gla_norm_g, mla_q_norm_g, mla_w_uq,
               mla_kv_norm_g, mla_w_ukv, w_out, norm2_g, w_ff1, w_ff2)
    layers = [_layer_params(i, stacked) for i in range(norm1_g.shape[0])]
    return (_trunk(x_prompt, layers, final_norm_g), _trunk(x_sample, layers, final_norm_g))
```
